```python
import functools
import numpy as np
import jax
import jax.numpy as jnp
from jax import lax

D_MODEL = 2048
BATCH = 4
SEQ = 4096
DEPTH = 2

GRID_W = 64
CTX_LEN = 256
D_MIX = D_MODEL
N_MIXERS = 4
GROUP_W = D_MIX // N_MIXERS
CONV_W = 5
EPS = 1e-6
N_MOD = 6
D_FF = -(-8 * D_MODEL // (3 * 256)) * 256

SSD_HEAD_DIM = 64
SSD_HEADS = GROUP_W // SSD_HEAD_DIM
SSD_GROUPS = 2
SSD_HPG = SSD_HEADS // SSD_GROUPS
SSD_STATE = 128
SSD_CHUNK = 128
SSD_SPLIT = (GROUP_W, GROUP_W, SSD_GROUPS * SSD_STATE, SSD_GROUPS * SSD_STATE, 2 * SSD_HEADS)

ML_HEADS = 4
ML_HEAD_DIM = GROUP_W // ML_HEADS
ML_CHUNK = 64
ML_SPLIT = (GROUP_W, GROUP_W, GROUP_W, GROUP_W, 2 * ML_HEADS, 2 * ML_HEADS)

LRU_BLOCKS = 8
LRU_BLOCK_W = GROUP_W // LRU_BLOCKS
LRU_C = 8.0
LRU_SPLIT = (GROUP_W, GROUP_W)

GLA_HEADS = 4
GLA_DK = GROUP_W // (2 * GLA_HEADS)
GLA_DV = GROUP_W // GLA_HEADS
GLA_RANK = 16
GLA_TAU = 16.0
GLA_CHUNK = 64
GLA_SPLIT = (GLA_HEADS * GLA_DK, GLA_HEADS * GLA_DK, GROUP_W, GROUP_W, 2 * GLA_RANK)

MIXER_COLS = (sum(SSD_SPLIT), sum(ML_SPLIT), sum(LRU_SPLIT), sum(GLA_SPLIT))
P_IN = sum(MIXER_COLS)

kernel_name = 'hybrid_ssd_mlstm_rglru_gla_prefix_dit'


def split(a, sizes):
    return jnp.split(a, np.cumsum(sizes)[:-1].tolist(), axis=-1)


def rmsnorm(x, g):
    xf = x.astype(jnp.float32)
    y = xf * lax.rsqrt(jnp.mean(xf * xf, axis=-1, keepdims=True) + EPS)
    return (y * g.astype(jnp.float32)).astype(x.dtype)


def head_rmsnorm(y, g, n_heads):
    bsz, length, width = y.shape
    yh = y.astype(jnp.float32).reshape(bsz, length, n_heads, width // n_heads)
    yh = yh * lax.rsqrt(jnp.mean(yh * yh, axis=-1, keepdims=True) + EPS)
    return yh.reshape(bsz, length, width) * g.astype(jnp.float32)


def modulate(h, shift, scale):
    return h * (1.0 + scale) + shift


def swiglu(h, w_gate, w_up, w_down):
    return (jax.nn.silu(h @ w_gate) * (h @ w_up)) @ w_down


def dwconv(x, w, b):
    ch = x.shape[-1]
    out = lax.conv_general_dilated(x, w.astype(x.dtype)[:, None, :], window_strides=(1,),
                                   padding=[(CONV_W // 2, CONV_W // 2)],
                                   dimension_numbers=('NWC', 'WIO', 'NWC'), feature_group_count=ch)
    return out + b


def to_colmajor(u, rows):
    bsz, length, ch = u.shape
    return u.reshape(bsz, rows, GRID_W, ch).transpose(0, 2, 1, 3).reshape(bsz, length, ch)


def from_colmajor(y, rows):
    bsz, length, ch = y.shape
    return y.reshape(bsz, GRID_W, rows, ch).transpose(0, 2, 1, 3).reshape(bsz, length, ch)


def to_chunks(a, chunk):
    bsz, length = a.shape[:2]
    return jnp.moveaxis(a.reshape(bsz, length // chunk, chunk, *a.shape[2:]), 1, 0)


def from_chunks(a):
    n, bsz, chunk = a.shape[:3]
    return jnp.moveaxis(a, 0, 1).reshape(bsz, n * chunk, *a.shape[3:])


def chunked_scan(step, inputs, state, chunk):
    state, ys = lax.scan(step, state, tuple(to_chunks(a, chunk) for a in inputs))
    return from_chunks(ys), state


def bidir_scan(run, ctx_dirs, lat_dirs, init):
    flip = lambda t: tuple(jnp.flip(a, 1) for a in t)
    yc_f, s_f = run(ctx_dirs[0], init)
    yc_b, s_b = run(flip(ctx_dirs[1]), init)
    yl_f, _ = run(lat_dirs[0], s_f)
    yl_b, _ = run(flip(lat_dirs[1]), s_b)
    return yc_f + jnp.flip(yc_b, 1), yl_f + jnp.flip(yl_b, 1)


def tril_mask(t):
    return jnp.tril(jnp.ones((t, t), dtype=bool))


def ssd_step(s, inp):
    x, bm, cm, dt, loga = inp
    b = jnp.cumsum(loga, axis=1)
    mask = tril_mask(x.shape[1])[None, :, :, None, None]
    seg = jnp.exp(jnp.where(mask, b[:, :, None] - b[:, None], -jnp.inf))
    cb = jnp.einsum('btgn,bsgn->btsg', cm, bm)
    w = cb[..., None] * seg * dt[:, None]
    y = jnp.einsum('btsgh,bsghp->btghp', w, x)
    y = y + jnp.einsum('btgn,bghpn->btghp', cm, s) * jnp.exp(b)[..., None]
    xw = x * (jnp.exp(b[:, -1:] - b) * dt)[..., None]
    s = jnp.exp(b[:, -1])[..., None, None] * s + jnp.einsum('bsghp,bsgn->bghpn', xw, bm)
    return s, y


def ssd_mixer(u_ctx, u_lat, conv_w, conv_b, dt_bias, a_log, d_skip, norm_g, need_ctx):
    neg_a = (-jnp.exp(a_log.astype(jnp.float32))).reshape(2, SSD_GROUPS, SSD_HPG)

    def prep(u):
        bsz, length = u.shape[:2]
        z, xs, bm, cm, dt_raw = split(u, SSD_SPLIT)
        xbc = jax.nn.silu(dwconv(jnp.concatenate([xs, bm, cm], axis=-1), conv_w, conv_b))
        xs, bm, cm = split(xbc, (GROUP_W, SSD_GROUPS * SSD_STATE, SSD_GROUPS * SSD_STATE))
        xs = xs.reshape(bsz, length, SSD_GROUPS, SSD_HPG, SSD_HEAD_DIM)
        bm = bm.reshape(bsz, length, SSD_GROUPS, SSD_STATE)
        cm = cm.reshape(bsz, length, SSD_GROUPS, SSD_STATE)
        dt = jax.nn.softplus(dt_raw.astype(jnp.float32).reshape(bsz, length, 2, SSD_GROUPS, SSD_HPG)
                             + dt_bias.astype(jnp.float32).reshape(2, SSD_GROUPS, SSD_HPG))
        loga = dt * neg_a
        dirs = tuple((xs, bm, cm, dt[:, :, d], loga[:, :, d]) for d in range(2))
        return z, xs, dirs

    z_c, xs_c, dirs_c = prep(u_ctx)
    z_l, xs_l, dirs_l = prep(u_lat)
    init = jnp.zeros((u_lat.shape[0], SSD_GROUPS, SSD_HPG, SSD_HEAD_DIM, SSD_STATE), jnp.float32)
    y_c, y_l = bidir_scan(functools.partial(chunked_scan, ssd_step, chunk=SSD_CHUNK), dirs_c, dirs_l, init)

    def finish(y, xs, z):
        bsz, length = z.shape[:2]
        y = (y + d_skip.reshape(SSD_GROUPS, SSD_HPG, 1) * xs).reshape(bsz, length, GROUP_W)
        return rmsnorm(y * jax.nn.silu(z.astype(jnp.float32)), norm_g).astype(z.dtype)

    return (finish(y_c, xs_c, z_c) if need_ctx else None), finish(y_l, xs_l, z_l)


def mlstm_step(state, inp):
    cs, ns, m = state
    q, k, v, ig, logf = inp
    b = jnp.cumsum(logf, axis=1)
    mask = tril_mask(q.shape[1])[None, :, :, None]
    dmat = jnp.where(mask, b[:, :, None] - b[:, None] + ig[:, None], -jnp.inf)
    inter = b + m[:, None]
    mt = jnp.maximum(inter, jnp.max(dmat, axis=2))
    w = jnp.exp(dmat - mt[:, :, None])
    sc = jnp.exp(inter - mt)
    sw = jnp.einsum('bthd,bshd->btsh', q, k) * w
    num = jnp.einsum('btsh,bshv->bthv', sw, v) + sc[..., None] * jnp.einsum('bhvd,bthd->bthv', cs, q)
    den = jnp.sum(sw, axis=2) + sc * jnp.einsum('bhd,bthd->bth', ns, q)
    y = num / jnp.maximum(jnp.abs(den), jnp.exp(-mt))[..., None]
    bl = b[:, -1]
    tail = bl[:, None] - b + ig
    m_new = jnp.maximum(bl + m, jnp.max(tail, axis=1))
    ws = jnp.exp(tail - m_new[:, None])
    sc_end = jnp.exp(bl + m - m_new)
    cs = sc_end[..., None, None] * cs + jnp.einsum('bshv,bshd->bhvd', v * ws[..., None], k)
    ns = sc_end[..., None] * ns + jnp.einsum('bsh,bshd->bhd', ws, k)
    return (cs, ns, m_new), y


def mlstm_mixer(u_ctx, u_lat, conv_w, conv_b, igate_b, fgate_b, norm_g, need_ctx):
    def prep(u):
        bsz, length = u.shape[:2]
        q, k, v, o, ig, fg = split(u, ML_SPLIT)
        q, k = split(jax.nn.silu(dwconv(jnp.concatenate([q, k], axis=-1), conv_w, conv_b)), (GROUP_W, GROUP_W))
        hs = lambda a: a.reshape(bsz, length, ML_HEADS, ML_HEAD_DIM)
        q, k, v = hs(q), hs(k) * ML_HEAD_DIM ** -0.5, hs(v)
        ig = ig.astype(jnp.float32).reshape(bsz, length, 2, ML_HEADS) + igate_b.astype(jnp.float32)
        logf = jax.nn.log_sigmoid(fg.astype(jnp.float32).reshape(bsz, length, 2, ML_HEADS) + fgate_b.astype(jnp.float32))
        dirs = tuple((q, k, v, ig[:, :, d], logf[:, :, d]) for d in range(2))
        return o, dirs

    o_c, dirs_c = prep(u_ctx)
    o_l, dirs_l = prep(u_lat)
    bsz = u_lat.shape[0]
    init = (jnp.zeros((bsz, ML_HEADS, ML_HEAD_DIM, ML_HEAD_DIM), jnp.float32),
            jnp.zeros((bsz, ML_HEADS, ML_HEAD_DIM), jnp.float32),
            jnp.zeros((bsz, ML_HEADS), jnp.float32))
    y_c, y_l = bidir_scan(functools.partial(chunked_scan, mlstm_step, chunk=ML_CHUNK), dirs_c, dirs_l, init)

    def finish(y, o):
        bsz, length = o.shape[:2]
        y = head_rmsnorm(y.reshape(bsz, length, GROUP_W), norm_g, ML_HEADS)
        return (jax.nn.sigmoid(o.astype(jnp.float32)) * y).astype(o.dtype)

    return (finish(y_c, o_c) if need_ctx else None), finish(y_l, o_l)


def lru_run(inputs, h0):
    a, bx = inputs
    bx = bx.at[:, 0].add(a[:, 0] * h0)
    combine = lambda l, r: (l[0] * r[0], r[0] * l[1] + r[1])
    _, h = lax.associative_scan(combine, (a, bx), axis=1)
    return h, h[:, -1]


def lru_mixer(u_ctx, u_lat, conv_w, conv_b, wa, ba, wx, bx_b, lam, need_ctx):
    log_sig_lam = jax.nn.log_sigmoid(lam.astype(jnp.float32))

    def prep(u):
        bsz, length = u.shape[:2]
        gate, xb = split(u, LRU_SPLIT)
        xf = dwconv(xb, conv_w, conv_b).astype(jnp.float32)
        xblk = xf.reshape(bsz, length, LRU_BLOCKS, LRU_BLOCK_W)
        blockdiag = lambda w, bias: jnp.einsum('blni,dnij->bldnj', xblk, w).reshape(bsz, length, 2, GROUP_W) + bias
        r = jax.nn.sigmoid(blockdiag(wa, ba))
        i = jax.nn.sigmoid(blockdiag(wx, bx_b))
        loga = LRU_C * r * log_sig_lam
        a = jnp.exp(loga)
        bx = jnp.sqrt(-jnp.expm1(2.0 * loga)) * i * xf[:, :, None]
        dirs = tuple((a[:, :, d], bx[:, :, d]) for d in range(2))
        return gate, dirs

    g_c, dirs_c = prep(u_ctx)
    g_l, dirs_l = prep(u_lat)
    init = jnp.zeros((u_lat.shape[0], GROUP_W), jnp.float32)
    h_c, h_l = bidir_scan(lru_run, dirs_c, dirs_l, init)

    def finish(h, gate):
        return (h * jax.nn.gelu(gate.astype(jnp.float32))).astype(gate.dtype)

    return (finish(h_c, g_c) if need_ctx else None), finish(h_l, g_l)


def gla_step(s, inp):
    q, k, v, la = inp
    b = jnp.cumsum(la, axis=1)
    mask = tril_mask(q.shape[1])[None, :, :, None, None]
    dec = jnp.exp(jnp.where(mask, b[:, :, None] - b[:, None], -jnp.inf))
    att = jnp.einsum('btshd,bshd->btsh', dec * q[:, :, None], k)
    o = jnp.einsum('btsh,bshv->bthv', att, v) + jnp.einsum('bthd,bhdv->bthv', q * jnp.exp(b), s)
    bl = b[:, -1]
    s = jnp.exp(bl)[..., None] * s + jnp.einsum('bshd,bshv->bhdv', k * jnp.exp(bl[:, None] - b), v)
    return s, o


def gla_mixer(u_ctx, u_lat, wg2, bg, norm_g, need_ctx):
    def prep(u):
        bsz, length = u.shape[:2]
        q, k, v, r, g1 = split(u, GLA_SPLIT)
        q = q.reshape(bsz, length, GLA_HEADS, GLA_DK) * GLA_DK ** -0.5
        k = k.reshape(bsz, length, GLA_HEADS, GLA_DK)
        v = v.reshape(bsz, length, GLA_HEADS, GLA_DV)
        glog = jnp.einsum('bldr,drk->bldk', g1.reshape(bsz, length, 2, GLA_RANK), wg2) + bg
        la = (jax.nn.log_sigmoid(glog.astype(jnp.float32)) / GLA_TAU).reshape(bsz, length, 2, GLA_HEADS, GLA_DK)
        dirs = tuple((q, k, v, la[:, :, d]) for d in range(2))
        return r, dirs

    r_c, dirs_c = prep(u_ctx)
    r_l, dirs_l = prep(u_lat)
    init = jnp.zeros((u_lat.shape[0], GLA_HEADS, GLA_DK, GLA_DV), jnp.float32)
    o_c, o_l = bidir_scan(functools.partial(chunked_scan, gla_step, chunk=GLA_CHUNK), dirs_c, dirs_l, init)

    def finish(o, r):
        bsz, length = r.shape[:2]
        o = head_rmsnorm(o.reshape(bsz, length, GROUP_W), norm_g, GLA_HEADS)
        return (o * jax.nn.silu(r.astype(jnp.float32))).astype(r.dtype)

    return (finish(o_c, r_c) if need_ctx else None), finish(o_l, r_l)


def setup_inputs(seed: int = 0) -> dict:
    key = jax.random.key(seed)
    keys = iter(jax.random.split(key, 48))
    f32 = jnp.float32
    D = D_MODEL

    def normal(shape, scale):
        return jax.random.normal(next(keys), shape, f32) * scale

    def uniform(shape, lo, hi):
        return jax.random.uniform(next(keys), shape, f32, lo, hi)

    def gain(shape):
        return 1.0 + normal(shape, 0.02)

    x = normal((BATCH, SEQ, D), 1.0)
    c = normal((BATCH, D), 1.0)
    ctx = normal((BATCH, CTX_LEN, D), 1.0)
    c_ctx = normal((D,), 1.0)
    norm1_g = gain((DEPTH, D))
    norm2_g = gain((DEPTH, D))
    w_mod = normal((DEPTH, D, N_MOD * D), 0.5 * D ** -0.5)
    b_mod = normal((DEPTH, N_MOD * D), 0.02)
    w_in = normal((DEPTH, D, P_IN), D ** -0.5)
    w_out = normal((DEPTH, D_MIX, D), D_MIX ** -0.5)
    ssd_ch = GROUP_W + 2 * SSD_GROUPS * SSD_STATE
    ssd_conv_w = normal((DEPTH, CONV_W, ssd_ch), CONV_W ** -0.5)
    ssd_conv_b = normal((DEPTH, ssd_ch), 0.02)
    dt0 = jnp.exp(uniform((DEPTH, 2, SSD_HEADS), float(np.log(1e-3)), float(np.log(1e-1))))
    ssd_dt_bias = dt0 + jnp.log(-jnp.expm1(-dt0))
    ssd_a_log = jnp.log(uniform((DEPTH, 2, SSD_HEADS), 1.0, 16.0))
    ssd_d = 1.0 + normal((DEPTH, SSD_HEADS), 0.1)
    ssd_norm_g = gain((DEPTH, GROUP_W))
    ml_conv_w = normal((DEPTH, CONV_W, 2 * GROUP_W), CONV_W ** -0.5)
    ml_conv_b = normal((DEPTH, 2 * GROUP_W), 0.02)
    ml_igate_b = normal((DEPTH, 2, ML_HEADS), 0.1)
    ml_fgate_b = jnp.linspace(3.0, 6.0, ML_HEADS, dtype=f32) + normal((DEPTH, 2, ML_HEADS), 0.1)
    ml_norm_g = gain((DEPTH, GROUP_W))
    lru_conv_w = normal((DEPTH, CONV_W, GROUP_W), CONV_W ** -0.5)
    lru_conv_b = normal((DEPTH, GROUP_W), 0.02)
    lru_wa = normal((DEPTH, 2, LRU_BLOCKS, LRU_BLOCK_W, LRU_BLOCK_W), LRU_BLOCK_W ** -0.5)
    lru_ba = normal((DEPTH, 2, GROUP_W), 0.02)
    lru_wx = normal((DEPTH, 2, LRU_BLOCKS, LRU_BLOCK_W, LRU_BLOCK_W), LRU_BLOCK_W ** -0.5)
    lru_bx = normal((DEPTH, 2, GROUP_W), 0.02)
    s = uniform((DEPTH, 2, GROUP_W), 0.9, 0.999) ** (1.0 / LRU_C)
    lru_lambda = jnp.log(s) - jnp.log1p(-s)
    gla_wg2 = normal((DEPTH, 2, GLA_RANK, GLA_HEADS * GLA_DK), GLA_RANK ** -0.5)
    gla_bg = normal((DEPTH, 2, GLA_HEADS * GLA_DK), 0.1)
    gla_norm_g = gain((DEPTH, GROUP_W))
    w_gate = normal((DEPTH, D, D_FF), D ** -0.5)
    w_up = normal((DEPTH, D, D_FF), D ** -0.5)
    w_down = normal((DEPTH, D_FF, D), D_FF ** -0.5)
    final_g = gain((D,))
    return {'x': x, 'c': c, 'ctx': ctx, 'c_ctx': c_ctx, 'norm1_g': norm1_g, 'norm2_g': norm2_g,
            'w_mod': w_mod, 'b_mod': b_mod, 'w_in': w_in, 'w_out': w_out,
            'ssd_conv_w': ssd_conv_w, 'ssd_conv_b': ssd_conv_b, 'ssd_dt_bias': ssd_dt_bias,
            'ssd_a_log': ssd_a_log, 'ssd_d': ssd_d, 'ssd_norm_g': ssd_norm_g,
            'ml_conv_w': ml_conv_w, 'ml_conv_b': ml_conv_b, 'ml_igate_b': ml_igate_b,
            'ml_fgate_b': ml_fgate_b, 'ml_norm_g': ml_norm_g,
            'lru_conv_w': lru_conv_w, 'lru_conv_b': lru_conv_b, 'lru_wa': lru_wa, 'lru_ba': lru_ba,
            'lru_wx': lru_wx, 'lru_bx': lru_bx, 'lru_lambda': lru_lambda,
            'gla_wg2': gla_wg2, 'gla_bg': gla_bg, 'gla_norm_g': gla_norm_g,
            'w_gate': w_gate, 'w_up': w_up, 'w_down': w_down, 'final_g': final_g}


def reference(x, c, ctx, c_ctx, norm1_g, norm2_g, w_mod, b_mod, w_in, w_out,
              ssd_conv_w, ssd_conv_b, ssd_dt_bias, ssd_a_log, ssd_d, ssd_norm_g,
              ml_conv_w, ml_conv_b, ml_igate_b, ml_fgate_b, ml_norm_g,
              lru_conv_w, lru_conv_b, lru_wa, lru_ba, lru_wx, lru_bx, lru_lambda,
              gla_wg2, gla_bg, gla_norm_g, w_gate, w_up, w_down, final_g):
    length = x.shape[1]
    rows = length // GRID_W
    silu_c = jax.nn.silu(c)
    silu_cc = jax.nn.silu(c_ctx)
    xl, xt = x, ctx
    for i in range(DEPTH):
        need_ctx = i < DEPTH - 1
        m_l = jnp.split((silu_c @ w_mod[i] + b_mod[i])[:, None, :], N_MOD, axis=-1)
        m_t = jnp.split(silu_cc @ w_mod[i] + b_mod[i], N_MOD, axis=-1)
        hl = modulate(rmsnorm(xl, norm1_g[i]), m_l[0], m_l[1])
        ht = modulate(rmsnorm(xt, norm1_g[i]), m_t[0], m_t[1])
        ul = split(hl @ w_in[i], MIXER_COLS)
        ut = split(ht @ w_in[i], MIXER_COLS)
        ya_t, ya_l = ssd_mixer(ut[0], ul[0], ssd_conv_w[i], ssd_conv_b[i], ssd_dt_bias[i], ssd_a_log[i],
                               ssd_d[i], ssd_norm_g[i], need_ctx)
        yb_t, yb_l = mlstm_mixer(ut[1], to_colmajor(ul[1], rows), ml_conv_w[i], ml_conv_b[i],
                                 ml_igate_b[i], ml_fgate_b[i], ml_norm_g[i], need_ctx)
        yb_l = from_colmajor(yb_l, rows)
        yc_t, yc_l = lru_mixer(ut[2], ul[2], lru_conv_w[i], lru_conv_b[i], lru_wa[i], lru_ba[i],
                               lru_wx[i], lru_bx[i], lru_lambda[i], need_ctx)
        yd_t, yd_l = gla_mixer(ut[3], to_colmajor(ul[3], rows), gla_wg2[i], gla_bg[i], gla_norm_g[i], need_ctx)
        yd_l = from_colmajor(yd_l, rows)
        xl = xl + m_l[2] * (jnp.concatenate([ya_l, yb_l, yc_l, yd_l], axis=-1) @ w_out[i])
        xl = xl + m_l[5] * swiglu(modulate(rmsnorm(xl, norm2_g[i]), m_l[3], m_l[4]), w_gate[i], w_up[i], w_down[i])
        if need_ctx:
            xt = xt + m_t[2] * (jnp.concatenate([ya_t, yb_t, yc_t, yd_t], axis=-1) @ w_out[i])
            xt = xt + m_t[5] * swiglu(modulate(rmsnorm(xt, norm2_g[i]), m_t[3], m_t[4]), w_gate[i], w_up[i], w_down[i])
    return rmsnorm(xl, final_g)
```

```python
import functools

import jax
import jax.numpy as jnp
from jax import lax
from jax.experimental import pallas as pl
from jax.experimental.pallas import tpu as pltpu

F32 = jnp.float32
BF16 = jnp.bfloat16
EPS = 1e-6
NEG = -1e30

D_MODEL = 2048
GRID_W = 64
GROUP_W = 512
N_MOD = 6
CONV_W = 5
HALO = 8

SSD_HEADS = 8
SSD_CHUNK = 128
ML_HEADS = 4
ML_CHUNK = 64
LRU_BLOCKS = 8
LRU_C = 8.0
LRU_CHUNK = 128
GLA_HEADS = 4
GLA_DK = 64
GLA_RANK = 16
GLA_TAU = 16.0
GLA_CHUNK = 64
GLA_SUB = 16

OFF_SSD_XBC = 0
OFF_LRU_X = 1024
OFF_LRU_GATE = 1536
OFF_ML_Q = 2048
OFF_ML_K = 2560
OFF_ML_V = 3072
OFF_ML_O = 3584
OFF_GLA_QK = 4096
OFF_GLA_V = 4608
OFF_GLA_R = 5120
OFF_SSD_Z = 5632
OFF_SSD_DT = 6144
OFF_ML_G = 6272
OFF_GLA_G1 = 6400
N_TOT = 6656

VMEM_LIMIT = 56 * 1024 * 1024


def _cparams(sem):
    return pltpu.CompilerParams(dimension_semantics=sem, vmem_limit_bytes=VMEM_LIMIT)


def _sigmoid(x):
    return 1.0 / (1.0 + jnp.exp(-x))


def _silu(x):
    return x * _sigmoid(x)


def _softplus(x):
    return jnp.maximum(x, 0.0) + jnp.log1p(jnp.exp(-jnp.abs(x)))


def _log_sigmoid(x):
    return jnp.minimum(x, 0.0) - jnp.log1p(jnp.exp(-jnp.abs(x)))


def _gelu_tanh(x):
    return 0.5 * x * (1.0 + jnp.tanh(0.7978845608028654 * (x + 0.044715 * (x * x * x))))


def _rms(x):
    return x * lax.rsqrt(jnp.mean(x * x, axis=-1, keepdims=True) + EPS)


def _dot(a, b):
    return jnp.dot(a, b, preferred_element_type=F32)


def _dot_nt(a, b):
    return lax.dot_general(a, b, (((1,), (1,)), ((), ())), preferred_element_type=F32)


def _dot_exact(a, b):
    return jnp.dot(a, b, precision=lax.Precision.HIGHEST, preferred_element_type=F32)


def _scan_mask(t, rev):
    ri = lax.broadcasted_iota(jnp.int32, (t, t), 0)
    ci = lax.broadcasted_iota(jnp.int32, (t, t), 1)
    return (ci >= ri) if rev else (ci <= ri)


def _conv5(prev, main, nxt, w, bias, first, last):
    t = main.shape[0]
    prev = jnp.where(first, 0.0, prev)
    nxt = jnp.where(last, 0.0, nxt)
    ext = jnp.concatenate([prev, main, nxt], axis=0)
    n = t + 2 * HALO
    acc = None
    for k in range(CONV_W):
        sh = (CONV_W // 2 - k) % n
        r = ext if sh == 0 else pltpu.roll(ext, sh, axis=0)
        term = r[HALO:HALO + t] * w[k:k + 1, :]
        acc = term if acc is None else acc + term
    return acc + bias


def _mod_kernel(c_ref, w_ref, b_ref, o_ref):
    s = _silu(c_ref[...]).astype(BF16)
    o_ref[...] = _dot(s, w_ref[...].astype(BF16)) + b_ref[...]


def _inproj_kernel(x_ref, g_ref, sh_ref, sc_ref, w_ref, o_ref, h_scr):
    @pl.when(pl.program_id(1) == 0)
    def _():
        y = _rms(x_ref[...]) * g_ref[...]
        h_scr[...] = (y * (1.0 + sc_ref[0]) + sh_ref[0]).astype(BF16)

    o_ref[...] = _dot(h_scr[...], w_ref[...])


def _outproj_kernel(ya_ref, yb_ref, yc_ref, yd_ref, w_ref, x_ref, gate_ref, g_ref, sh_ref, sc_ref,
                    xo_ref, h_ref):
    acc = _dot(ya_ref[...], w_ref[0])
    acc = acc + _dot(yb_ref[...], w_ref[1])
    acc = acc + _dot(yc_ref[...], w_ref[2])
    acc = acc + _dot(yd_ref[...], w_ref[3])
    xn = x_ref[...] + gate_ref[0] * acc
    xo_ref[...] = xn
    y = _rms(xn) * g_ref[...]
    h_ref[...] = (y * (1.0 + sc_ref[0]) + sh_ref[0]).astype(BF16)


def _ffn_up_kernel(h_ref, wg_ref, wu_ref, o_ref):
    h = h_ref[...]
    o_ref[...] = (_silu(_dot(h, wg_ref[...])) * _dot(h, wu_ref[...])).astype(BF16)


def _ffn_down_kernel(final, nk, a_ref, w_ref, x_ref, gate_ref, fg_ref, o_ref, acc_ref):
    k = pl.program_id(1)

    @pl.when(k == 0)
    def _():
        acc_ref[...] = jnp.zeros_like(acc_ref)

    acc_ref[...] += _dot(a_ref[...], w_ref[...])

    @pl.when(k == nk - 1)
    def _():
        xn = x_ref[...] + gate_ref[0] * acc_ref[...]
        if final:
            xn = _rms(xn) * fg_ref[...]
        o_ref[...] = xn


def _ssd_kernel(rev, finish, nchunks, *refs):
    if finish:
        (xbc_ref, xp_ref, xn_ref, z_ref, dt_ref, yp_ref, s0_ref, cw_ref, cb_ref, dtb_ref, alog_ref,
         dsk_ref, ng_ref, y_ref, s_ref) = refs
    else:
        (xbc_ref, xp_ref, xn_ref, z_ref, dt_ref, s0_ref, cw_ref, cb_ref, dtb_ref, alog_ref,
         dsk_ref, ng_ref, y_ref, s_ref) = refs
    t = SSD_CHUNK
    d = 1 if rev else 0
    j = pl.program_id(1)
    c = (nchunks - 1 - j) if rev else j

    @pl.when(j == 0)
    def _():
        s_ref[...] = s0_ref[...]

    xbc = _silu(_conv5(xp_ref[0], xbc_ref[0], xn_ref[0], cw_ref[...], cb_ref[...], c == 0, c == nchunks - 1))
    dt_all = _softplus(dt_ref[0] + dtb_ref[...])
    loga = dt_all * (-jnp.exp(alog_ref[...]))
    mask = _scan_mask(t, rev)
    b_all = _dot_exact(mask.astype(F32), loga)
    b_t = b_all.T
    dt_t = dt_all.T
    er = 0 if rev else t - 1
    lo = lax.broadcasted_iota(jnp.int32, (t, 128), 1) < 64
    lo_rows = lax.broadcasted_iota(jnp.int32, (128, 128), 0) < 64

    ys = []
    for p in range(SSD_HEADS // 2):
        g = p // 2
        bm = xbc[:, 512 + 128 * g:640 + 128 * g].astype(BF16)
        cm = xbc[:, 768 + 128 * g:896 + 128 * g].astype(BF16)
        cb = _dot_nt(cm, bm)
        xpair = xbc[:, 128 * p:128 * p + 128]
        xpair_b = xpair.astype(BF16)
        yh, eb, coef, dec = [], [], [], []
        for hh in range(2):
            l = d * SSD_HEADS + 2 * p + hh
            bcol = b_all[:, l:l + 1]
            seg = jnp.exp(jnp.where(mask, bcol - b_t[l:l + 1, :], NEG))
            w = (cb * seg * dt_t[l:l + 1, :]).astype(BF16)
            yh.append(_dot(w, xpair_b))
            eb.append(jnp.exp(bcol))
            bl = b_all[er:er + 1, l:l + 1]
            coef.append(jnp.exp(bl - bcol) * dt_all[:, l:l + 1])
            dec.append(jnp.exp(bl))
        s_pair = s_ref[0, p]
        y_inter = _dot_nt(cm, s_pair.astype(BF16))
        ys.append(jnp.where(lo, yh[0], yh[1]) + y_inter * jnp.where(lo, eb[0], eb[1]))
        xw = xpair * jnp.where(lo, coef[0], coef[1])
        s_ref[0, p] = jnp.where(lo_rows, dec[0], dec[1]) * s_pair + _dot(xw.T.astype(BF16), bm)
    y = jnp.concatenate(ys, axis=1)
    if finish:
        y = y + yp_ref[0] + dsk_ref[...] * xbc[:, :GROUP_W]
        y = _rms(y * _silu(z_ref[0])) * ng_ref[...]
        y_ref[0] = y.astype(y_ref.dtype)
    else:
        y_ref[0] = y


def _mlstm_kernel(rev, finish, nchunks, *refs):
    if finish:
        (q_ref, qp_ref, qn_ref, k_ref, kp_ref, kn_ref, v_ref, o_ref, g_ref, yp_ref, cs0_ref, sm0_ref,
         cw_ref, cb_ref, gb_ref, ng_ref, y_ref, cs_ref, sm_ref) = refs
    else:
        (q_ref, qp_ref, qn_ref, k_ref, kp_ref, kn_ref, v_ref, o_ref, g_ref, cs0_ref, sm0_ref,
         cw_ref, cb_ref, gb_ref, ng_ref, y_ref, cs_ref, sm_ref) = refs
    t = ML_CHUNK
    d = 1 if rev else 0
    j = pl.program_id(1)
    c = (nchunks - 1 - j) if rev else j
    first, last = c == 0, c == nchunks - 1

    @pl.when(j == 0)
    def _():
        cs_ref[...] = cs0_ref[...]
        sm_ref[...] = sm0_ref[...]

    cw = cw_ref[...]
    cbias = cb_ref[...]
    q = _silu(_conv5(qp_ref[0], q_ref[0], qn_ref[0], cw[:, :GROUP_W], cbias[:, :GROUP_W], first, last))
    k = _silu(_conv5(kp_ref[0], k_ref[0], kn_ref[0], cw[:, GROUP_W:], cbias[:, GROUP_W:], first, last))
    k = k * (128.0 ** -0.5)
    v = v_ref[0]
    gts = g_ref[0] + gb_ref[...]
    mask = _scan_mask(t, rev)
    b_all = _dot_exact(mask.astype(F32), _log_sigmoid(gts))
    g_t = gts.T
    b_t = b_all.T
    er = 0 if rev else t - 1

    ys = []
    for h in range(ML_HEADS):
        li = d * ML_HEADS + h
        lf = 2 * ML_HEADS + li
        sl = slice(128 * h, 128 * h + 128)
        bcol = b_all[:, lf:lf + 1]
        igcol = gts[:, li:li + 1]
        m_prev = sm_ref[0, ML_HEADS + h:ML_HEADS + h + 1, 0:1]
        ns = sm_ref[0, h:h + 1, :]
        cs = cs_ref[0, h]
        dmat = jnp.where(mask, bcol - b_t[lf:lf + 1, :] + g_t[li:li + 1, :], NEG)
        inter = bcol + m_prev
        mt = jnp.maximum(inter, jnp.max(dmat, axis=1, keepdims=True))
        w = jnp.exp(dmat - mt)
        sc = jnp.exp(inter - mt)
        qh = q[:, sl]
        kh = k[:, sl]
        vh = v[:, sl]
        qb = qh.astype(BF16)
        kb = kh.astype(BF16)
        sw = _dot_nt(qb, kb) * w
        num = _dot(sw.astype(BF16), vh.astype(BF16)) + sc * _dot_nt(qb, cs.astype(BF16))
        den = jnp.sum(sw, axis=1, keepdims=True) + sc * jnp.sum(qh * ns, axis=1, keepdims=True)
        ys.append(num / jnp.maximum(jnp.abs(den), jnp.exp(-mt)))
        bl = b_all[er:er + 1, lf:lf + 1]
        tail = bl - bcol + igcol
        m_new = jnp.maximum(bl + m_prev, jnp.max(tail, axis=0, keepdims=True))
        ws = jnp.exp(tail - m_new)
        sc_end = jnp.exp(bl + m_prev - m_new)
        cs_ref[0, h] = sc_end * cs + _dot((vh * ws).T.astype(BF16), kb)
        sm_ref[0, h:h + 1, :] = sc_end * ns + jnp.sum(ws * kh, axis=0, keepdims=True)
        sm_ref[0, ML_HEADS + h:ML_HEADS + h + 1, :] = jnp.broadcast_to(m_new, (1, 128))
    if finish:
        yp = yp_ref[0]
        outs = [_rms(ys[h] + yp[:, 128 * h:128 * h + 128]) for h in range(ML_HEADS)]
        y = jnp.concatenate(outs, axis=1) * ng_ref[...]
        y_ref[0] = (_sigmoid(o_ref[0]) * y).astype(y_ref.dtype)
    else:
        y_ref[0] = jnp.concatenate(ys, axis=1)


def _lru_kernel(rev, finish, nchunks, *refs):
    if finish:
        (x_ref, xp_ref, xn_ref, gate_ref, yp_ref, h0_ref, cw_ref, cb_ref, w_ref, bias_ref, lam_ref,
         y_ref, h_ref) = refs
    else:
        (x_ref, xp_ref, xn_ref, gate_ref, h0_ref, cw_ref, cb_ref, w_ref, bias_ref, lam_ref,
         y_ref, h_ref) = refs
    t = LRU_CHUNK
    d = 1 if rev else 0
    j = pl.program_id(1)
    c = (nchunks - 1 - j) if rev else j

    @pl.when(j == 0)
    def _():
        h_ref[...] = h0_ref[...]

    xf = _conv5(xp_ref[0], x_ref[0], xn_ref[0], cw_ref[...], cb_ref[...], c == 0, c == nchunks - 1)
    pre = _dot(xf.astype(BF16), w_ref[d]) + bias_ref[d]
    r = _sigmoid(pre[:, :GROUP_W])
    ig = _sigmoid(pre[:, GROUP_W:])
    loga = LRU_C * r * _log_sigmoid(lam_ref[d])
    a = jnp.exp(loga)
    bx = jnp.sqrt(-jnp.tanh(loga) * (1.0 + a * a)) * ig * xf

    sub = lax.broadcasted_iota(jnp.int32, (8, GROUP_W), 0)
    n = t // 8
    a_loc, h_loc = [], []
    for i in range(n):
        ai = a[8 * i:8 * i + 8]
        hi = bx[8 * i:8 * i + 8]
        for sh in (1, 2, 4):
            if rev:
                valid = sub < 8 - sh
                a_s = pltpu.roll(ai, 8 - sh, axis=0)
                h_s = pltpu.roll(hi, 8 - sh, axis=0)
            else:
                valid = sub >= sh
                a_s = pltpu.roll(ai, sh, axis=0)
                h_s = pltpu.roll(hi, sh, axis=0)
            hi = jnp.where(valid, ai * h_s + hi, hi)
            ai = jnp.where(valid, ai * a_s, ai)
        a_loc.append(ai)
        h_loc.append(hi)
    carry = h_ref[0]
    outs = [None] * n
    for i in (range(n - 1, -1, -1) if rev else range(n)):
        hi = h_loc[i] + a_loc[i] * carry
        outs[i] = hi
        carry = hi[0:1] if rev else hi[7:8]
    h_ref[0] = carry
    hs = jnp.concatenate(outs, axis=0)
    if finish:
        y_ref[0] = ((hs + yp_ref[0]) * _gelu_tanh(gate_ref[0])).astype(y_ref.dtype)
    else:
        y_ref[0] = hs


def _gla_kernel(rev, finish, nchunks, *refs):
    if finish:
        (qk_ref, v_ref, r_ref, g1_ref, yp_ref, s0_ref, wg_ref, bg_ref, ng_ref, y_ref, s_ref) = refs
    else:
        (qk_ref, v_ref, r_ref, g1_ref, s0_ref, wg_ref, bg_ref, ng_ref, y_ref, s_ref) = refs
    t = GLA_CHUNK
    d = 1 if rev else 0
    dkk = GLA_HEADS * GLA_DK
    j = pl.program_id(1)

    @pl.when(j == 0)
    def _():
        s_ref[...] = s0_ref[...]

    qk = qk_ref[0]
    q = qk[:, :dkk] * (GLA_DK ** -0.5)
    k = qk[:, dkk:]
    v = v_ref[0]
    vb = v.astype(BF16)
    glog = _dot(g1_ref[0].astype(BF16), wg_ref[d]) + bg_ref[d]
    la = _log_sigmoid(glog) * (1.0 / GLA_TAU)
    mask = _scan_mask(t, rev)
    b = _dot_exact(mask.astype(F32), la)
    excl = b - la
    rows = lax.broadcasted_iota(jnp.int32, (t, 1), 0)
    lane_head = lax.broadcasted_iota(jnp.int32, (GLA_SUB, dkk), 1) // GLA_DK

    att_blocks = []
    for i in range(t // GLA_SUB):
        r0 = GLA_SUB * i
        if rev:
            ref_row = excl[r0 + GLA_SUB - 1:r0 + GLA_SUB]
            kvalid = rows >= r0
        else:
            ref_row = excl[r0:r0 + 1]
            kvalid = rows < r0 + GLA_SUB
        qi = q[r0:r0 + GLA_SUB] * jnp.exp(b[r0:r0 + GLA_SUB] - ref_row)
        ki = k * jnp.exp(jnp.where(kvalid, ref_row - b, NEG))
        qs = jnp.concatenate([jnp.where(lane_head == h, qi, 0.0) for h in range(GLA_HEADS)], axis=0)
        att_blocks.append(_dot_nt(qs.astype(BF16), ki.astype(BF16)))
    outs = []
    for h in range(GLA_HEADS):
        att = jnp.concatenate([blk[GLA_SUB * h:GLA_SUB * h + GLA_SUB] for blk in att_blocks], axis=0)
        att = jnp.where(mask, att, 0.0)
        outs.append(_dot(att.astype(BF16), vb[:, 128 * h:128 * h + 128]))
    s_t = s_ref[0]
    o = jnp.concatenate(outs, axis=1) + _dot_nt((q * jnp.exp(b)).astype(BF16), s_t.astype(BF16))

    er = 0 if rev else t - 1
    bl = b[er:er + 1]
    kd = (k * jnp.exp(bl - b)).astype(BF16)
    v_t = jnp.concatenate([v[:, 128 * h:128 * h + 128].T for h in range(GLA_HEADS)], axis=0)
    upd = _dot(v_t.astype(BF16), kd)
    blk = (lax.broadcasted_iota(jnp.int32, (GROUP_W, dkk), 0) // 128
           == lax.broadcasted_iota(jnp.int32, (GROUP_W, dkk), 1) // GLA_DK)
    s_ref[0] = s_t * jnp.exp(bl) + jnp.where(blk, upd, 0.0)

    if finish:
        o = o + yp_ref[0]
        o = jnp.concatenate([_rms(o[:, 128 * h:128 * h + 128]) for h in range(GLA_HEADS)], axis=1)
        y_ref[0] = (o * ng_ref[...] * _silu(r_ref[0])).astype(y_ref.dtype)
    else:
        y_ref[0] = o


def _chunk_specs(width, off, t, length, colmajor, rev, nchunks, halo):
    nblk = N_TOT // width
    ob = off // width
    tb = t // HALO

    def cj(j):
        return (nchunks - 1 - j) if rev else j

    if colmajor:
        main = pl.BlockSpec((1, t, width), lambda b, j: (b, 0, cj(j) * nblk + ob))
        prev = pl.BlockSpec((1, HALO, width), lambda b, j: (b, tb - 1, jnp.maximum(cj(j) - 1, 0) * nblk + ob))
        nxt = pl.BlockSpec((1, HALO, width), lambda b, j: (b, 0, jnp.minimum(cj(j) + 1, nchunks - 1) * nblk + ob))
    else:
        main = pl.BlockSpec((1, t, width), lambda b, j: (b, cj(j), ob))
        prev = pl.BlockSpec((1, HALO, width), lambda b, j: (b, jnp.maximum(cj(j) * tb - 1, 0), ob))
        nxt = pl.BlockSpec((1, HALO, width),
                           lambda b, j: (b, jnp.minimum((cj(j) + 1) * tb, length // HALO - 1), ob))
    return [main, prev, nxt] if halo else [main]


def _y_spec(t, colmajor, rev, nchunks):
    def cj(j):
        return (nchunks - 1 - j) if rev else j

    if colmajor:
        return pl.BlockSpec((1, t, GROUP_W), lambda b, j: (b, 0, cj(j)))
    return pl.BlockSpec((1, t, GROUP_W), lambda b, j: (b, cj(j), 0))


def _full_spec(a):
    nd = a.ndim
    return pl.BlockSpec(a.shape, lambda b, j, _nd=nd: (0,) * _nd)


def _state_spec(a):
    nd = a.ndim
    return pl.BlockSpec((1,) + a.shape[1:], lambda b, j, _nd=nd: (b,) + (0,) * (_nd - 1))


def _run_mixer(body, pieces, t, u, length, colmajor, rev, finish, ypart, states, params, name):
    bsz = u.shape[0]
    nchunks = length // t
    if colmajor:
        u_view = u.reshape(bsz, t, nchunks * N_TOT)
        y_shape = (bsz, t, nchunks * GROUP_W)
    else:
        u_view = u
        y_shape = (bsz, length, GROUP_W)
    in_specs, args = [], []
    for width, off, halo in pieces:
        sp = _chunk_specs(width, off, t, length, colmajor, rev, nchunks, halo)
        in_specs += sp
        args += [u_view] * len(sp)
    if finish:
        in_specs.append(_y_spec(t, colmajor, rev, nchunks))
        args.append(ypart.reshape(y_shape))
    for s in states:
        in_specs.append(_state_spec(s))
        args.append(s)
    for p in params:
        in_specs.append(_full_spec(p))
        args.append(p)
    out_shape = [jax.ShapeDtypeStruct(y_shape, BF16 if finish else F32)]
    out_specs = [_y_spec(t, colmajor, rev, nchunks)]
    for s in states:
        out_shape.append(jax.ShapeDtypeStruct(s.shape, s.dtype))
        out_specs.append(_state_spec(s))
    res = pl.pallas_call(
        functools.partial(body, rev, finish, nchunks),
        grid=(bsz, nchunks),
        in_specs=in_specs,
        out_specs=out_specs,
        out_shape=out_shape,
        compiler_params=_cparams(("parallel", "arbitrary")),
        name=name,
    )(*args)
    return res[0].reshape(bsz, length, GROUP_W), list(res[1:])


def _bidir_mixer(body, pieces, t, u_ctx, u_lat, lat_colmajor, zero_states, params, name):
    lc, ll = u_ctx.shape[1], u_lat.shape[1]
    yb_c, st_b = _run_mixer(body, pieces, t, u_ctx, lc, False, True, False, None, zero_states, params,
                            name + "_ctx_bwd")
    y_c, st_f = _run_mixer(body, pieces, t, u_ctx, lc, False, False, True, yb_c, zero_states, params,
                           name + "_ctx_fwd")
    yb_l, _ = _run_mixer(body, pieces, t, u_lat, ll, lat_colmajor, True, False, None, st_b, params,
                         name + "_lat_bwd")
    y_l, _ = _run_mixer(body, pieces, t, u_lat, ll, lat_colmajor, False, True, yb_l, st_f, params,
                        name + "_lat_fwd")
    return y_c, y_l


def _mods(cvec, w_mod, b_mod):
    depth, dm, nm = w_mod.shape
    tn = 1024
    return pl.pallas_call(
        _mod_kernel,
        grid=(depth, nm // tn),
        in_specs=[pl.BlockSpec((8, dm), lambda l, j: (0, 0)),
                  pl.BlockSpec((None, dm, tn), lambda l, j: (l, 0, j)),
                  pl.BlockSpec((None, 1, tn), lambda l, j: (l, 0, j))],
        out_specs=pl.BlockSpec((None, 8, tn), lambda l, j: (l, 0, j)),
        out_shape=jax.ShapeDtypeStruct((depth, 8, nm), F32),
        compiler_params=_cparams(("parallel", "parallel")),
        name="mods",
    )(cvec, w_mod, b_mod.reshape(depth, 1, nm))


def _inproj(x, g, shift, scale, w, rows_per_group, tm):
    m, dm = x.shape
    tn = 512
    grp = lambda i, j: (i * tm // rows_per_group, 0, 0)
    return pl.pallas_call(
        _inproj_kernel,
        grid=(m // tm, N_TOT // tn),
        in_specs=[pl.BlockSpec((tm, dm), lambda i, j: (i, 0)),
                  pl.BlockSpec((1, dm), lambda i, j: (0, 0)),
                  pl.BlockSpec((1, 1, dm), grp),
                  pl.BlockSpec((1, 1, dm), grp),
                  pl.BlockSpec((dm, tn), lambda i, j: (0, j))],
        out_specs=pl.BlockSpec((tm, tn), lambda i, j: (i, j)),
        out_shape=jax.ShapeDtypeStruct((m, N_TOT), F32),
        scratch_shapes=[pltpu.VMEM((tm, dm), BF16)],
        compiler_params=_cparams(("parallel", "arbitrary")),
        name="inproj",
    )(x, g, shift, scale, w)


def _outproj(ys, w, x, gate, g, shift, scale, rows_per_group, tm):
    m, dm = x.shape
    grp = lambda i: (i * tm // rows_per_group, 0, 0)
    yspec = pl.BlockSpec((tm, GROUP_W), lambda i: (i, 0))
    return pl.pallas_call(
        _outproj_kernel,
        grid=(m // tm,),
        in_specs=[yspec, yspec, yspec, yspec,
                  pl.BlockSpec((4, GROUP_W, dm), lambda i: (0, 0, 0)),
                  pl.BlockSpec((tm, dm), lambda i: (i, 0)),
                  pl.BlockSpec((1, 1, dm), grp),
                  pl.BlockSpec((1, dm), lambda i: (0, 0)),
                  pl.BlockSpec((1, 1, dm), grp),
                  pl.BlockSpec((1, 1, dm), grp)],
        out_specs=[pl.BlockSpec((tm, dm), lambda i: (i, 0)), pl.BlockSpec((tm, dm), lambda i: (i, 0))],
        out_shape=[jax.ShapeDtypeStruct((m, dm), F32), jax.ShapeDtypeStruct((m, dm), BF16)],
        compiler_params=_cparams(("parallel",)),
        name="outproj",
    )(*ys, w, x, gate, g, shift, scale)


def _ffn_up(h, wg, wu, tm):
    m, dm = h.shape
    dff = wg.shape[1]
    tn = 512
    return pl.pallas_call(
        _ffn_up_kernel,
        grid=(m // tm, dff // tn),
        in_specs=[pl.BlockSpec((tm, dm), lambda i, j: (i, 0)),
                  pl.BlockSpec((dm, tn), lambda i, j: (0, j)),
                  pl.BlockSpec((dm, tn), lambda i, j: (0, j))],
        out_specs=pl.BlockSpec((tm, tn), lambda i, j: (i, j)),
        out_shape=jax.ShapeDtypeStruct((m, dff), BF16),
        compiler_params=_cparams(("parallel", "arbitrary")),
        name="ffn_up",
    )(h, wg, wu)


def _ffn_down(a, w, x, gate, final_g, final, rows_per_group, tm):
    m, dm = x.shape
    dff = a.shape[1]
    tk = 512
    nk = dff // tk
    grp = lambda i, k: (i * tm // rows_per_group, 0, 0)
    return pl.pallas_call(
        functools.partial(_ffn_down_kernel, final, nk),
        grid=(m // tm, nk),
        in_specs=[pl.BlockSpec((tm, tk), lambda i, k: (i, k)),
                  pl.BlockSpec((tk, dm), lambda i, k: (k, 0)),
                  pl.BlockSpec((tm, dm), lambda i, k: (i, 0)),
                  pl.BlockSpec((1, 1, dm), grp),
                  pl.BlockSpec((1, dm), lambda i, k: (0, 0))],
        out_specs=pl.BlockSpec((tm, dm), lambda i, k: (i, 0)),
        out_shape=jax.ShapeDtypeStruct((m, dm), F32),
        scratch_shapes=[pltpu.VMEM((tm, dm), F32)],
        compiler_params=_cparams(("parallel", "arbitrary")),
        name="ffn_down",
    )(a, w, x, gate, final_g)


def _pad_cols(a, width):
    return jnp.pad(a, ((0, 0), (0, width - a.shape[1])))


def _permute_w_in(w):
    ssd, ml, lru, gla = jnp.split(w, [1552, 1552 + 2064, 1552 + 2064 + 1024], axis=1)
    ssd_z, ssd_xbc, ssd_dt = ssd[:, :512], ssd[:, 512:1536], ssd[:, 1536:]
    ml_q, ml_k, ml_v, ml_o, ml_g = ml[:, :512], ml[:, 512:1024], ml[:, 1024:1536], ml[:, 1536:2048], ml[:, 2048:]
    lru_gate, lru_x = lru[:, :512], lru[:, 512:]
    gla_qk, gla_v, gla_r, gla_g1 = gla[:, :512], gla[:, 512:1024], gla[:, 1024:1536], gla[:, 1536:]
    cols = [ssd_xbc, lru_x, lru_gate, ml_q, ml_k, ml_v, ml_o, gla_qk, gla_v, gla_r, ssd_z,
            _pad_cols(ssd_dt, 128), _pad_cols(ml_g, 128), _pad_cols(gla_g1, 128),
            jnp.zeros((w.shape[0], N_TOT - 6528), w.dtype)]
    return jnp.concatenate(cols, axis=1).astype(BF16)


def _row128(a):
    a = a.reshape(1, -1).astype(F32)
    return _pad_cols(a, 128)


def _block_diag(w):
    nb, bi, bj = w.shape
    eye = jnp.eye(nb, dtype=w.dtype)
    return (eye[:, None, :, None] * w[:, :, None, :]).reshape(nb * bi, nb * bj)


def kernel(x, c, ctx, c_ctx, norm1_g, norm2_g, w_mod, b_mod, w_in, w_out, ssd_conv_w, ssd_conv_b, ssd_dt_bias, ssd_a_log, ssd_d, ssd_norm_g, ml_conv_w, ml_conv_b, ml_igate_b, ml_fgate_b, ml_norm_g, lru_conv_w, lru_conv_b, lru_wa, lru_ba, lru_wx, lru_bx, lru_lambda, gla_wg2, gla_bg, gla_norm_g, w_gate, w_up, w_down, final_g):
    bsz, length, dm = x.shape
    lctx = ctx.shape[1]
    depth = w_in.shape[0]
    assert dm == D_MODEL and length == GRID_W * ML_CHUNK and ML_CHUNK == GLA_CHUNK
    assert lctx % SSD_CHUNK == 0 and length % SSD_CHUNK == 0

    cvec = jnp.concatenate([c, c_ctx[None, :], jnp.zeros((8 - bsz - 1, dm), F32)], axis=0)
    mods = _mods(cvec, w_mod, b_mod)

    xl = x.reshape(bsz * length, dm)
    xt = ctx.reshape(bsz * lctx, dm)
    row = lambda a: a.reshape(1, -1).astype(F32)

    for i in range(depth):
        need_ctx = i < depth - 1
        m_l = [mods[i, :bsz, k * dm:(k + 1) * dm].reshape(bsz, 1, dm) for k in range(N_MOD)]
        m_t = [mods[i, bsz:bsz + 1, k * dm:(k + 1) * dm].reshape(1, 1, dm) for k in range(N_MOD)]
        w_in_p = _permute_w_in(w_in[i])
        ul = _inproj(xl, row(norm1_g[i]), m_l[0], m_l[1], w_in_p, length, 512).reshape(bsz, length, N_TOT)
        ut = _inproj(xt, row(norm1_g[i]), m_t[0], m_t[1], w_in_p, bsz * lctx, 256).reshape(bsz, lctx, N_TOT)

        ssd_params = [ssd_conv_w[i], row(ssd_conv_b[i]), _row128(ssd_dt_bias[i]), _row128(ssd_a_log[i]),
                      row(jnp.repeat(ssd_d[i], GROUP_W // SSD_HEADS)), row(ssd_norm_g[i])]
        ssd_pieces = [(1024, OFF_SSD_XBC, True), (512, OFF_SSD_Z, False), (128, OFF_SSD_DT, False)]
        ssd_zero = [jnp.zeros((bsz, SSD_HEADS // 2, 128, 128), F32)]
        ya_t, ya_l = _bidir_mixer(_ssd_kernel, ssd_pieces, SSD_CHUNK, ut, ul, False, ssd_zero, ssd_params, "ssd")

        ml_gb = _row128(jnp.concatenate([ml_igate_b[i].reshape(-1), ml_fgate_b[i].reshape(-1)]))
        ml_params = [ml_conv_w[i], row(ml_conv_b[i]), ml_gb, row(ml_norm_g[i])]
        ml_pieces = [(512, OFF_ML_Q, True), (512, OFF_ML_K, True), (512, OFF_ML_V, False),
                     (512, OFF_ML_O, False), (128, OFF_ML_G, False)]
        ml_zero = [jnp.zeros((bsz, ML_HEADS, 128, 128), F32), jnp.zeros((bsz, 2 * ML_HEADS, 128), F32)]
        yb_t, yb_l = _bidir_mixer(_mlstm_kernel, ml_pieces, ML_CHUNK, ut, ul, True, ml_zero, ml_params, "mlstm")

        lru_w = jnp.stack([jnp.concatenate([_block_diag(lru_wa[i, dd]), _block_diag(lru_wx[i, dd])], axis=1)
                           for dd in range(2)]).astype(BF16)
        lru_bias = jnp.concatenate([lru_ba[i], lru_bx[i]], axis=1).reshape(2, 1, 2 * GROUP_W)
        lru_params = [lru_conv_w[i], row(lru_conv_b[i]), lru_w, lru_bias, lru_lambda[i].reshape(2, 1, GROUP_W)]
        lru_pieces = [(512, OFF_LRU_X, True), (512, OFF_LRU_GATE, False)]
        lru_zero = [jnp.zeros((bsz, 1, GROUP_W), F32)]
        yc_t, yc_l = _bidir_mixer(_lru_kernel, lru_pieces, LRU_CHUNK, ut, ul, False, lru_zero, lru_params, "lru")

        wg = jnp.zeros((2, 128, GLA_HEADS * GLA_DK), F32)
        wg = wg.at[0, :GLA_RANK].set(gla_wg2[i, 0]).at[1, GLA_RANK:2 * GLA_RANK].set(gla_wg2[i, 1])
        gla_params = [wg.astype(BF16), gla_bg[i].reshape(2, 1, GLA_HEADS * GLA_DK), row(gla_norm_g[i])]
        gla_pieces = [(512, OFF_GLA_QK, False), (512, OFF_GLA_V, False), (512, OFF_GLA_R, False),
                      (128, OFF_GLA_G1, False)]
        gla_zero = [jnp.zeros((bsz, GROUP_W, GLA_HEADS * GLA_DK), F32)]
        yd_t, yd_l = _bidir_mixer(_gla_kernel, gla_pieces, GLA_CHUNK, ut, ul, True, gla_zero, gla_params, "gla")

        w_out_b = w_out[i].astype(BF16).reshape(4, GROUP_W, dm)
        wg_b, wu_b, wd_b = w_gate[i].astype(BF16), w_up[i].astype(BF16), w_down[i].astype(BF16)
        fg = row(final_g)

        def tail(xs, ys, m, rows_per_group, tm, final):
            ys = [y.reshape(-1, GROUP_W) for y in ys]
            xs, h2 = _outproj(ys, w_out_b, xs, m[2], row(norm2_g[i]), m[3], m[4], rows_per_group, tm)
            act = _ffn_up(h2, wg_b, wu_b, tm * 2)
            return _ffn_down(act, wd_b, xs, m[5], fg, final, rows_per_group, tm * 2)

        xl = tail(xl, [ya_l, yb_l, yc_l, yd_l], m_l, length, 256, i == depth - 1)
        if need_ctx:
            xt = tail(xt, [ya_t, yb_t, yc_t, yd_t], m_t, bsz * lctx, 256, False)
    return xl.reshape(bsz, length, dm)
```

```python
import functools

import jax
import jax.numpy as jnp
from jax import lax
from jax.experimental import pallas as pl
from jax.experimental.pallas import tpu as pltpu

F32 = jnp.float32
BF16 = jnp.bfloat16
EPS = 1e-6
NEG = -1e30

D_MODEL = 2048
GRID_W = 64
GROUP_W = 512
N_MOD = 6
CONV_W = 5
HALO = 8

SSD_HEADS = 8
SSD_CHUNK = 128
ML_HEADS = 4
ML_CHUNK = 64
LRU_C = 8.0
LRU_CHUNK = 128
GLA_HEADS = 4
GLA_DK = 64
GLA_RANK = 16
GLA_TAU = 16.0
GLA_CHUNK = 64
GLA_SUB = 16

OFF_SSD_XBC = 0
OFF_LRU_X = 1024
OFF_LRU_GATE = 1536
OFF_SSD_Z = 2048
OFF_SSD_DT = 2560
N_ROW = 2688
OFF_ML_Q = 0
OFF_ML_K = 512
OFF_ML_V = 1024
OFF_ML_O = 1536
OFF_GLA_QK = 2048
OFF_GLA_V = 2560
OFF_GLA_R = 3072
OFF_ML_G = 3584
OFF_GLA_G1 = 3712
N_COL = 3840

VMEM_LIMIT = 56 * 1024 * 1024
TM_PROJ = 256
TM_OUT = 512
TM_FFN_UP = 2048
TN_FFN_UP = 512
TM_FFN_DOWN = 256


def _cparams(sem):
    return pltpu.CompilerParams(dimension_semantics=sem, vmem_limit_bytes=VMEM_LIMIT)


def _resident(shape):
    nd = len(shape)
    return pl.BlockSpec(shape, lambda *_: (0,) * nd, pipeline_mode=pl.Buffered(1))


def _sigmoid(x):
    return 1.0 / (1.0 + jnp.exp(-x))


def _silu(x):
    return x * _sigmoid(x)


def _softplus(x):
    return jnp.maximum(x, 0.0) + jnp.log1p(jnp.exp(-jnp.abs(x)))


def _log_sigmoid(x):
    return jnp.minimum(x, 0.0) - jnp.log1p(jnp.exp(-jnp.abs(x)))


def _gelu_tanh(x):
    return 0.5 * x * (1.0 + jnp.tanh(0.7978845608028654 * (x + 0.044715 * (x * x * x))))


def _rms(x):
    return x * lax.rsqrt(jnp.mean(x * x, axis=-1, keepdims=True) + EPS)


def _dot(a, b):
    return jnp.dot(a, b, preferred_element_type=F32)


def _dot_nt(a, b):
    return lax.dot_general(a, b, (((1,), (1,)), ((), ())), preferred_element_type=F32)


def _dot_exact(a, b):
    return jnp.dot(a, b, precision=lax.Precision.HIGHEST, preferred_element_type=F32)


def _scan_mask(t, rev):
    ri = lax.broadcasted_iota(jnp.int32, (t, t), 0)
    ci = lax.broadcasted_iota(jnp.int32, (t, t), 1)
    return (ci >= ri) if rev else (ci <= ri)


def _conv5(prev, main, nxt, w, bias, first, last):
    t = main.shape[0]
    prev = jnp.where(first, 0.0, prev)
    nxt = jnp.where(last, 0.0, nxt)
    ext = jnp.concatenate([prev, main, nxt], axis=0)
    n = t + 2 * HALO
    acc = None
    for k in range(CONV_W):
        sh = (CONV_W // 2 - k) % n
        r = ext if sh == 0 else pltpu.roll(ext, sh, axis=0)
        term = r[HALO:HALO + t] * w[k:k + 1, :]
        acc = term if acc is None else acc + term
    return acc + bias


def _mod_kernel(c_ref, w_ref, b_ref, o_ref):
    s = _silu(c_ref[...]).astype(BF16)
    o_ref[...] = _dot(s, w_ref[...].astype(BF16)) + b_ref[...]


def _inproj_kernel(x_ref, g_ref, sh_ref, sc_ref, w_ref, o_ref):
    y = _rms(x_ref[...]) * g_ref[...]
    h = (y * (1.0 + sc_ref[0]) + sh_ref[0]).astype(BF16)
    o_ref[...] = _dot(h, w_ref[...])


def _outproj_kernel(ya_ref, yb_ref, yc_ref, yd_ref, w_ref, x_ref, gate_ref, g_ref, sh_ref, sc_ref,
                    xo_ref, h_ref):
    acc = _dot(ya_ref[...], w_ref[0])
    acc = acc + _dot(yb_ref[...], w_ref[1])
    acc = acc + _dot(yc_ref[...], w_ref[2])
    acc = acc + _dot(yd_ref[...], w_ref[3])
    xn = x_ref[...] + gate_ref[0] * acc
    xo_ref[...] = xn
    y = _rms(xn) * g_ref[...]
    h_ref[...] = (y * (1.0 + sc_ref[0]) + sh_ref[0]).astype(BF16)


def _ffn_up_kernel(h_ref, wg_ref, wu_ref, o_ref):
    h = h_ref[...]
    o_ref[...] = (_silu(_dot(h, wg_ref[...])) * _dot(h, wu_ref[...])).astype(BF16)


def _ffn_down_kernel(final, a_ref, w_ref, x_ref, gate_ref, fg_ref, o_ref):
    xn = x_ref[...] + gate_ref[0] * _dot(a_ref[...], w_ref[...])
    if final:
        xn = _rms(xn) * fg_ref[...]
    o_ref[...] = xn


def _ssd_kernel(rev, finish, nchunks, nb, *refs):
    if finish:
        (xbc_ref, xp_ref, xn_ref, dt_ref, z_ref, yp_ref, s0_ref, cw_ref, cb_ref, dtb_ref, alog_ref,
         dsk_ref, ng_ref, y_ref, s_ref) = refs
    else:
        (xbc_ref, xp_ref, xn_ref, dt_ref, s0_ref, cw_ref, cb_ref, dtb_ref, alog_ref,
         dsk_ref, ng_ref, y_ref, s_ref) = refs
    t = SSD_CHUNK
    d = 1 if rev else 0
    j = pl.program_id(0)
    c = (nchunks - 1 - j) if rev else j

    @pl.when(j == 0)
    def _():
        s_ref[...] = s0_ref[...]

    mask = _scan_mask(t, rev)
    maskf = mask.astype(F32)
    er = 0 if rev else t - 1
    lo = lax.broadcasted_iota(jnp.int32, (t, 128), 1) < 64
    lo_rows = lax.broadcasted_iota(jnp.int32, (128, 128), 0) < 64
    cw, cbias = cw_ref[...], cb_ref[...]
    nega = -jnp.exp(alog_ref[...])

    bs = range(nb)
    xbc = [_silu(_conv5(xp_ref[bi], xbc_ref[bi], xn_ref[bi], cw, cbias, c == 0, c == nchunks - 1)) for bi in bs]
    dt_all = [_softplus(dt_ref[bi] + dtb_ref[...]) for bi in bs]
    b_all = [_dot_exact(maskf, dt_all[bi] * nega) for bi in bs]
    b_t = [b_all[bi].T for bi in bs]
    dt_t = [dt_all[bi].T for bi in bs]
    ys = [[None] * (SSD_HEADS // 2) for _ in bs]
    for p in range(SSD_HEADS // 2):
        g = p // 2
        bm = [xbc[bi][:, 512 + 128 * g:640 + 128 * g].astype(BF16) for bi in bs]
        cm = [xbc[bi][:, 768 + 128 * g:896 + 128 * g].astype(BF16) for bi in bs]
        cb = [_dot_nt(cm[bi], bm[bi]) for bi in bs]
        xpair = [xbc[bi][:, 128 * p:128 * p + 128] for bi in bs]
        xpair_b = [xpair[bi].astype(BF16) for bi in bs]
        s_pair = [s_ref[bi, p] for bi in bs]
        y_inter = [_dot_nt(cm[bi], s_pair[bi].astype(BF16)) for bi in bs]
        yh, eb, coef, dec = [], [], [], []
        for hh in range(2):
            l = d * SSD_HEADS + 2 * p + hh
            bcol = [b_all[bi][:, l:l + 1] for bi in bs]
            seg = [jnp.exp(jnp.where(mask, bcol[bi] - b_t[bi][l:l + 1, :], NEG)) for bi in bs]
            w = [(cb[bi] * seg[bi] * dt_t[bi][l:l + 1, :]).astype(BF16) for bi in bs]
            yh.append([_dot(w[bi], xpair_b[bi]) for bi in bs])
            eb.append([jnp.exp(bcol[bi]) for bi in bs])
            bl = [b_all[bi][er:er + 1, l:l + 1] for bi in bs]
            coef.append([jnp.exp(bl[bi] - bcol[bi]) * dt_all[bi][:, l:l + 1] for bi in bs])
            dec.append([jnp.exp(bl[bi]) for bi in bs])
        xw = [(xpair[bi] * jnp.where(lo, coef[0][bi], coef[1][bi])).T.astype(BF16) for bi in bs]
        upd = [_dot(xw[bi], bm[bi]) for bi in bs]
        for bi in bs:
            ys[bi][p] = (jnp.where(lo, yh[0][bi], yh[1][bi])
                         + y_inter[bi] * jnp.where(lo, eb[0][bi], eb[1][bi]))
            s_ref[bi, p] = jnp.where(lo_rows, dec[0][bi], dec[1][bi]) * s_pair[bi] + upd[bi]
    for bi in bs:
        y = jnp.concatenate(ys[bi], axis=1)
        if finish:
            y = y + yp_ref[bi] + dsk_ref[...] * xbc[bi][:, :GROUP_W]
            y = _rms(y * _silu(z_ref[bi])) * ng_ref[...]
            y_ref[bi] = y.astype(y_ref.dtype)
        else:
            y_ref[bi] = y


def _mlstm_kernel(rev, finish, nchunks, nb, *refs):
    if finish:
        (q_ref, qp_ref, qn_ref, k_ref, kp_ref, kn_ref, v_ref, g_ref, o_ref, yp_ref, cs0_ref, sm0_ref,
         cw_ref, cb_ref, gb_ref, ng_ref, y_ref, cs_ref, sm_ref) = refs
    else:
        (q_ref, qp_ref, qn_ref, k_ref, kp_ref, kn_ref, v_ref, g_ref, cs0_ref, sm0_ref,
         cw_ref, cb_ref, gb_ref, ng_ref, y_ref, cs_ref, sm_ref) = refs
    t = ML_CHUNK
    d = 1 if rev else 0
    j = pl.program_id(0)
    c = (nchunks - 1 - j) if rev else j
    first, last = c == 0, c == nchunks - 1

    @pl.when(j == 0)
    def _():
        cs_ref[...] = cs0_ref[...]
        sm_ref[...] = sm0_ref[...]

    cw = cw_ref[...]
    cbias = cb_ref[...]
    mask = _scan_mask(t, rev)
    maskf = mask.astype(F32)
    er = 0 if rev else t - 1

    bs = range(nb)
    q = [_silu(_conv5(qp_ref[bi], q_ref[bi], qn_ref[bi], cw[:, :GROUP_W], cbias[:, :GROUP_W], first, last))
         for bi in bs]
    k = [_silu(_conv5(kp_ref[bi], k_ref[bi], kn_ref[bi], cw[:, GROUP_W:], cbias[:, GROUP_W:], first, last))
         * (128.0 ** -0.5) for bi in bs]
    v = [v_ref[bi] for bi in bs]
    gts = [g_ref[bi] + gb_ref[...] for bi in bs]
    b_all = [_dot_exact(maskf, _log_sigmoid(gts[bi])) for bi in bs]
    g_t = [gts[bi].T for bi in bs]
    b_t = [b_all[bi].T for bi in bs]
    ys = [[None] * ML_HEADS for _ in bs]
    for h in range(ML_HEADS):
        li = d * ML_HEADS + h
        lf = 2 * ML_HEADS + li
        sl = slice(128 * h, 128 * h + 128)
        bcol = [b_all[bi][:, lf:lf + 1] for bi in bs]
        m_prev = [sm_ref[bi, ML_HEADS + h:ML_HEADS + h + 1, 0:1] for bi in bs]
        ns = [sm_ref[bi, h:h + 1, :] for bi in bs]
        cs = [cs_ref[bi, h] for bi in bs]
        dmat = [jnp.where(mask, bcol[bi] - b_t[bi][lf:lf + 1, :] + g_t[bi][li:li + 1, :], NEG) for bi in bs]
        inter = [bcol[bi] + m_prev[bi] for bi in bs]
        mt = [jnp.maximum(inter[bi], jnp.max(dmat[bi], axis=1, keepdims=True)) for bi in bs]
        w = [jnp.exp(dmat[bi] - mt[bi]) for bi in bs]
        sc = [jnp.exp(inter[bi] - mt[bi]) for bi in bs]
        qb = [q[bi][:, sl].astype(BF16) for bi in bs]
        kb = [k[bi][:, sl].astype(BF16) for bi in bs]
        sw = [_dot_nt(qb[bi], kb[bi]) * w[bi] for bi in bs]
        num = [_dot(sw[bi].astype(BF16), v[bi][:, sl].astype(BF16)) for bi in bs]
        qc = [_dot_nt(qb[bi], cs[bi].astype(BF16)) for bi in bs]
        den = [jnp.sum(sw[bi], axis=1, keepdims=True)
               + sc[bi] * jnp.sum(q[bi][:, sl] * ns[bi], axis=1, keepdims=True) for bi in bs]
        for bi in bs:
            ys[bi][h] = (num[bi] + sc[bi] * qc[bi]) / jnp.maximum(jnp.abs(den[bi]), jnp.exp(-mt[bi]))
        bl = [b_all[bi][er:er + 1, lf:lf + 1] for bi in bs]
        tail = [bl[bi] - bcol[bi] + gts[bi][:, li:li + 1] for bi in bs]
        m_new = [jnp.maximum(bl[bi] + m_prev[bi], jnp.max(tail[bi], axis=0, keepdims=True)) for bi in bs]
        ws = [jnp.exp(tail[bi] - m_new[bi]) for bi in bs]
        sc_end = [jnp.exp(bl[bi] + m_prev[bi] - m_new[bi]) for bi in bs]
        upd = [_dot((v[bi][:, sl] * ws[bi]).T.astype(BF16), kb[bi]) for bi in bs]
        for bi in bs:
            cs_ref[bi, h] = sc_end[bi] * cs[bi] + upd[bi]
            sm_ref[bi, h:h + 1, :] = sc_end[bi] * ns[bi] + jnp.sum(ws[bi] * k[bi][:, sl], axis=0, keepdims=True)
            sm_ref[bi, ML_HEADS + h:ML_HEADS + h + 1, :] = jnp.broadcast_to(m_new[bi], (1, 128))
    for bi in bs:
        if finish:
            yp = yp_ref[bi]
            outs = [_rms(ys[bi][h] + yp[:, 128 * h:128 * h + 128]) for h in range(ML_HEADS)]
            y = jnp.concatenate(outs, axis=1) * ng_ref[...]
            y_ref[bi] = (_sigmoid(o_ref[bi]) * y).astype(y_ref.dtype)
        else:
            y_ref[bi] = jnp.concatenate(ys[bi], axis=1)


def _lru_kernel(rev, finish, nchunks, nb, *refs):
    if finish:
        (x_ref, xp_ref, xn_ref, gate_ref, yp_ref, h0_ref, cw_ref, cb_ref, w_ref, bias_ref, lam_ref,
         y_ref, h_ref) = refs
    else:
        (x_ref, xp_ref, xn_ref, h0_ref, cw_ref, cb_ref, w_ref, bias_ref, lam_ref,
         y_ref, h_ref) = refs
    t = LRU_CHUNK
    d = 1 if rev else 0
    j = pl.program_id(0)
    c = (nchunks - 1 - j) if rev else j

    @pl.when(j == 0)
    def _():
        h_ref[...] = h0_ref[...]

    cw, cbias = cw_ref[...], cb_ref[...]
    lsl = LRU_C * _log_sigmoid(lam_ref[d])
    sub = lax.broadcasted_iota(jnp.int32, (8, GROUP_W), 0)
    n = t // 8

    for bi in range(nb):
        xf = _conv5(xp_ref[bi], x_ref[bi], xn_ref[bi], cw, cbias, c == 0, c == nchunks - 1)
        pre = _dot(xf.astype(BF16), w_ref[d]) + bias_ref[d]
        r = _sigmoid(pre[:, :GROUP_W])
        ig = _sigmoid(pre[:, GROUP_W:])
        loga = r * lsl
        a = jnp.exp(loga)
        bx = jnp.sqrt(-jnp.tanh(loga) * (1.0 + a * a)) * ig * xf

        a_loc, h_loc = [], []
        for i in range(n):
            ai = a[8 * i:8 * i + 8]
            hi = bx[8 * i:8 * i + 8]
            for sh in (1, 2, 4):
                if rev:
                    valid = sub < 8 - sh
                    a_s = pltpu.roll(ai, 8 - sh, axis=0)
                    h_s = pltpu.roll(hi, 8 - sh, axis=0)
                else:
                    valid = sub >= sh
                    a_s = pltpu.roll(ai, sh, axis=0)
                    h_s = pltpu.roll(hi, sh, axis=0)
                hi = jnp.where(valid, ai * h_s + hi, hi)
                ai = jnp.where(valid, ai * a_s, ai)
            a_loc.append(ai)
            h_loc.append(hi)
        carry = h_ref[bi]
        outs = [None] * n
        for i in (range(n - 1, -1, -1) if rev else range(n)):
            hi = h_loc[i] + a_loc[i] * carry
            outs[i] = hi
            carry = hi[0:1] if rev else hi[7:8]
        h_ref[bi] = carry
        hs = jnp.concatenate(outs, axis=0)
        if finish:
            y_ref[bi] = ((hs + yp_ref[bi]) * _gelu_tanh(gate_ref[bi])).astype(y_ref.dtype)
        else:
            y_ref[bi] = hs


def _gla_kernel(rev, finish, nchunks, nb, *refs):
    if finish:
        (qk_ref, v_ref, g1_ref, r_ref, yp_ref, s0_ref, wg_ref, bg_ref, ng_ref, y_ref, s_ref) = refs
    else:
        (qk_ref, v_ref, g1_ref, s0_ref, wg_ref, bg_ref, ng_ref, y_ref, s_ref) = refs
    t = GLA_CHUNK
    d = 1 if rev else 0
    dkk = GLA_HEADS * GLA_DK
    j = pl.program_id(0)

    @pl.when(j == 0)
    def _():
        s_ref[...] = s0_ref[...]

    mask = _scan_mask(t, rev)
    maskf = mask.astype(F32)
    rows = lax.broadcasted_iota(jnp.int32, (t, 1), 0)
    lane_head = lax.broadcasted_iota(jnp.int32, (GLA_SUB, dkk), 1) // GLA_DK
    blk = (lax.broadcasted_iota(jnp.int32, (GROUP_W, dkk), 0) // 128
           == lax.broadcasted_iota(jnp.int32, (GROUP_W, dkk), 1) // GLA_DK)
    er = 0 if rev else t - 1

    bs = range(nb)
    q = [qk_ref[bi][:, :dkk] * (GLA_DK ** -0.5) for bi in bs]
    k = [qk_ref[bi][:, dkk:] for bi in bs]
    v = [v_ref[bi] for bi in bs]
    vb = [v[bi].astype(BF16) for bi in bs]
    glog = [_dot(g1_ref[bi].astype(BF16), wg_ref[d]) + bg_ref[d] for bi in bs]
    la = [_log_sigmoid(glog[bi]) * (1.0 / GLA_TAU) for bi in bs]
    b = [_dot_exact(maskf, la[bi]) for bi in bs]
    excl = [b[bi] - la[bi] for bi in bs]
    s_t = [s_ref[bi] for bi in bs]
    o_inter = [_dot_nt((q[bi] * jnp.exp(b[bi])).astype(BF16), s_t[bi].astype(BF16)) for bi in bs]

    att_blocks = [[None] * (t // GLA_SUB) for _ in bs]
    for i in range(t // GLA_SUB):
        r0 = GLA_SUB * i
        if rev:
            ref_row = [excl[bi][r0 + GLA_SUB - 1:r0 + GLA_SUB] for bi in bs]
            kvalid = rows >= r0
        else:
            ref_row = [excl[bi][r0:r0 + 1] for bi in bs]
            kvalid = rows < r0 + GLA_SUB
        qi = [q[bi][r0:r0 + GLA_SUB] * jnp.exp(b[bi][r0:r0 + GLA_SUB] - ref_row[bi]) for bi in bs]
        ki = [(k[bi] * jnp.exp(jnp.where(kvalid, ref_row[bi] - b[bi], NEG))).astype(BF16) for bi in bs]
        qs = [jnp.concatenate([jnp.where(lane_head == h, qi[bi], 0.0) for h in range(GLA_HEADS)],
                              axis=0).astype(BF16) for bi in bs]
        for bi in bs:
            att_blocks[bi][i] = _dot_nt(qs[bi], ki[bi])
    outs = [[None] * GLA_HEADS for _ in bs]
    for h in range(GLA_HEADS):
        att = [jnp.where(mask, jnp.concatenate([ab[GLA_SUB * h:GLA_SUB * h + GLA_SUB] for ab in att_blocks[bi]],
                                               axis=0), 0.0).astype(BF16) for bi in bs]
        for bi in bs:
            outs[bi][h] = _dot(att[bi], vb[bi][:, 128 * h:128 * h + 128])

    bl = [b[bi][er:er + 1] for bi in bs]
    kd = [(k[bi] * jnp.exp(bl[bi] - b[bi])).astype(BF16) for bi in bs]
    v_t = [jnp.concatenate([v[bi][:, 128 * h:128 * h + 128].T for h in range(GLA_HEADS)], axis=0).astype(BF16)
           for bi in bs]
    upd = [_dot(v_t[bi], kd[bi]) for bi in bs]
    for bi in bs:
        s_ref[bi] = s_t[bi] * jnp.exp(bl[bi]) + jnp.where(blk, upd[bi], 0.0)
    for bi in bs:
        o = jnp.concatenate(outs[bi], axis=1) + o_inter[bi]
        if finish:
            o = o + yp_ref[bi]
            o = jnp.concatenate([_rms(o[:, 128 * h:128 * h + 128]) for h in range(GLA_HEADS)], axis=1)
            y_ref[bi] = (o * ng_ref[...] * _silu(r_ref[bi])).astype(y_ref.dtype)
        else:
            y_ref[bi] = o


def _chunk_specs(nb, width, off, t, length, rev, nchunks, halo):
    ob = off // width
    tb = t // HALO

    def cj(j):
        return (nchunks - 1 - j) if rev else j

    main = pl.BlockSpec((nb, t, width), lambda j: (0, cj(j), ob))
    if not halo:
        return [main]
    prev = pl.BlockSpec((nb, HALO, width), lambda j: (0, jnp.maximum(cj(j) * tb - 1, 0), ob))
    nxt = pl.BlockSpec((nb, HALO, width), lambda j: (0, jnp.minimum((cj(j) + 1) * tb, length // HALO - 1), ob))
    return [main, prev, nxt]


def _run_mixer(body, pieces, fin_pieces, t, u, rev, finish, ypart, states, params, name):
    nb, length = u.shape[0], u.shape[1]
    nchunks = length // t
    cj = (lambda j: nchunks - 1 - j) if rev else (lambda j: j)
    y_spec = pl.BlockSpec((nb, t, GROUP_W), lambda j: (0, cj(j), 0))
    in_specs, args = [], []
    for width, off, halo in pieces + (fin_pieces if finish else []):
        sp = _chunk_specs(nb, width, off, t, length, rev, nchunks, halo)
        in_specs += sp
        args += [u] * len(sp)
    if finish:
        in_specs.append(y_spec)
        args.append(ypart)
    for a in list(states) + list(params):
        in_specs.append(pl.BlockSpec(a.shape, lambda j, _nd=a.ndim: (0,) * _nd))
        args.append(a)
    out_shape = [jax.ShapeDtypeStruct((nb, length, GROUP_W), BF16 if finish else F32)]
    out_specs = [y_spec]
    for s in states:
        out_shape.append(jax.ShapeDtypeStruct(s.shape, s.dtype))
        out_specs.append(pl.BlockSpec(s.shape, lambda j, _nd=s.ndim: (0,) * _nd))
    res = pl.pallas_call(
        functools.partial(body, rev, finish, nchunks, nb),
        grid=(nchunks,),
        in_specs=in_specs,
        out_specs=out_specs,
        out_shape=out_shape,
        compiler_params=_cparams(("arbitrary",)),
        name=name,
    )(*args)
    return res[0], list(res[1:])


def _bidir_mixer(body, pieces, fin_pieces, t, u_ctx, u_lat, zero_states, params, name):
    run = functools.partial(_run_mixer, body, pieces, fin_pieces, t)
    yb_c, st_b = run(u_ctx, True, False, None, zero_states, params, name + "_ctx_bwd")
    y_c, st_f = run(u_ctx, False, True, yb_c, zero_states, params, name + "_ctx_fwd")
    yb_l, _ = run(u_lat, True, False, None, st_b, params, name + "_lat_bwd")
    y_l, _ = run(u_lat, False, True, yb_l, st_f, params, name + "_lat_fwd")
    return y_c, y_l


def _mods(cvec, w_mod, b_mod):
    depth, dm, nm = w_mod.shape
    tn = 1024
    return pl.pallas_call(
        _mod_kernel,
        grid=(depth, nm // tn),
        in_specs=[pl.BlockSpec((8, dm), lambda l, j: (0, 0)),
                  pl.BlockSpec((None, dm, tn), lambda l, j: (l, 0, j)),
                  pl.BlockSpec((None, 1, tn), lambda l, j: (l, 0, j))],
        out_specs=pl.BlockSpec((None, 8, tn), lambda l, j: (l, 0, j)),
        out_shape=jax.ShapeDtypeStruct((depth, 8, nm), F32),
        compiler_params=_cparams(("parallel", "parallel")),
        name="mods",
    )(cvec, w_mod, b_mod.reshape(depth, 1, nm))


def _inproj(x, g, shift, scale, w, rows_per_group):
    m, dm = x.shape
    n = w.shape[1]
    tm = TM_PROJ
    grp = lambda i: (i * tm // rows_per_group, 0, 0)
    return pl.pallas_call(
        _inproj_kernel,
        grid=(m // tm,),
        in_specs=[pl.BlockSpec((tm, dm), lambda i: (i, 0)),
                  _resident((1, dm)),
                  pl.BlockSpec((1, 1, dm), grp),
                  pl.BlockSpec((1, 1, dm), grp),
                  _resident((dm, n))],
        out_specs=pl.BlockSpec((tm, n), lambda i: (i, 0)),
        out_shape=jax.ShapeDtypeStruct((m, n), F32),
        compiler_params=_cparams(("parallel",)),
        name="inproj",
    )(x, g, shift, scale, w)


def _outproj(ys, w, x, gate, g, shift, scale, rows_per_group):
    m, dm = x.shape
    tm = min(TM_OUT, rows_per_group)
    grp = lambda i: (i * tm // rows_per_group, 0, 0)
    yspec = pl.BlockSpec((tm, GROUP_W), lambda i: (i, 0))
    return pl.pallas_call(
        _outproj_kernel,
        grid=(m // tm,),
        in_specs=[yspec, yspec, yspec, yspec,
                  _resident((4, GROUP_W, dm)),
                  pl.BlockSpec((tm, dm), lambda i: (i, 0)),
                  pl.BlockSpec((1, 1, dm), grp),
                  _resident((1, dm)),
                  pl.BlockSpec((1, 1, dm), grp),
                  pl.BlockSpec((1, 1, dm), grp)],
        out_specs=[pl.BlockSpec((tm, dm), lambda i: (i, 0)), pl.BlockSpec((tm, dm), lambda i: (i, 0))],
        out_shape=[jax.ShapeDtypeStruct((m, dm), F32), jax.ShapeDtypeStruct((m, dm), BF16)],
        compiler_params=_cparams(("parallel",)),
        name="outproj",
    )(*ys, w, x, gate, g, shift, scale)


def _ffn_up(h, wg, wu):
    m, dm = h.shape
    dff = wg.shape[1]
    tm = min(TM_FFN_UP, m)
    tn = TN_FFN_UP
    return pl.pallas_call(
        _ffn_up_kernel,
        grid=(m // tm, dff // tn),
        in_specs=[pl.BlockSpec((tm, dm), lambda i, j: (i, 0)),
                  pl.BlockSpec((dm, tn), lambda i, j: (0, j)),
                  pl.BlockSpec((dm, tn), lambda i, j: (0, j))],
        out_specs=pl.BlockSpec((tm, tn), lambda i, j: (i, j)),
        out_shape=jax.ShapeDtypeStruct((m, dff), BF16),
        compiler_params=_cparams(("parallel", "arbitrary")),
        name="ffn_up",
    )(h, wg, wu)


def _ffn_down(a, w, x, gate, final_g, final, rows_per_group):
    m, dm = x.shape
    dff = a.shape[1]
    tm = TM_FFN_DOWN
    grp = lambda i: (i * tm // rows_per_group, 0, 0)
    return pl.pallas_call(
        functools.partial(_ffn_down_kernel, final),
        grid=(m // tm,),
        in_specs=[pl.BlockSpec((tm, dff), lambda i: (i, 0)),
                  _resident((dff, dm)),
                  pl.BlockSpec((tm, dm), lambda i: (i, 0)),
                  pl.BlockSpec((1, 1, dm), grp),
                  _resident((1, dm))],
        out_specs=pl.BlockSpec((tm, dm), lambda i: (i, 0)),
        out_shape=jax.ShapeDtypeStruct((m, dm), F32),
        compiler_params=_cparams(("parallel",)),
        name="ffn_down",
    )(a, w, x, gate, final_g)


def _pad_cols(a, width):
    return jnp.pad(a, ((0, 0), (0, width - a.shape[1])))


def _split_w_in(w):
    ssd, ml, lru, gla = jnp.split(w, [1552, 1552 + 2064, 1552 + 2064 + 1024], axis=1)
    ssd_z, ssd_xbc, ssd_dt = ssd[:, :512], ssd[:, 512:1536], ssd[:, 1536:]
    ml_qkvo, ml_g = ml[:, :2048], ml[:, 2048:]
    lru_gate, lru_x = lru[:, :512], lru[:, 512:]
    gla_qkvr, gla_g1 = gla[:, :1536], gla[:, 1536:]
    w_row = jnp.concatenate([ssd_xbc, lru_x, lru_gate, ssd_z, _pad_cols(ssd_dt, 128)], axis=1)
    w_col = jnp.concatenate([ml_qkvo, gla_qkvr, _pad_cols(ml_g, 128), _pad_cols(gla_g1, 128)], axis=1)
    return w_row.astype(BF16), w_col.astype(BF16)


def _row128(a):
    return _pad_cols(a.reshape(1, -1).astype(F32), 128)


def _block_diag(w):
    nb, bi, bj = w.shape
    eye = jnp.eye(nb, dtype=w.dtype)
    return (eye[:, None, :, None] * w[:, :, None, :]).reshape(nb * bi, nb * bj)


def _to_colmajor(a, bsz, rows):
    n = a.shape[-1]
    return a.reshape(bsz, rows, GRID_W, n).transpose(0, 2, 1, 3).reshape(bsz, rows * GRID_W, n)


def _from_colmajor(a, bsz, rows):
    n = a.shape[-1]
    return a.reshape(bsz, GRID_W, rows, n).transpose(0, 2, 1, 3).reshape(bsz, rows * GRID_W, n)


def kernel(x, c, ctx, c_ctx, norm1_g, norm2_g, w_mod, b_mod, w_in, w_out, ssd_conv_w, ssd_conv_b, ssd_dt_bias, ssd_a_log, ssd_d, ssd_norm_g, ml_conv_w, ml_conv_b, ml_igate_b, ml_fgate_b, ml_norm_g, lru_conv_w, lru_conv_b, lru_wa, lru_ba, lru_wx, lru_bx, lru_lambda, gla_wg2, gla_bg, gla_norm_g, w_gate, w_up, w_down, final_g):
    bsz, length, dm = x.shape
    lctx = ctx.shape[1]
    depth = w_in.shape[0]
    rows = length // GRID_W
    assert dm == D_MODEL and length % SSD_CHUNK == 0 and lctx % SSD_CHUNK == 0

    cvec = jnp.concatenate([c, c_ctx[None, :], jnp.zeros((8 - bsz - 1, dm), F32)], axis=0)
    mods = _mods(cvec, w_mod, b_mod)

    xl = x.reshape(bsz * length, dm)
    xt = ctx.reshape(bsz * lctx, dm)
    row = lambda a: a.reshape(1, -1).astype(F32)

    for i in range(depth):
        need_ctx = i < depth - 1
        m_l = [mods[i, :bsz, k * dm:(k + 1) * dm].reshape(bsz, 1, dm) for k in range(N_MOD)]
        m_t = [mods[i, bsz:bsz + 1, k * dm:(k + 1) * dm].reshape(1, 1, dm) for k in range(N_MOD)]
        w_row, w_col = _split_w_in(w_in[i])
        g1 = row(norm1_g[i])
        xl_cm = _to_colmajor(xl.reshape(bsz, length, dm), bsz, rows).reshape(bsz * length, dm)
        ul_row = _inproj(xl, g1, m_l[0], m_l[1], w_row, length).reshape(bsz, length, N_ROW)
        ul_col = _inproj(xl_cm, g1, m_l[0], m_l[1], w_col, length).reshape(bsz, length, N_COL)
        ut_row = _inproj(xt, g1, m_t[0], m_t[1], w_row, bsz * lctx).reshape(bsz, lctx, N_ROW)
        ut_col = _inproj(xt, g1, m_t[0], m_t[1], w_col, bsz * lctx).reshape(bsz, lctx, N_COL)

        ssd_params = [ssd_conv_w[i], row(ssd_conv_b[i]), _row128(ssd_dt_bias[i]), _row128(ssd_a_log[i]),
                      row(jnp.repeat(ssd_d[i], GROUP_W // SSD_HEADS)), row(ssd_norm_g[i])]
        ssd_pieces = [(1024, OFF_SSD_XBC, True), (128, OFF_SSD_DT, False)]
        ssd_fin = [(512, OFF_SSD_Z, False)]
        ssd_zero = [jnp.zeros((bsz, SSD_HEADS // 2, 128, 128), F32)]
        ya_t, ya_l = _bidir_mixer(_ssd_kernel, ssd_pieces, ssd_fin, SSD_CHUNK, ut_row, ul_row, ssd_zero,
                                  ssd_params, "ssd")

        ml_gb = _row128(jnp.concatenate([ml_igate_b[i].reshape(-1), ml_fgate_b[i].reshape(-1)]))
        ml_params = [ml_conv_w[i], row(ml_conv_b[i]), ml_gb, row(ml_norm_g[i])]
        ml_pieces = [(512, OFF_ML_Q, True), (512, OFF_ML_K, True), (512, OFF_ML_V, False), (128, OFF_ML_G, False)]
        ml_fin = [(512, OFF_ML_O, False)]
        ml_zero = [jnp.zeros((bsz, ML_HEADS, 128, 128), F32), jnp.zeros((bsz, 2 * ML_HEADS, 128), F32)]
        yb_t, yb_l = _bidir_mixer(_mlstm_kernel, ml_pieces, ml_fin, ML_CHUNK, ut_col, ul_col, ml_zero,
                                  ml_params, "mlstm")

        lru_w = jnp.stack([jnp.concatenate([_block_diag(lru_wa[i, dd]), _block_diag(lru_wx[i, dd])], axis=1)
                           for dd in range(2)]).astype(BF16)
        lru_bias = jnp.concatenate([lru_ba[i], lru_bx[i]], axis=1).reshape(2, 1, 2 * GROUP_W)
        lru_params = [lru_conv_w[i], row(lru_conv_b[i]), lru_w, lru_bias, lru_lambda[i].reshape(2, 1, GROUP_W)]
        lru_pieces = [(512, OFF_LRU_X, True)]
        lru_fin = [(512, OFF_LRU_GATE, False)]
        lru_zero = [jnp.zeros((bsz, 1, GROUP_W), F32)]
        yc_t, yc_l = _bidir_mixer(_lru_kernel, lru_pieces, lru_fin, LRU_CHUNK, ut_row, ul_row, lru_zero,
                                  lru_params, "lru")

        wg = jnp.zeros((2, 128, GLA_HEADS * GLA_DK), F32)
        wg = wg.at[0, :GLA_RANK].set(gla_wg2[i, 0]).at[1, GLA_RANK:2 * GLA_RANK].set(gla_wg2[i, 1])
        gla_params = [wg.astype(BF16), gla_bg[i].reshape(2, 1, GLA_HEADS * GLA_DK), row(gla_norm_g[i])]
        gla_pieces = [(512, OFF_GLA_QK, False), (512, OFF_GLA_V, False), (128, OFF_GLA_G1, False)]
        gla_fin = [(512, OFF_GLA_R, False)]
        gla_zero = [jnp.zeros((bsz, GROUP_W, GLA_HEADS * GLA_DK), F32)]
        yd_t, yd_l = _bidir_mixer(_gla_kernel, gla_pieces, gla_fin, GLA_CHUNK, ut_col, ul_col, gla_zero,
                                  gla_params, "gla")

        yb_l = _from_colmajor(yb_l, bsz, rows)
        yd_l = _from_colmajor(yd_l, bsz, rows)

        w_out_b = w_out[i].astype(BF16).reshape(4, GROUP_W, dm)
        wg_b, wu_b, wd_b = w_gate[i].astype(BF16), w_up[i].astype(BF16), w_down[i].astype(BF16)
        fg = row(final_g)

        def tail(xs, ys, m, rows_per_group, final):
            ys = [y.reshape(-1, GROUP_W) for y in ys]
            xs, h2 = _outproj(ys, w_out_b, xs, m[2], row(norm2_g[i]), m[3], m[4], rows_per_group)
            act = _ffn_up(h2, wg_b, wu_b)
            return _ffn_down(act, wd_b, xs, m[5], fg, final, rows_per_group)

        xl = tail(xl, [ya_l, yb_l, yc_l, yd_l], m_l, length, i == depth - 1)
        if need_ctx:
            xt = tail(xt, [ya_t, yb_t, yc_t, yd_t], m_t, bsz * lctx, False)
    return xl.reshape(bsz, length, dm)
```

```python
import functools

import jax
import jax.numpy as jnp
from jax import lax
from jax.experimental import pallas as pl
from jax.experimental.pallas import tpu as pltpu

F32 = jnp.float32
BF16 = jnp.bfloat16
EPS = 1e-6
NEG = -1e30

D_MODEL = 2048
GRID_W = 64
GROUP_W = 512
N_MOD = 6
CONV_W = 5
HALO = 8

SSD_HEADS = 8
SSD_CHUNK = 128
ML_HEADS = 4
ML_CHUNK = 64
ML_HEAD_GROUP = 4
LRU_C = 8.0
LRU_CHUNK = 128
GLA_HEADS = 4
GLA_DK = 64
GLA_RANK = 16
GLA_TAU = 16.0
GLA_CHUNK = 64
GLA_SUB = 16

OFF_SSD_XBC = 0
OFF_LRU_X = 1024
OFF_LRU_GATE = 1536
OFF_SSD_Z = 2048
OFF_SSD_DT = 2560
N_ROW = 2688
OFF_ML_Q = 0
OFF_ML_K = 512
OFF_ML_V = 1024
OFF_ML_O = 1536
OFF_GLA_QK = 2048
OFF_GLA_V = 2560
OFF_GLA_R = 3072
OFF_ML_G = 3584
OFF_GLA_G1 = 3712
N_COL = 3840

VMEM_LIMIT = 56 * 1024 * 1024
TM_PROJ = 256
TM_OUT = 512
TM_FFN_UP = 2048
TN_FFN_UP = 512
TM_FFN_DOWN = 256


def _cparams(sem):
    return pltpu.CompilerParams(dimension_semantics=sem, vmem_limit_bytes=VMEM_LIMIT)


def _resident(shape):
    nd = len(shape)
    return pl.BlockSpec(shape, lambda *_: (0,) * nd, pipeline_mode=pl.Buffered(1))


def _sigmoid(x):
    return 0.5 + 0.5 * jnp.tanh(0.5 * x)


def _silu(x):
    return x * _sigmoid(x)


def _softplus(x):
    return jnp.maximum(x, 0.0) + jnp.log(1.0 + jnp.exp(-jnp.abs(x)))


def _log_sigmoid(x):
    return jnp.minimum(x, 0.0) - jnp.log(1.0 + jnp.exp(-jnp.abs(x)))


def _gelu_tanh(x):
    return 0.5 * x * (1.0 + jnp.tanh(0.7978845608028654 * (x + 0.044715 * (x * x * x))))


def _rms(x):
    return x * lax.rsqrt(jnp.mean(x * x, axis=-1, keepdims=True) + EPS)


def _dot(a, b):
    return jnp.dot(a, b, preferred_element_type=F32)


def _dot_nt(a, b):
    return lax.dot_general(a, b, (((1,), (1,)), ((), ())), preferred_element_type=F32)


def _dot_exact(a, b):
    return jnp.dot(a, b, precision=lax.Precision.HIGHEST, preferred_element_type=F32)


def _scan_mask(t, rev):
    ri = lax.broadcasted_iota(jnp.int32, (t, t), 0)
    ci = lax.broadcasted_iota(jnp.int32, (t, t), 1)
    return (ci >= ri) if rev else (ci <= ri)


def _conv5(prev, main, nxt, w, bias, first, last):
    t = main.shape[0]
    prev = jnp.where(first, 0.0, prev)
    nxt = jnp.where(last, 0.0, nxt)
    ext = jnp.concatenate([prev, main, nxt], axis=0)
    n = t + 2 * HALO
    acc = None
    for k in range(CONV_W):
        sh = (CONV_W // 2 - k) % n
        r = ext if sh == 0 else pltpu.roll(ext, sh, axis=0)
        term = r[HALO:HALO + t] * w[k:k + 1, :]
        acc = term if acc is None else acc + term
    return acc + bias


def _mod_kernel(c_ref, w_ref, b_ref, o_ref):
    s = _silu(c_ref[...]).astype(BF16)
    o_ref[...] = _dot(s, w_ref[...].astype(BF16)) + b_ref[...]


def _inproj_kernel(x_ref, g_ref, sh_ref, sc_ref, w_ref, o_ref):
    y = _rms(x_ref[...]) * g_ref[...]
    h = (y * (1.0 + sc_ref[0]) + sh_ref[0]).astype(BF16)
    o_ref[...] = _dot(h, w_ref[...])


def _outproj_kernel(ya_ref, yb_ref, yc_ref, yd_ref, w_ref, x_ref, gate_ref, g_ref, sh_ref, sc_ref,
                    xo_ref, h_ref):
    acc = _dot(ya_ref[...], w_ref[0])
    acc = acc + _dot(yb_ref[...], w_ref[1])
    acc = acc + _dot(yc_ref[...], w_ref[2])
    acc = acc + _dot(yd_ref[...], w_ref[3])
    xn = x_ref[...] + gate_ref[0] * acc
    xo_ref[...] = xn
    y = _rms(xn) * g_ref[...]
    h_ref[...] = (y * (1.0 + sc_ref[0]) + sh_ref[0]).astype(BF16)


def _ffn_up_kernel(h_ref, wg_ref, wu_ref, o_ref):
    h = h_ref[...]
    o_ref[...] = (_silu(_dot(h, wg_ref[...])) * _dot(h, wu_ref[...])).astype(BF16)


def _ffn_down_kernel(final, a_ref, w_ref, x_ref, gate_ref, fg_ref, o_ref):
    xn = x_ref[...] + gate_ref[0] * _dot(a_ref[...], w_ref[...])
    if final:
        xn = _rms(xn) * fg_ref[...]
    o_ref[...] = xn


def _ssd_kernel(rev, finish, nchunks, nb, *refs):
    if finish:
        (xc_ref, dt_ref, z_ref, yp_ref, s0_ref, cw_ref, cb_ref, dtb_ref, alog_ref,
         dsk_ref, ng_ref, y_ref, s_ref) = refs
    else:
        (xbc_ref, xp_ref, xn_ref, dt_ref, s0_ref, cw_ref, cb_ref, dtb_ref, alog_ref,
         dsk_ref, ng_ref, y_ref, xc_ref, s_ref) = refs
    t = SSD_CHUNK
    d = 1 if rev else 0
    j = pl.program_id(0)
    c = (nchunks - 1 - j) if rev else j

    @pl.when(j == 0)
    def _():
        s_ref[...] = s0_ref[...]

    mask = _scan_mask(t, rev)
    maskf = mask.astype(F32)
    er = 0 if rev else t - 1
    lo = lax.broadcasted_iota(jnp.int32, (t, 128), 1) < 64
    lo_rows = lax.broadcasted_iota(jnp.int32, (128, 128), 0) < 64
    cw, cbias = cw_ref[...], cb_ref[...]
    nega = -jnp.exp(alog_ref[...])

    bs = range(nb)
    if finish:
        xbc = [xc_ref[bi] for bi in bs]
    else:
        xbc = [_silu(_conv5(xp_ref[bi], xbc_ref[bi], xn_ref[bi], cw, cbias, c == 0, c == nchunks - 1))
               for bi in bs]
        for bi in bs:
            xc_ref[bi] = xbc[bi]
    dt_all = [_softplus(dt_ref[bi] + dtb_ref[...]) for bi in bs]
    b_all = [_dot_exact(maskf, dt_all[bi] * nega) for bi in bs]
    b_t = [b_all[bi].T for bi in bs]
    dt_t = [dt_all[bi].T for bi in bs]
    ys = [[None] * (SSD_HEADS // 2) for _ in bs]
    for p in range(SSD_HEADS // 2):
        g = p // 2
        bm = [xbc[bi][:, 512 + 128 * g:640 + 128 * g].astype(BF16) for bi in bs]
        cm = [xbc[bi][:, 768 + 128 * g:896 + 128 * g].astype(BF16) for bi in bs]
        cb = [_dot_nt(cm[bi], bm[bi]) for bi in bs]
        xpair = [xbc[bi][:, 128 * p:128 * p + 128] for bi in bs]
        xpair_b = [xpair[bi].astype(BF16) for bi in bs]
        s_pair = [s_ref[bi, p] for bi in bs]
        y_inter = [_dot_nt(cm[bi], s_pair[bi].astype(BF16)) for bi in bs]
        yh, eb, coef, dec = [], [], [], []
        for hh in range(2):
            l = d * SSD_HEADS + 2 * p + hh
            bcol = [b_all[bi][:, l:l + 1] for bi in bs]
            seg = [jnp.exp(jnp.where(mask, bcol[bi] - b_t[bi][l:l + 1, :], NEG)) for bi in bs]
            w = [(cb[bi] * seg[bi] * dt_t[bi][l:l + 1, :]).astype(BF16) for bi in bs]
            yh.append([_dot(w[bi], xpair_b[bi]) for bi in bs])
            eb.append([jnp.exp(bcol[bi]) for bi in bs])
            bl = [b_all[bi][er:er + 1, l:l + 1] for bi in bs]
            coef.append([jnp.exp(bl[bi] - bcol[bi]) * dt_all[bi][:, l:l + 1] for bi in bs])
            dec.append([jnp.exp(bl[bi]) for bi in bs])
        xw = [(xpair[bi] * jnp.where(lo, coef[0][bi], coef[1][bi])).T.astype(BF16) for bi in bs]
        upd = [_dot(xw[bi], bm[bi]) for bi in bs]
        for bi in bs:
            ys[bi][p] = (jnp.where(lo, yh[0][bi], yh[1][bi])
                         + y_inter[bi] * jnp.where(lo, eb[0][bi], eb[1][bi]))
            s_ref[bi, p] = jnp.where(lo_rows, dec[0][bi], dec[1][bi]) * s_pair[bi] + upd[bi]
    for bi in bs:
        y = jnp.concatenate(ys[bi], axis=1)
        if finish:
            y = y + yp_ref[bi] + dsk_ref[...] * xbc[bi][:, :GROUP_W]
            y = _rms(y * _silu(z_ref[bi])) * ng_ref[...]
            y_ref[bi] = y.astype(y_ref.dtype)
        else:
            y_ref[bi] = y


def _mlstm_kernel(rev, finish, nchunks, nb, *refs):
    if finish:
        (qkc_ref, v_ref, g_ref, o_ref, yp_ref, cs0_ref, sm0_ref,
         cw_ref, cb_ref, gb_ref, ng_ref, y_ref, cs_ref, sm_ref) = refs
    else:
        (q_ref, qp_ref, qn_ref, k_ref, kp_ref, kn_ref, v_ref, g_ref, cs0_ref, sm0_ref,
         cw_ref, cb_ref, gb_ref, ng_ref, y_ref, qkc_ref, cs_ref, sm_ref) = refs
    t = ML_CHUNK
    d = 1 if rev else 0
    j = pl.program_id(0)
    c = (nchunks - 1 - j) if rev else j
    first, last = c == 0, c == nchunks - 1

    @pl.when(j == 0)
    def _():
        cs_ref[...] = cs0_ref[...]
        sm_ref[...] = sm0_ref[...]

    cw = cw_ref[...]
    cbias = cb_ref[...]
    mask = _scan_mask(t, rev)
    maskf = mask.astype(F32)
    er = 0 if rev else t - 1

    bs = range(nb)
    if finish:
        q = [qkc_ref[bi][:, :GROUP_W] for bi in bs]
        k = [qkc_ref[bi][:, GROUP_W:] for bi in bs]
    else:
        q = [_silu(_conv5(qp_ref[bi], q_ref[bi], qn_ref[bi], cw[:, :GROUP_W], cbias[:, :GROUP_W], first, last))
             for bi in bs]
        k = [_silu(_conv5(kp_ref[bi], k_ref[bi], kn_ref[bi], cw[:, GROUP_W:], cbias[:, GROUP_W:], first, last))
             * (128.0 ** -0.5) for bi in bs]
        for bi in bs:
            qkc_ref[bi, :, :GROUP_W] = q[bi]
            qkc_ref[bi, :, GROUP_W:] = k[bi]
    v = [v_ref[bi] for bi in bs]
    gts = [g_ref[bi] + gb_ref[...] for bi in bs]
    b_all = [_dot_exact(maskf, _log_sigmoid(gts[bi])) for bi in bs]
    g_t = [gts[bi].T for bi in bs]
    b_t = [b_all[bi].T for bi in bs]
    ys = [[None] * ML_HEADS for _ in bs]
    for h0 in range(0, ML_HEADS, ML_HEAD_GROUP):
        ch = [(bi, h) for h in range(h0, h0 + ML_HEAD_GROUP) for bi in bs]
        cs_ = range(len(ch))
        li = [d * ML_HEADS + h for _, h in ch]
        lf = [2 * ML_HEADS + l for l in li]
        sl = [slice(128 * h, 128 * h + 128) for _, h in ch]
        bcol = [b_all[bi][:, lf[c]:lf[c] + 1] for c, (bi, h) in enumerate(ch)]
        m_prev = [sm_ref[bi, ML_HEADS + h:ML_HEADS + h + 1, 0:1] for bi, h in ch]
        ns = [sm_ref[bi, h:h + 1, :] for bi, h in ch]
        cs = [cs_ref[bi, h] for bi, h in ch]
        dmat = [jnp.where(mask, bcol[c] - b_t[bi][lf[c]:lf[c] + 1, :] + g_t[bi][li[c]:li[c] + 1, :], NEG)
                for c, (bi, h) in enumerate(ch)]
        inter = [bcol[c] + m_prev[c] for c in cs_]
        mt = [jnp.maximum(inter[c], jnp.max(dmat[c], axis=1, keepdims=True)) for c in cs_]
        w = [jnp.exp(dmat[c] - mt[c]) for c in cs_]
        sc = [jnp.exp(inter[c] - mt[c]) for c in cs_]
        qb = [q[bi][:, sl[c]].astype(BF16) for c, (bi, h) in enumerate(ch)]
        kb = [k[bi][:, sl[c]].astype(BF16) for c, (bi, h) in enumerate(ch)]
        sw = [_dot_nt(qb[c], kb[c]) * w[c] for c in cs_]
        num = [_dot(sw[c].astype(BF16), v[bi][:, sl[c]].astype(BF16)) for c, (bi, h) in enumerate(ch)]
        qc = [_dot_nt(qb[c], cs[c].astype(BF16)) for c in cs_]
        den = [jnp.sum(sw[c], axis=1, keepdims=True)
               + sc[c] * jnp.sum(q[bi][:, sl[c]] * ns[c], axis=1, keepdims=True) for c, (bi, h) in enumerate(ch)]
        for c, (bi, h) in enumerate(ch):
            inv = 1.0 / jnp.maximum(jnp.abs(den[c]), jnp.exp(-mt[c]))
            ys[bi][h] = (num[c] + sc[c] * qc[c]) * inv
        bl = [b_all[bi][er:er + 1, lf[c]:lf[c] + 1] for c, (bi, h) in enumerate(ch)]
        tail = [bl[c] - bcol[c] + gts[bi][:, li[c]:li[c] + 1] for c, (bi, h) in enumerate(ch)]
        m_new = [jnp.maximum(bl[c] + m_prev[c], jnp.max(tail[c], axis=0, keepdims=True)) for c in cs_]
        ws = [jnp.exp(tail[c] - m_new[c]) for c in cs_]
        sc_end = [jnp.exp(bl[c] + m_prev[c] - m_new[c]) for c in cs_]
        upd = [_dot((v[bi][:, sl[c]] * ws[c]).T.astype(BF16), kb[c]) for c, (bi, h) in enumerate(ch)]
        for c, (bi, h) in enumerate(ch):
            cs_ref[bi, h] = sc_end[c] * cs[c] + upd[c]
            sm_ref[bi, h:h + 1, :] = sc_end[c] * ns[c] + jnp.sum(ws[c] * k[bi][:, sl[c]], axis=0, keepdims=True)
            sm_ref[bi, ML_HEADS + h:ML_HEADS + h + 1, :] = jnp.broadcast_to(m_new[c], (1, 128))
    for bi in bs:
        if finish:
            yp = yp_ref[bi]
            outs = [_rms(ys[bi][h] + yp[:, 128 * h:128 * h + 128]) for h in range(ML_HEADS)]
            y = jnp.concatenate(outs, axis=1) * ng_ref[...]
            y_ref[bi] = (_sigmoid(o_ref[bi]) * y).astype(y_ref.dtype)
        else:
            y_ref[bi] = jnp.concatenate(ys[bi], axis=1)


def _lru_kernel(rev, finish, nchunks, nb, *refs):
    if finish:
        (xf_ref, gate_ref, yp_ref, h0_ref, cw_ref, cb_ref, w_ref, bias_ref, lam_ref,
         y_ref, h_ref) = refs
    else:
        (x_ref, xp_ref, xn_ref, h0_ref, cw_ref, cb_ref, w_ref, bias_ref, lam_ref,
         y_ref, xf_ref, h_ref) = refs
    t = LRU_CHUNK
    d = 1 if rev else 0
    j = pl.program_id(0)
    c = (nchunks - 1 - j) if rev else j

    @pl.when(j == 0)
    def _():
        h_ref[...] = h0_ref[...]

    cw, cbias = cw_ref[...], cb_ref[...]
    lsl = LRU_C * _log_sigmoid(lam_ref[d])
    sub = lax.broadcasted_iota(jnp.int32, (8, GROUP_W), 0)
    n = t // 8

    for bi in range(nb):
        if finish:
            xf = xf_ref[bi]
        else:
            xf = _conv5(xp_ref[bi], x_ref[bi], xn_ref[bi], cw, cbias, c == 0, c == nchunks - 1)
            xf_ref[bi] = xf
        pre = _dot(xf.astype(BF16), w_ref[d]) + bias_ref[d]
        r = _sigmoid(pre[:, :GROUP_W])
        ig = _sigmoid(pre[:, GROUP_W:])
        loga = r * lsl
        a = jnp.exp(loga)
        bx = jnp.sqrt(-jnp.tanh(loga) * (1.0 + a * a)) * ig * xf

        a_loc, h_loc = [], []
        for i in range(n):
            ai = a[8 * i:8 * i + 8]
            hi = bx[8 * i:8 * i + 8]
            for sh in (1, 2, 4):
                if rev:
                    valid = sub < 8 - sh
                    a_s = pltpu.roll(ai, 8 - sh, axis=0)
                    h_s = pltpu.roll(hi, 8 - sh, axis=0)
                else:
                    valid = sub >= sh
                    a_s = pltpu.roll(ai, sh, axis=0)
                    h_s = pltpu.roll(hi, sh, axis=0)
                hi = jnp.where(valid, ai * h_s + hi, hi)
                ai = jnp.where(valid, ai * a_s, ai)
            a_loc.append(ai)
            h_loc.append(hi)
        carry = h_ref[bi]
        outs = [None] * n
        for i in (range(n - 1, -1, -1) if rev else range(n)):
            hi = h_loc[i] + a_loc[i] * carry
            outs[i] = hi
            carry = hi[0:1] if rev else hi[7:8]
        h_ref[bi] = carry
        hs = jnp.concatenate(outs, axis=0)
        if finish:
            y_ref[bi] = ((hs + yp_ref[bi]) * _gelu_tanh(gate_ref[bi])).astype(y_ref.dtype)
        else:
            y_ref[bi] = hs


def _gla_kernel(rev, finish, nchunks, nb, *refs):
    if finish:
        (qk_ref, v_ref, g1_ref, r_ref, yp_ref, s0_ref, wg_ref, bg_ref, ng_ref, y_ref, s_ref) = refs
    else:
        (qk_ref, v_ref, g1_ref, s0_ref, wg_ref, bg_ref, ng_ref, y_ref, s_ref) = refs
    t = GLA_CHUNK
    d = 1 if rev else 0
    dkk = GLA_HEADS * GLA_DK
    j = pl.program_id(0)

    @pl.when(j == 0)
    def _():
        s_ref[...] = s0_ref[...]

    mask = _scan_mask(t, rev)
    maskf = mask.astype(F32)
    rows = lax.broadcasted_iota(jnp.int32, (t, 1), 0)
    lane_head = lax.broadcasted_iota(jnp.int32, (GLA_SUB, dkk), 1) // GLA_DK
    blk = (lax.broadcasted_iota(jnp.int32, (GROUP_W, dkk), 0) // 128
           == lax.broadcasted_iota(jnp.int32, (GROUP_W, dkk), 1) // GLA_DK)
    er = 0 if rev else t - 1

    bs = range(nb)
    q = [qk_ref[bi][:, :dkk] * (GLA_DK ** -0.5) for bi in bs]
    k = [qk_ref[bi][:, dkk:] for bi in bs]
    v = [v_ref[bi] for bi in bs]
    vb = [v[bi].astype(BF16) for bi in bs]
    glog = [_dot(g1_ref[bi].astype(BF16), wg_ref[d]) + bg_ref[d] for bi in bs]
    la = [_log_sigmoid(glog[bi]) * (1.0 / GLA_TAU) for bi in bs]
    b = [_dot_exact(maskf, la[bi]) for bi in bs]
    excl = [b[bi] - la[bi] for bi in bs]
    s_t = [s_ref[bi] for bi in bs]
    o_inter = [_dot_nt((q[bi] * jnp.exp(b[bi])).astype(BF16), s_t[bi].astype(BF16)) for bi in bs]

    att_blocks = [[None] * (t // GLA_SUB) for _ in bs]
    for i in range(t // GLA_SUB):
        r0 = GLA_SUB * i
        if rev:
            ref_row = [excl[bi][r0 + GLA_SUB - 1:r0 + GLA_SUB] for bi in bs]
            kvalid = rows >= r0
        else:
            ref_row = [excl[bi][r0:r0 + 1] for bi in bs]
            kvalid = rows < r0 + GLA_SUB
        qi = [q[bi][r0:r0 + GLA_SUB] * jnp.exp(b[bi][r0:r0 + GLA_SUB] - ref_row[bi]) for bi in bs]
        ki = [(k[bi] * jnp.exp(jnp.where(kvalid, ref_row[bi] - b[bi], NEG))).astype(BF16) for bi in bs]
        qs = [jnp.concatenate([jnp.where(lane_head == h, qi[bi], 0.0) for h in range(GLA_HEADS)],
                              axis=0).astype(BF16) for bi in bs]
        for bi in bs:
            att_blocks[bi][i] = _dot_nt(qs[bi], ki[bi])
    outs = [[None] * GLA_HEADS for _ in bs]
    for h in range(GLA_HEADS):
        att = [jnp.where(mask, jnp.concatenate([ab[GLA_SUB * h:GLA_SUB * h + GLA_SUB] for ab in att_blocks[bi]],
                                               axis=0), 0.0).astype(BF16) for bi in bs]
        for bi in bs:
            outs[bi][h] = _dot(att[bi], vb[bi][:, 128 * h:128 * h + 128])

    bl = [b[bi][er:er + 1] for bi in bs]
    kd = [(k[bi] * jnp.exp(bl[bi] - b[bi])).astype(BF16) for bi in bs]
    v_t = [jnp.concatenate([v[bi][:, 128 * h:128 * h + 128].T for h in range(GLA_HEADS)], axis=0).astype(BF16)
           for bi in bs]
    upd = [_dot(v_t[bi], kd[bi]) for bi in bs]
    for bi in bs:
        s_ref[bi] = s_t[bi] * jnp.exp(bl[bi]) + jnp.where(blk, upd[bi], 0.0)
    for bi in bs:
        o = jnp.concatenate(outs[bi], axis=1) + o_inter[bi]
        if finish:
            o = o + yp_ref[bi]
            o = jnp.concatenate([_rms(o[:, 128 * h:128 * h + 128]) for h in range(GLA_HEADS)], axis=1)
            y_ref[bi] = (o * ng_ref[...] * _silu(r_ref[bi])).astype(y_ref.dtype)
        else:
            y_ref[bi] = o


def _chunk_specs(nb, width, off, t, length, rev, nchunks, halo):
    ob = off // width
    tb = t // HALO

    def cj(j):
        return (nchunks - 1 - j) if rev else j

    main = pl.BlockSpec((nb, t, width), lambda j: (0, cj(j), ob))
    if not halo:
        return [main]
    prev = pl.BlockSpec((nb, HALO, width), lambda j: (0, jnp.maximum(cj(j) * tb - 1, 0), ob))
    nxt = pl.BlockSpec((nb, HALO, width), lambda j: (0, jnp.minimum((cj(j) + 1) * tb, length // HALO - 1), ob))
    return [main, prev, nxt]


def _run_mixer(body, pieces, t, rev, finish, ypart, states, params, name, save_width=0):
    nb, length = pieces[0][0].shape[0], pieces[0][0].shape[1]
    nchunks = length // t
    cj = (lambda j: nchunks - 1 - j) if rev else (lambda j: j)
    row_spec = lambda width: pl.BlockSpec((nb, t, width), lambda j: (0, cj(j), 0))
    in_specs, args = [], []
    for arr, width, off, halo in pieces:
        sp = _chunk_specs(nb, width, off, t, length, rev, nchunks, halo)
        in_specs += sp
        args += [arr] * len(sp)
    if finish:
        in_specs.append(row_spec(GROUP_W))
        args.append(ypart)
    for a in list(states) + list(params):
        in_specs.append(pl.BlockSpec(a.shape, lambda j, _nd=a.ndim: (0,) * _nd))
        args.append(a)
    out_shape = [jax.ShapeDtypeStruct((nb, length, GROUP_W), BF16 if finish else F32)]
    out_specs = [row_spec(GROUP_W)]
    if save_width:
        out_shape.append(jax.ShapeDtypeStruct((nb, length, save_width), F32))
        out_specs.append(row_spec(save_width))
    for s in states:
        out_shape.append(jax.ShapeDtypeStruct(s.shape, s.dtype))
        out_specs.append(pl.BlockSpec(s.shape, lambda j, _nd=s.ndim: (0,) * _nd))
    res = pl.pallas_call(
        functools.partial(body, rev, finish, nchunks, nb),
        grid=(nchunks,),
        in_specs=in_specs,
        out_specs=out_specs,
        out_shape=out_shape,
        compiler_params=_cparams(("arbitrary",)),
        name=name,
    )(*args)
    nfix = 2 if save_width else 1
    return res[0], (res[1] if save_width else None), list(res[nfix:])


def _bidir_mixer(body, bwd_pieces, fwd_pieces, save_width, t, u_ctx, u_lat, zero_states, params, name):
    def run(u, rev, ypart, saved, states, tag):
        src = {"u": u, "saved": saved}
        pieces = [(src[s], w, o, h) for s, w, o, h in (bwd_pieces if rev else fwd_pieces)]
        return _run_mixer(body, pieces, t, rev, not rev, ypart, states, params, name + tag,
                          save_width if rev else 0)

    yb_c, sv_c, st_b = run(u_ctx, True, None, None, zero_states, "_ctx_bwd")
    y_c, _, st_f = run(u_ctx, False, yb_c, sv_c, zero_states, "_ctx_fwd")
    yb_l, sv_l, _ = run(u_lat, True, None, None, st_b, "_lat_bwd")
    y_l, _, _ = run(u_lat, False, yb_l, sv_l, st_f, "_lat_fwd")
    return y_c, y_l


def _mods(cvec, w_mod, b_mod):
    depth, dm, nm = w_mod.shape
    tn = 1024
    return pl.pallas_call(
        _mod_kernel,
        grid=(depth, nm // tn),
        in_specs=[pl.BlockSpec((8, dm), lambda l, j: (0, 0)),
                  pl.BlockSpec((None, dm, tn), lambda l, j: (l, 0, j)),
                  pl.BlockSpec((None, 1, tn), lambda l, j: (l, 0, j))],
        out_specs=pl.BlockSpec((None, 8, tn), lambda l, j: (l, 0, j)),
        out_shape=jax.ShapeDtypeStruct((depth, 8, nm), F32),
        compiler_params=_cparams(("parallel", "parallel")),
        name="mods",
    )(cvec, w_mod, b_mod.reshape(depth, 1, nm))


def _inproj(x, g, shift, scale, w, rows_per_group):
    m, dm = x.shape
    n = w.shape[1]
    tm = TM_PROJ
    grp = lambda i: (i * tm // rows_per_group, 0, 0)
    return pl.pallas_call(
        _inproj_kernel,
        grid=(m // tm,),
        in_specs=[pl.BlockSpec((tm, dm), lambda i: (i, 0)),
                  _resident((1, dm)),
                  pl.BlockSpec((1, 1, dm), grp),
                  pl.BlockSpec((1, 1, dm), grp),
                  _resident((dm, n))],
        out_specs=pl.BlockSpec((tm, n), lambda i: (i, 0)),
        out_shape=jax.ShapeDtypeStruct((m, n), F32),
        compiler_params=_cparams(("parallel",)),
        name="inproj",
    )(x, g, shift, scale, w)


def _outproj(ys, w, x, gate, g, shift, scale, rows_per_group):
    m, dm = x.shape
    tm = min(TM_OUT, rows_per_group)
    grp = lambda i: (i * tm // rows_per_group, 0, 0)
    yspec = pl.BlockSpec((tm, GROUP_W), lambda i: (i, 0))
    return pl.pallas_call(
        _outproj_kernel,
        grid=(m // tm,),
        in_specs=[yspec, yspec, yspec, yspec,
                  _resident((4, GROUP_W, dm)),
                  pl.BlockSpec((tm, dm), lambda i: (i, 0)),
                  pl.BlockSpec((1, 1, dm), grp),
                  _resident((1, dm)),
                  pl.BlockSpec((1, 1, dm), grp),
                  pl.BlockSpec((1, 1, dm), grp)],
        out_specs=[pl.BlockSpec((tm, dm), lambda i: (i, 0)), pl.BlockSpec((tm, dm), lambda i: (i, 0))],
        out_shape=[jax.ShapeDtypeStruct((m, dm), F32), jax.ShapeDtypeStruct((m, dm), BF16)],
        compiler_params=_cparams(("parallel",)),
        name="outproj",
    )(*ys, w, x, gate, g, shift, scale)


def _ffn_up(h, wg, wu):
    m, dm = h.shape
    dff = wg.shape[1]
    tm = min(TM_FFN_UP, m)
    tn = TN_FFN_UP
    return pl.pallas_call(
        _ffn_up_kernel,
        grid=(m // tm, dff // tn),
        in_specs=[pl.BlockSpec((tm, dm), lambda i, j: (i, 0)),
                  pl.BlockSpec((dm, tn), lambda i, j: (0, j)),
                  pl.BlockSpec((dm, tn), lambda i, j: (0, j))],
        out_specs=pl.BlockSpec((tm, tn), lambda i, j: (i, j)),
        out_shape=jax.ShapeDtypeStruct((m, dff), BF16),
        compiler_params=_cparams(("parallel", "arbitrary")),
        name="ffn_up",
    )(h, wg, wu)


def _ffn_down(a, w, x, gate, final_g, final, rows_per_group):
    m, dm = x.shape
    dff = a.shape[1]
    tm = TM_FFN_DOWN
    grp = lambda i: (i * tm // rows_per_group, 0, 0)
    return pl.pallas_call(
        functools.partial(_ffn_down_kernel, final),
        grid=(m // tm,),
        in_specs=[pl.BlockSpec((tm, dff), lambda i: (i, 0)),
                  _resident((dff, dm)),
                  pl.BlockSpec((tm, dm), lambda i: (i, 0)),
                  pl.BlockSpec((1, 1, dm), grp),
                  _resident((1, dm))],
        out_specs=pl.BlockSpec((tm, dm), lambda i: (i, 0)),
        out_shape=jax.ShapeDtypeStruct((m, dm), F32),
        compiler_params=_cparams(("parallel",)),
        name="ffn_down",
    )(a, w, x, gate, final_g)


def _pad_cols(a, width):
    return jnp.pad(a, ((0, 0), (0, width - a.shape[1])))


def _split_w_in(w):
    ssd, ml, lru, gla = jnp.split(w, [1552, 1552 + 2064, 1552 + 2064 + 1024], axis=1)
    ssd_z, ssd_xbc, ssd_dt = ssd[:, :512], ssd[:, 512:1536], ssd[:, 1536:]
    ml_qkvo, ml_g = ml[:, :2048], ml[:, 2048:]
    lru_gate, lru_x = lru[:, :512], lru[:, 512:]
    gla_qkvr, gla_g1 = gla[:, :1536], gla[:, 1536:]
    w_row = jnp.concatenate([ssd_xbc, lru_x, lru_gate, ssd_z, _pad_cols(ssd_dt, 128)], axis=1)
    w_col = jnp.concatenate([ml_qkvo, gla_qkvr, _pad_cols(ml_g, 128), _pad_cols(gla_g1, 128)], axis=1)
    return w_row.astype(BF16), w_col.astype(BF16)


def _row128(a):
    return _pad_cols(a.reshape(1, -1).astype(F32), 128)


def _block_diag(w):
    nb, bi, bj = w.shape
    eye = jnp.eye(nb, dtype=w.dtype)
    return (eye[:, None, :, None] * w[:, :, None, :]).reshape(nb * bi, nb * bj)


def _to_colmajor(a, bsz, rows):
    n = a.shape[-1]
    return a.reshape(bsz, rows, GRID_W, n).transpose(0, 2, 1, 3).reshape(bsz, rows * GRID_W, n)


def _from_colmajor(a, bsz, rows):
    n = a.shape[-1]
    return a.reshape(bsz, GRID_W, rows, n).transpose(0, 2, 1, 3).reshape(bsz, rows * GRID_W, n)


def kernel(x, c, ctx, c_ctx, norm1_g, norm2_g, w_mod, b_mod, w_in, w_out, ssd_conv_w, ssd_conv_b, ssd_dt_bias, ssd_a_log, ssd_d, ssd_norm_g, ml_conv_w, ml_conv_b, ml_igate_b, ml_fgate_b, ml_norm_g, lru_conv_w, lru_conv_b, lru_wa, lru_ba, lru_wx, lru_bx, lru_lambda, gla_wg2, gla_bg, gla_norm_g, w_gate, w_up, w_down, final_g):
    bsz, length, dm = x.shape
    lctx = ctx.shape[1]
    depth = w_in.shape[0]
    rows = length // GRID_W
    assert dm == D_MODEL and length % SSD_CHUNK == 0 and lctx % SSD_CHUNK == 0

    cvec = jnp.concatenate([c, c_ctx[None, :], jnp.zeros((8 - bsz - 1, dm), F32)], axis=0)
    mods = _mods(cvec, w_mod, b_mod)

    xl = x.reshape(bsz * length, dm)
    xt = ctx.reshape(bsz * lctx, dm)
    row = lambda a: a.reshape(1, -1).astype(F32)

    for i in range(depth):
        need_ctx = i < depth - 1
        m_l = [mods[i, :bsz, k * dm:(k + 1) * dm].reshape(bsz, 1, dm) for k in range(N_MOD)]
        m_t = [mods[i, bsz:bsz + 1, k * dm:(k + 1) * dm].reshape(1, 1, dm) for k in range(N_MOD)]
        w_row, w_col = _split_w_in(w_in[i])
        g1 = row(norm1_g[i])
        xl_cm = _to_colmajor(xl.reshape(bsz, length, dm), bsz, rows).reshape(bsz * length, dm)
        ul_row = _inproj(xl, g1, m_l[0], m_l[1], w_row, length).reshape(bsz, length, N_ROW)
        ul_col = _inproj(xl_cm, g1, m_l[0], m_l[1], w_col, length).reshape(bsz, length, N_COL)
        ut_row = _inproj(xt, g1, m_t[0], m_t[1], w_row, bsz * lctx).reshape(bsz, lctx, N_ROW)
        ut_col = _inproj(xt, g1, m_t[0], m_t[1], w_col, bsz * lctx).reshape(bsz, lctx, N_COL)

        ssd_params = [ssd_conv_w[i], row(ssd_conv_b[i]), _row128(ssd_dt_bias[i]), _row128(ssd_a_log[i]),
                      row(jnp.repeat(ssd_d[i], GROUP_W // SSD_HEADS)), row(ssd_norm_g[i])]
        ssd_bwd = [("u", 1024, OFF_SSD_XBC, True), ("u", 128, OFF_SSD_DT, False)]
        ssd_fwd = [("saved", 1024, 0, False), ("u", 128, OFF_SSD_DT, False), ("u", 512, OFF_SSD_Z, False)]
        ssd_zero = [jnp.zeros((bsz, SSD_HEADS // 2, 128, 128), F32)]
        ya_t, ya_l = _bidir_mixer(_ssd_kernel, ssd_bwd, ssd_fwd, 1024, SSD_CHUNK, ut_row, ul_row, ssd_zero,
                                  ssd_params, "ssd")

        ml_gb = _row128(jnp.concatenate([ml_igate_b[i].reshape(-1), ml_fgate_b[i].reshape(-1)]))
        ml_params = [ml_conv_w[i], row(ml_conv_b[i]), ml_gb, row(ml_norm_g[i])]
        ml_bwd = [("u", 512, OFF_ML_Q, True), ("u", 512, OFF_ML_K, True), ("u", 512, OFF_ML_V, False),
                  ("u", 128, OFF_ML_G, False)]
        ml_fwd = [("saved", 1024, 0, False), ("u", 512, OFF_ML_V, False), ("u", 128, OFF_ML_G, False),
                  ("u", 512, OFF_ML_O, False)]
        ml_zero = [jnp.zeros((bsz, ML_HEADS, 128, 128), F32), jnp.zeros((bsz, 2 * ML_HEADS, 128), F32)]
        yb_t, yb_l = _bidir_mixer(_mlstm_kernel, ml_bwd, ml_fwd, 1024, ML_CHUNK, ut_col, ul_col, ml_zero,
                                  ml_params, "mlstm")

        lru_w = jnp.stack([jnp.concatenate([_block_diag(lru_wa[i, dd]), _block_diag(lru_wx[i, dd])], axis=1)
                           for dd in range(2)]).astype(BF16)
        lru_bias = jnp.concatenate([lru_ba[i], lru_bx[i]], axis=1).reshape(2, 1, 2 * GROUP_W)
        lru_params = [lru_conv_w[i], row(lru_conv_b[i]), lru_w, lru_bias, lru_lambda[i].reshape(2, 1, GROUP_W)]
        lru_bwd = [("u", 512, OFF_LRU_X, True)]
        lru_fwd = [("saved", 512, 0, False), ("u", 512, OFF_LRU_GATE, False)]
        lru_zero = [jnp.zeros((bsz, 1, GROUP_W), F32)]
        yc_t, yc_l = _bidir_mixer(_lru_kernel, lru_bwd, lru_fwd, 512, LRU_CHUNK, ut_row, ul_row, lru_zero,
                                  lru_params, "lru")

        wg = jnp.zeros((2, 128, GLA_HEADS * GLA_DK), F32)
        wg = wg.at[0, :GLA_RANK].set(gla_wg2[i, 0]).at[1, GLA_RANK:2 * GLA_RANK].set(gla_wg2[i, 1])
        gla_params = [wg.astype(BF16), gla_bg[i].reshape(2, 1, GLA_HEADS * GLA_DK), row(gla_norm_g[i])]
        gla_bwd = [("u", 512, OFF_GLA_QK, False), ("u", 512, OFF_GLA_V, False), ("u", 128, OFF_GLA_G1, False)]
        gla_fwd = gla_bwd + [("u", 512, OFF_GLA_R, False)]
        gla_zero = [jnp.zeros((bsz, GROUP_W, GLA_HEADS * GLA_DK), F32)]
        yd_t, yd_l = _bidir_mixer(_gla_kernel, gla_bwd, gla_fwd, 0, GLA_CHUNK, ut_col, ul_col, gla_zero,
                                  gla_params, "gla")

        yb_l = _from_colmajor(yb_l, bsz, rows)
        yd_l = _from_colmajor(yd_l, bsz, rows)

        w_out_b = w_out[i].astype(BF16).reshape(4, GROUP_W, dm)
        wg_b, wu_b, wd_b = w_gate[i].astype(BF16), w_up[i].astype(BF16), w_down[i].astype(BF16)
        fg = row(final_g)

        def tail(xs, ys, m, rows_per_group, final):
            ys = [y.reshape(-1, GROUP_W) for y in ys]
            xs, h2 = _outproj(ys, w_out_b, xs, m[2], row(norm2_g[i]), m[3], m[4], rows_per_group)
            act = _ffn_up(h2, wg_b, wu_b)
            return _ffn_down(act, wd_b, xs, m[5], fg, final, rows_per_group)

        xl = tail(xl, [ya_l, yb_l, yc_l, yd_l], m_l, length, i == depth - 1)
        if need_ctx:
            xt = tail(xt, [ya_t, yb_t, yc_t, yd_t], m_t, bsz * lctx, False)
    return xl.reshape(bsz, length, dm)
```

```python
import functools

import jax
import jax.numpy as jnp
from jax import lax
from jax.experimental import pallas as pl
from jax.experimental.pallas import tpu as pltpu

F32 = jnp.float32
BF16 = jnp.bfloat16
EPS = 1e-6
NEG = -1e30

D_MODEL = 2048
GRID_W = 64
GROUP_W = 512
N_MOD = 6
CONV_W = 5
HALO = 8

SSD_HEADS = 8
SSD_CHUNK = 128
ML_HEADS = 4
ML_CHUNK = 64
ML_HEAD_GROUP = 4
LRU_C = 8.0
LRU_CHUNK = 128
GLA_HEADS = 4
GLA_DK = 64
GLA_RANK = 16
GLA_TAU = 16.0
GLA_CHUNK = 64
GLA_SUB = 16

OFF_SSD_XBC = 0
OFF_LRU_X = 1024
OFF_LRU_GATE = 1536
OFF_SSD_Z = 2048
OFF_SSD_DT = 2560
N_ROW = 2688
OFF_ML_Q = 0
OFF_ML_K = 512
OFF_ML_V = 1024
OFF_ML_O = 1536
OFF_GLA_QK = 2048
OFF_GLA_V = 2560
OFF_GLA_R = 3072
OFF_ML_G = 3584
OFF_GLA_G1 = 3712
N_COL = 3840

VMEM_LIMIT = 56 * 1024 * 1024
TM_PROJ = 256
TM_OUT = 512
TM_FFN_UP = 2048
TN_FFN_UP = 512
TM_FFN_DOWN = 256


def _cparams(sem):
    return pltpu.CompilerParams(dimension_semantics=sem, vmem_limit_bytes=VMEM_LIMIT)


def _resident(shape):
    nd = len(shape)
    return pl.BlockSpec(shape, lambda *_: (0,) * nd, pipeline_mode=pl.Buffered(1))


def _sigmoid(x):
    return 0.5 + 0.5 * jnp.tanh(0.5 * x)


def _silu(x):
    return x * _sigmoid(x)


def _softplus(x):
    return jnp.maximum(x, 0.0) + jnp.log(1.0 + jnp.exp(-jnp.abs(x)))


def _log_sigmoid(x):
    return jnp.minimum(x, 0.0) - jnp.log(1.0 + jnp.exp(-jnp.abs(x)))


def _gelu_tanh(x):
    return 0.5 * x * (1.0 + jnp.tanh(0.7978845608028654 * (x + 0.044715 * (x * x * x))))


def _rms(x):
    return x * lax.rsqrt(jnp.mean(x * x, axis=-1, keepdims=True) + EPS)


def _dot(a, b):
    return jnp.dot(a, b, preferred_element_type=F32)


def _dot_nt(a, b):
    return lax.dot_general(a, b, (((1,), (1,)), ((), ())), preferred_element_type=F32)


def _dot_exact(a, b):
    return jnp.dot(a, b, precision=lax.Precision.HIGHEST, preferred_element_type=F32)


def _scan_mask(t, rev):
    ri = lax.broadcasted_iota(jnp.int32, (t, t), 0)
    ci = lax.broadcasted_iota(jnp.int32, (t, t), 1)
    return (ci >= ri) if rev else (ci <= ri)


def _conv5(prev, main, nxt, w, bias, first, last):
    t = main.shape[0]
    prev = jnp.where(first, 0.0, prev)
    nxt = jnp.where(last, 0.0, nxt)
    ext = jnp.concatenate([prev, main, nxt], axis=0)
    n = t + 2 * HALO
    acc = None
    for k in range(CONV_W):
        sh = (CONV_W // 2 - k) % n
        r = ext if sh == 0 else pltpu.roll(ext, sh, axis=0)
        term = r[HALO:HALO + t] * w[k:k + 1, :]
        acc = term if acc is None else acc + term
    return acc + bias


def _mod_kernel(c_ref, w_ref, b_ref, o_ref):
    s = _silu(c_ref[...]).astype(BF16)
    o_ref[...] = _dot(s, w_ref[...].astype(BF16)) + b_ref[...]


def _inproj_kernel(x_ref, g_ref, sh_ref, sc_ref, w_ref, o_ref):
    y = _rms(x_ref[...]) * g_ref[...]
    h = (y * (1.0 + sc_ref[0]) + sh_ref[0]).astype(BF16)
    o_ref[...] = _dot(h, w_ref[...])


def _outproj_kernel(ya_ref, yb_ref, yc_ref, yd_ref, w_ref, x_ref, gate_ref, g_ref, sh_ref, sc_ref,
                    xo_ref, h_ref):
    acc = _dot(ya_ref[...], w_ref[0])
    acc = acc + _dot(yb_ref[...], w_ref[1])
    acc = acc + _dot(yc_ref[...], w_ref[2])
    acc = acc + _dot(yd_ref[...], w_ref[3])
    xn = x_ref[...] + gate_ref[0] * acc
    xo_ref[...] = xn
    y = _rms(xn) * g_ref[...]
    h_ref[...] = (y * (1.0 + sc_ref[0]) + sh_ref[0]).astype(BF16)


def _ffn_up_kernel(h_ref, wg_ref, wu_ref, o_ref):
    h = h_ref[...]
    o_ref[...] = (_silu(_dot(h, wg_ref[...])) * _dot(h, wu_ref[...])).astype(BF16)


def _ffn_down_kernel(final, a_ref, w_ref, x_ref, gate_ref, fg_ref, o_ref):
    xn = x_ref[...] + gate_ref[0] * _dot(a_ref[...], w_ref[...])
    if final:
        xn = _rms(xn) * fg_ref[...]
    o_ref[...] = xn


def _ssd_kernel(rev, finish, nchunks, nb, *refs):
    if finish:
        (xc_ref, dt_ref, z_ref, yp_ref, s0_ref, cw_ref, cb_ref, dtb_ref, alog_ref,
         dsk_ref, ng_ref, y_ref, s_ref) = refs
    else:
        (xbc_ref, xp_ref, xn_ref, dt_ref, s0_ref, cw_ref, cb_ref, dtb_ref, alog_ref,
         dsk_ref, ng_ref, y_ref, xc_ref, s_ref) = refs
    t = SSD_CHUNK
    d = 1 if rev else 0
    j = pl.program_id(0)
    c = (nchunks - 1 - j) if rev else j

    @pl.when(j == 0)
    def _():
        s_ref[...] = s0_ref[...]

    mask = _scan_mask(t, rev)
    maskf = mask.astype(F32)
    er = 0 if rev else t - 1
    lo = lax.broadcasted_iota(jnp.int32, (t, 128), 1) < 64
    lo_rows = lax.broadcasted_iota(jnp.int32, (128, 128), 0) < 64
    cw, cbias = cw_ref[...], cb_ref[...]
    nega = -jnp.exp(alog_ref[...])

    bs = range(nb)
    if finish:
        xbc = [xc_ref[bi] for bi in bs]
    else:
        xbc = [_silu(_conv5(xp_ref[bi], xbc_ref[bi], xn_ref[bi], cw, cbias, c == 0, c == nchunks - 1))
               for bi in bs]
        for bi in bs:
            xc_ref[bi] = xbc[bi]
    dt_all = [_softplus(dt_ref[bi] + dtb_ref[...]) for bi in bs]
    b_all = [_dot_exact(maskf, dt_all[bi] * nega) for bi in bs]
    b_t = [b_all[bi].T for bi in bs]
    dt_t = [dt_all[bi].T for bi in bs]
    ys = [[None] * (SSD_HEADS // 2) for _ in bs]
    for p in range(SSD_HEADS // 2):
        g = p // 2
        bm = [xbc[bi][:, 512 + 128 * g:640 + 128 * g].astype(BF16) for bi in bs]
        cm = [xbc[bi][:, 768 + 128 * g:896 + 128 * g].astype(BF16) for bi in bs]
        cb = [_dot_nt(cm[bi], bm[bi]) for bi in bs]
        xpair = [xbc[bi][:, 128 * p:128 * p + 128] for bi in bs]
        xpair_b = [xpair[bi].astype(BF16) for bi in bs]
        s_pair = [s_ref[bi, p] for bi in bs]
        y_inter = [_dot_nt(cm[bi], s_pair[bi].astype(BF16)) for bi in bs]
        yh, eb, coef, dec = [], [], [], []
        for hh in range(2):
            l = d * SSD_HEADS + 2 * p + hh
            bcol = [b_all[bi][:, l:l + 1] for bi in bs]
            seg = [jnp.exp(jnp.where(mask, bcol[bi] - b_t[bi][l:l + 1, :], NEG)) for bi in bs]
            w = [(cb[bi] * seg[bi] * dt_t[bi][l:l + 1, :]).astype(BF16) for bi in bs]
            yh.append([_dot(w[bi], xpair_b[bi]) for bi in bs])
            eb.append([jnp.exp(bcol[bi]) for bi in bs])
            bl = [b_all[bi][er:er + 1, l:l + 1] for bi in bs]
            coef.append([jnp.exp(bl[bi] - bcol[bi]) * dt_all[bi][:, l:l + 1] for bi in bs])
            dec.append([jnp.exp(bl[bi]) for bi in bs])
        xw = [(xpair[bi] * jnp.where(lo, coef[0][bi], coef[1][bi])).T.astype(BF16) for bi in bs]
        upd = [_dot(xw[bi], bm[bi]) for bi in bs]
        for bi in bs:
            ys[bi][p] = (jnp.where(lo, yh[0][bi], yh[1][bi])
                         + y_inter[bi] * jnp.where(lo, eb[0][bi], eb[1][bi]))
            s_ref[bi, p] = jnp.where(lo_rows, dec[0][bi], dec[1][bi]) * s_pair[bi] + upd[bi]
    for bi in bs:
        y = jnp.concatenate(ys[bi], axis=1)
        if finish:
            y = y + yp_ref[bi] + dsk_ref[...] * xbc[bi][:, :GROUP_W]
            y = _rms(y * _silu(z_ref[bi])) * ng_ref[...]
            y_ref[bi] = y.astype(y_ref.dtype)
        else:
            y_ref[bi] = y


def _mlstm_kernel(rev, finish, nchunks, nb, *refs):
    if finish:
        (qkc_ref, v_ref, g_ref, o_ref, yp_ref, cs0_ref, sm0_ref,
         cw_ref, cb_ref, gb_ref, ng_ref, y_ref, cs_ref, sm_ref) = refs
    else:
        (q_ref, qp_ref, qn_ref, k_ref, kp_ref, kn_ref, v_ref, g_ref, cs0_ref, sm0_ref,
         cw_ref, cb_ref, gb_ref, ng_ref, y_ref, qkc_ref, cs_ref, sm_ref) = refs
    t = ML_CHUNK
    d = 1 if rev else 0
    j = pl.program_id(0)
    c = (nchunks - 1 - j) if rev else j
    first, last = c == 0, c == nchunks - 1

    @pl.when(j == 0)
    def _():
        cs_ref[...] = cs0_ref[...]
        sm_ref[...] = sm0_ref[...]

    cw = cw_ref[...]
    cbias = cb_ref[...]
    maskf = _scan_mask(t, rev).astype(F32)
    mask_t = _scan_mask(t, not rev)
    er = 0 if rev else t - 1

    bs = range(nb)
    if finish:
        q = [qkc_ref[bi][:, :GROUP_W] for bi in bs]
        k = [qkc_ref[bi][:, GROUP_W:] for bi in bs]
    else:
        q = [_silu(_conv5(qp_ref[bi], q_ref[bi], qn_ref[bi], cw[:, :GROUP_W], cbias[:, :GROUP_W], first, last))
             for bi in bs]
        k = [_silu(_conv5(kp_ref[bi], k_ref[bi], kn_ref[bi], cw[:, GROUP_W:], cbias[:, GROUP_W:], first, last))
             * (128.0 ** -0.5) for bi in bs]
        for bi in bs:
            qkc_ref[bi, :, :GROUP_W] = q[bi]
            qkc_ref[bi, :, GROUP_W:] = k[bi]
    v = [v_ref[bi] for bi in bs]
    gts = [g_ref[bi] + gb_ref[...] for bi in bs]
    b_all = [_dot_exact(maskf, _log_sigmoid(gts[bi])) for bi in bs]
    g_t = [gts[bi].T for bi in bs]
    b_t = [b_all[bi].T for bi in bs]
    ys = [[None] * ML_HEADS for _ in bs]
    for h0 in range(0, ML_HEADS, ML_HEAD_GROUP):
        ch = [(bi, h) for h in range(h0, h0 + ML_HEAD_GROUP) for bi in bs]
        cs_ = range(len(ch))
        li = [d * ML_HEADS + h for _, h in ch]
        lf = [2 * ML_HEADS + l for l in li]
        sl = [slice(128 * h, 128 * h + 128) for _, h in ch]
        brow = [b_t[bi][lf[c]:lf[c] + 1, :] for c, (bi, h) in enumerate(ch)]
        igrow = [g_t[bi][li[c]:li[c] + 1, :] for c, (bi, h) in enumerate(ch)]
        ccol = [gts[bi][:, li[c]:li[c] + 1] - b_all[bi][:, lf[c]:lf[c] + 1]
                for c, (bi, h) in enumerate(ch)]
        m_prev = [sm_ref[bi, ML_HEADS + h:ML_HEADS + h + 1, 0:1] for bi, h in ch]
        ns = [sm_ref[bi, h:h + 1, :] for bi, h in ch]
        cs = [cs_ref[bi, h] for bi, h in ch]
        dmat = [jnp.where(mask_t, brow[c] + ccol[c], NEG) for c in cs_]
        inter = [brow[c] + m_prev[c] for c in cs_]
        mt = [jnp.maximum(inter[c], jnp.max(dmat[c], axis=0, keepdims=True)) for c in cs_]
        w = [jnp.exp(dmat[c] - mt[c]) for c in cs_]
        sc = [jnp.exp(inter[c] - mt[c]) for c in cs_]
        qb = [q[bi][:, sl[c]].astype(BF16) for c, (bi, h) in enumerate(ch)]
        kb = [k[bi][:, sl[c]].astype(BF16) for c, (bi, h) in enumerate(ch)]
        sw = [_dot_nt(kb[c], qb[c]) * w[c] for c in cs_]
        v_t = [v[bi][:, sl[c]].T for c, (bi, h) in enumerate(ch)]
        num = [_dot(v_t[c].astype(BF16), sw[c].astype(BF16)) for c in cs_]
        qc = [_dot_nt(cs[c].astype(BF16), qb[c]) for c in cs_]
        qn = [_dot_nt(jnp.broadcast_to(ns[c], (8, 128)).astype(BF16), qb[c])[0:1] for c in cs_]
        den = [jnp.sum(sw[c], axis=0, keepdims=True) + sc[c] * qn[c] for c in cs_]
        for c, (bi, h) in enumerate(ch):
            inv = 1.0 / jnp.maximum(jnp.abs(den[c]), jnp.exp(-mt[c]))
            ys[bi][h] = ((num[c] + sc[c] * qc[c]) * inv).T
        bl = [b_all[bi][er:er + 1, lf[c]:lf[c] + 1] for c, (bi, h) in enumerate(ch)]
        tail = [bl[c] - brow[c] + igrow[c] for c in cs_]
        m_new = [jnp.maximum(bl[c] + m_prev[c], jnp.max(tail[c], axis=1, keepdims=True)) for c in cs_]
        ws = [jnp.exp(tail[c] - m_new[c]) for c in cs_]
        sc_end = [jnp.exp(bl[c] + m_prev[c] - m_new[c]) for c in cs_]
        upd = [_dot((v_t[c] * ws[c]).astype(BF16), kb[c]) for c in cs_]
        nup = [_dot(jnp.broadcast_to(ws[c], (8, t)).astype(BF16), kb[c])[0:1] for c in cs_]
        for c, (bi, h) in enumerate(ch):
            cs_ref[bi, h] = sc_end[c] * cs[c] + upd[c]
            sm_ref[bi, h:h + 1, :] = sc_end[c] * ns[c] + nup[c]
            sm_ref[bi, ML_HEADS + h:ML_HEADS + h + 1, :] = jnp.broadcast_to(m_new[c], (1, 128))
    for bi in bs:
        if finish:
            yp = yp_ref[bi]
            outs = [_rms(ys[bi][h] + yp[:, 128 * h:128 * h + 128]) for h in range(ML_HEADS)]
            y = jnp.concatenate(outs, axis=1) * ng_ref[...]
            y_ref[bi] = (_sigmoid(o_ref[bi]) * y).astype(y_ref.dtype)
        else:
            y_ref[bi] = jnp.concatenate(ys[bi], axis=1)


def _lru_kernel(rev, finish, nchunks, nb, *refs):
    if finish:
        (xf_ref, gate_ref, yp_ref, h0_ref, cw_ref, cb_ref, w_ref, bias_ref, lam_ref,
         y_ref, h_ref) = refs
    else:
        (x_ref, xp_ref, xn_ref, h0_ref, cw_ref, cb_ref, w_ref, bias_ref, lam_ref,
         y_ref, xf_ref, h_ref) = refs
    t = LRU_CHUNK
    d = 1 if rev else 0
    j = pl.program_id(0)
    c = (nchunks - 1 - j) if rev else j

    @pl.when(j == 0)
    def _():
        h_ref[...] = h0_ref[...]

    cw, cbias = cw_ref[...], cb_ref[...]
    lsl = LRU_C * _log_sigmoid(lam_ref[d])
    sub = lax.broadcasted_iota(jnp.int32, (8, GROUP_W), 0)
    n = t // 8

    for bi in range(nb):
        if finish:
            xf = xf_ref[bi]
        else:
            xf = _conv5(xp_ref[bi], x_ref[bi], xn_ref[bi], cw, cbias, c == 0, c == nchunks - 1)
            xf_ref[bi] = xf
        pre = _dot(xf.astype(BF16), w_ref[d]) + bias_ref[d]
        r = _sigmoid(pre[:, :GROUP_W])
        ig = _sigmoid(pre[:, GROUP_W:])
        loga = r * lsl
        a = jnp.exp(loga)
        bx = jnp.sqrt(-jnp.tanh(loga) * (1.0 + a * a)) * ig * xf

        a_loc, h_loc = [], []
        for i in range(n):
            ai = a[8 * i:8 * i + 8]
            hi = bx[8 * i:8 * i + 8]
            for sh in (1, 2, 4):
                if rev:
                    valid = sub < 8 - sh
                    a_s = pltpu.roll(ai, 8 - sh, axis=0)
                    h_s = pltpu.roll(hi, 8 - sh, axis=0)
                else:
                    valid = sub >= sh
                    a_s = pltpu.roll(ai, sh, axis=0)
                    h_s = pltpu.roll(hi, sh, axis=0)
                hi = jnp.where(valid, ai * h_s + hi, hi)
                ai = jnp.where(valid, ai * a_s, ai)
            a_loc.append(ai)
            h_loc.append(hi)
        carry = h_ref[bi]
        outs = [None] * n
        for i in (range(n - 1, -1, -1) if rev else range(n)):
            hi = h_loc[i] + a_loc[i] * carry
            outs[i] = hi
            carry = hi[0:1] if rev else hi[7:8]
        h_ref[bi] = carry
        hs = jnp.concatenate(outs, axis=0)
        if finish:
            y_ref[bi] = ((hs + yp_ref[bi]) * _gelu_tanh(gate_ref[bi])).astype(y_ref.dtype)
        else:
            y_ref[bi] = hs


def _gla_kernel(rev, finish, nchunks, nb, *refs):
    if finish:
        (qk_ref, v_ref, g1_ref, r_ref, yp_ref, s0_ref, wg_ref, bg_ref, ng_ref, y_ref, s_ref) = refs
    else:
        (qk_ref, v_ref, g1_ref, s0_ref, wg_ref, bg_ref, ng_ref, y_ref, s_ref) = refs
    t = GLA_CHUNK
    d = 1 if rev else 0
    dkk = GLA_HEADS * GLA_DK
    j = pl.program_id(0)

    @pl.when(j == 0)
    def _():
        s_ref[...] = s0_ref[...]

    mask = _scan_mask(t, rev)
    maskf = mask.astype(F32)
    rows = lax.broadcasted_iota(jnp.int32, (t, 1), 0)
    lane_head = lax.broadcasted_iota(jnp.int32, (GLA_SUB, dkk), 1) // GLA_DK
    blk = (lax.broadcasted_iota(jnp.int32, (GROUP_W, dkk), 0) // 128
           == lax.broadcasted_iota(jnp.int32, (GROUP_W, dkk), 1) // GLA_DK)
    er = 0 if rev else t - 1

    bs = range(nb)
    q = [qk_ref[bi][:, :dkk] * (GLA_DK ** -0.5) for bi in bs]
    k = [qk_ref[bi][:, dkk:] for bi in bs]
    v = [v_ref[bi] for bi in bs]
    vb = [v[bi].astype(BF16) for bi in bs]
    glog = [_dot(g1_ref[bi].astype(BF16), wg_ref[d]) + bg_ref[d] for bi in bs]
    la = [_log_sigmoid(glog[bi]) * (1.0 / GLA_TAU) for bi in bs]
    b = [_dot_exact(maskf, la[bi]) for bi in bs]
    excl = [b[bi] - la[bi] for bi in bs]
    s_t = [s_ref[bi] for bi in bs]
    o_inter = [_dot_nt((q[bi] * jnp.exp(b[bi])).astype(BF16), s_t[bi].astype(BF16)) for bi in bs]

    att_blocks = [[None] * (t // GLA_SUB) for _ in bs]
    for i in range(t // GLA_SUB):
        r0 = GLA_SUB * i
        if rev:
            ref_row = [excl[bi][r0 + GLA_SUB - 1:r0 + GLA_SUB] for bi in bs]
            kvalid = rows >= r0
        else:
            ref_row = [excl[bi][r0:r0 + 1] for bi in bs]
            kvalid = rows < r0 + GLA_SUB
        qi = [q[bi][r0:r0 + GLA_SUB] * jnp.exp(b[bi][r0:r0 + GLA_SUB] - ref_row[bi]) for bi in bs]
        ki = [(k[bi] * jnp.exp(jnp.where(kvalid, ref_row[bi] - b[bi], NEG))).astype(BF16) for bi in bs]
        qs = [jnp.concatenate([jnp.where(lane_head == h, qi[bi], 0.0) for h in range(GLA_HEADS)],
                              axis=0).astype(BF16) for bi in bs]
        for bi in bs:
            att_blocks[bi][i] = _dot_nt(qs[bi], ki[bi])
    outs = [[None] * GLA_HEADS for _ in bs]
    for h in range(GLA_HEADS):
        att = [jnp.where(mask, jnp.concatenate([ab[GLA_SUB * h:GLA_SUB * h + GLA_SUB] for ab in att_blocks[bi]],
                                               axis=0), 0.0).astype(BF16) for bi in bs]
        for bi in bs:
            outs[bi][h] = _dot(att[bi], vb[bi][:, 128 * h:128 * h + 128])

    bl = [b[bi][er:er + 1] for bi in bs]
    kd = [(k[bi] * jnp.exp(bl[bi] - b[bi])).astype(BF16) for bi in bs]
    v_t = [jnp.concatenate([v[bi][:, 128 * h:128 * h + 128].T for h in range(GLA_HEADS)], axis=0).astype(BF16)
           for bi in bs]
    upd = [_dot(v_t[bi], kd[bi]) for bi in bs]
    for bi in bs:
        s_ref[bi] = s_t[bi] * jnp.exp(bl[bi]) + jnp.where(blk, upd[bi], 0.0)
    for bi in bs:
        o = jnp.concatenate(outs[bi], axis=1) + o_inter[bi]
        if finish:
            o = o + yp_ref[bi]
            o = jnp.concatenate([_rms(o[:, 128 * h:128 * h + 128]) for h in range(GLA_HEADS)], axis=1)
            y_ref[bi] = (o * ng_ref[...] * _silu(r_ref[bi])).astype(y_ref.dtype)
        else:
            y_ref[bi] = o


def _chunk_specs(nb, width, off, t, length, rev, nchunks, halo):
    ob = off // width
    tb = t // HALO

    def cj(j):
        return (nchunks - 1 - j) if rev else j

    main = pl.BlockSpec((nb, t, width), lambda j: (0, cj(j), ob))
    if not halo:
        return [main]
    prev = pl.BlockSpec((nb, HALO, width), lambda j: (0, jnp.maximum(cj(j) * tb - 1, 0), ob))
    nxt = pl.BlockSpec((nb, HALO, width), lambda j: (0, jnp.minimum((cj(j) + 1) * tb, length // HALO - 1), ob))
    return [main, prev, nxt]


def _run_mixer(body, pieces, t, rev, finish, ypart, states, params, name, save_width=0):
    nb, length = pieces[0][0].shape[0], pieces[0][0].shape[1]
    nchunks = length // t
    cj = (lambda j: nchunks - 1 - j) if rev else (lambda j: j)
    row_spec = lambda width: pl.BlockSpec((nb, t, width), lambda j: (0, cj(j), 0))
    in_specs, args = [], []
    for arr, width, off, halo in pieces:
        sp = _chunk_specs(nb, width, off, t, length, rev, nchunks, halo)
        in_specs += sp
        args += [arr] * len(sp)
    if finish:
        in_specs.append(row_spec(GROUP_W))
        args.append(ypart)
    for a in list(states) + list(params):
        in_specs.append(pl.BlockSpec(a.shape, lambda j, _nd=a.ndim: (0,) * _nd))
        args.append(a)
    out_shape = [jax.ShapeDtypeStruct((nb, length, GROUP_W), BF16 if finish else F32)]
    out_specs = [row_spec(GROUP_W)]
    if save_width:
        out_shape.append(jax.ShapeDtypeStruct((nb, length, save_width), F32))
        out_specs.append(row_spec(save_width))
    for s in states:
        out_shape.append(jax.ShapeDtypeStruct(s.shape, s.dtype))
        out_specs.append(pl.BlockSpec(s.shape, lambda j, _nd=s.ndim: (0,) * _nd))
    res = pl.pallas_call(
        functools.partial(body, rev, finish, nchunks, nb),
        grid=(nchunks,),
        in_specs=in_specs,
        out_specs=out_specs,
        out_shape=out_shape,
        compiler_params=_cparams(("arbitrary",)),
        name=name,
    )(*args)
    nfix = 2 if save_width else 1
    return res[0], (res[1] if save_width else None), list(res[nfix:])


def _bidir_mixer(body, bwd_pieces, fwd_pieces, save_width, t, u_ctx, u_lat, zero_states, params, name):
    def run(u, rev, ypart, saved, states, tag):
        src = {"u": u, "saved": saved}
        pieces = [(src[s], w, o, h) for s, w, o, h in (bwd_pieces if rev else fwd_pieces)]
        return _run_mixer(body, pieces, t, rev, not rev, ypart, states, params, name + tag,
                          save_width if rev else 0)

    yb_c, sv_c, st_b = run(u_ctx, True, None, None, zero_states, "_ctx_bwd")
    y_c, _, st_f = run(u_ctx, False, yb_c, sv_c, zero_states, "_ctx_fwd")
    yb_l, sv_l, _ = run(u_lat, True, None, None, st_b, "_lat_bwd")
    y_l, _, _ = run(u_lat, False, yb_l, sv_l, st_f, "_lat_fwd")
    return y_c, y_l


def _mods(cvec, w_mod, b_mod):
    depth, dm, nm = w_mod.shape
    tn = 1024
    return pl.pallas_call(
        _mod_kernel,
        grid=(depth, nm // tn),
        in_specs=[pl.BlockSpec((8, dm), lambda l, j: (0, 0)),
                  pl.BlockSpec((None, dm, tn), lambda l, j: (l, 0, j)),
                  pl.BlockSpec((None, 1, tn), lambda l, j: (l, 0, j))],
        out_specs=pl.BlockSpec((None, 8, tn), lambda l, j: (l, 0, j)),
        out_shape=jax.ShapeDtypeStruct((depth, 8, nm), F32),
        compiler_params=_cparams(("parallel", "parallel")),
        name="mods",
    )(cvec, w_mod, b_mod.reshape(depth, 1, nm))


def _inproj(x, g, shift, scale, w, rows_per_group):
    m, dm = x.shape
    n = w.shape[1]
    tm = TM_PROJ
    grp = lambda i: (i * tm // rows_per_group, 0, 0)
    return pl.pallas_call(
        _inproj_kernel,
        grid=(m // tm,),
        in_specs=[pl.BlockSpec((tm, dm), lambda i: (i, 0)),
                  _resident((1, dm)),
                  pl.BlockSpec((1, 1, dm), grp),
                  pl.BlockSpec((1, 1, dm), grp),
                  _resident((dm, n))],
        out_specs=pl.BlockSpec((tm, n), lambda i: (i, 0)),
        out_shape=jax.ShapeDtypeStruct((m, n), F32),
        compiler_params=_cparams(("parallel",)),
        name="inproj",
    )(x, g, shift, scale, w)


def _outproj(ys, w, x, gate, g, shift, scale, rows_per_group):
    m, dm = x.shape
    tm = min(TM_OUT, rows_per_group)
    grp = lambda i: (i * tm // rows_per_group, 0, 0)
    yspec = pl.BlockSpec((tm, GROUP_W), lambda i: (i, 0))
    return pl.pallas_call(
        _outproj_kernel,
        grid=(m // tm,),
        in_specs=[yspec, yspec, yspec, yspec,
                  _resident((4, GROUP_W, dm)),
                  pl.BlockSpec((tm, dm), lambda i: (i, 0)),
                  pl.BlockSpec((1, 1, dm), grp),
                  _resident((1, dm)),
                  pl.BlockSpec((1, 1, dm), grp),
                  pl.BlockSpec((1, 1, dm), grp)],
        out_specs=[pl.BlockSpec((tm, dm), lambda i: (i, 0)), pl.BlockSpec((tm, dm), lambda i: (i, 0))],
        out_shape=[jax.ShapeDtypeStruct((m, dm), F32), jax.ShapeDtypeStruct((m, dm), BF16)],
        compiler_params=_cparams(("parallel",)),
        name="outproj",
    )(*ys, w, x, gate, g, shift, scale)


def _ffn_up(h, wg, wu):
    m, dm = h.shape
    dff = wg.shape[1]
    tm = min(TM_FFN_UP, m)
    tn = TN_FFN_UP
    return pl.pallas_call(
        _ffn_up_kernel,
        grid=(m // tm, dff // tn),
        in_specs=[pl.BlockSpec((tm, dm), lambda i, j: (i, 0)),
                  pl.BlockSpec((dm, tn), lambda i, j: (0, j)),
                  pl.BlockSpec((dm, tn), lambda i, j: (0, j))],
        out_specs=pl.BlockSpec((tm, tn), lambda i, j: (i, j)),
        out_shape=jax.ShapeDtypeStruct((m, dff), BF16),
        compiler_params=_cparams(("parallel", "arbitrary")),
        name="ffn_up",
    )(h, wg, wu)


def _ffn_down(a, w, x, gate, final_g, final, rows_per_group):
    m, dm = x.shape
    dff = a.shape[1]
    tm = TM_FFN_DOWN
    grp = lambda i: (i * tm // rows_per_group, 0, 0)
    return pl.pallas_call(
        functools.partial(_ffn_down_kernel, final),
        grid=(m // tm,),
        in_specs=[pl.BlockSpec((tm, dff), lambda i: (i, 0)),
                  _resident((dff, dm)),
                  pl.BlockSpec((tm, dm), lambda i: (i, 0)),
                  pl.BlockSpec((1, 1, dm), grp),
                  _resident((1, dm))],
        out_specs=pl.BlockSpec((tm, dm), lambda i: (i, 0)),
        out_shape=jax.ShapeDtypeStruct((m, dm), F32),
        compiler_params=_cparams(("parallel",)),
        name="ffn_down",
    )(a, w, x, gate, final_g)


def _pad_cols(a, width):
    return jnp.pad(a, ((0, 0), (0, width - a.shape[1])))


def _split_w_in(w):
    ssd, ml, lru, gla = jnp.split(w, [1552, 1552 + 2064, 1552 + 2064 + 1024], axis=1)
    ssd_z, ssd_xbc, ssd_dt = ssd[:, :512], ssd[:, 512:1536], ssd[:, 1536:]
    ml_qkvo, ml_g = ml[:, :2048], ml[:, 2048:]
    lru_gate, lru_x = lru[:, :512], lru[:, 512:]
    gla_qkvr, gla_g1 = gla[:, :1536], gla[:, 1536:]
    w_row = jnp.concatenate([ssd_xbc, lru_x, lru_gate, ssd_z, _pad_cols(ssd_dt, 128)], axis=1)
    w_col = jnp.concatenate([ml_qkvo, gla_qkvr, _pad_cols(ml_g, 128), _pad_cols(gla_g1, 128)], axis=1)
    return w_row.astype(BF16), w_col.astype(BF16)


def _row128(a):
    return _pad_cols(a.reshape(1, -1).astype(F32), 128)


def _block_diag(w):
    nb, bi, bj = w.shape
    eye = jnp.eye(nb, dtype=w.dtype)
    return (eye[:, None, :, None] * w[:, :, None, :]).reshape(nb * bi, nb * bj)


def _to_colmajor(a, bsz, rows):
    n = a.shape[-1]
    return a.reshape(bsz, rows, GRID_W, n).transpose(0, 2, 1, 3).reshape(bsz, rows * GRID_W, n)


def _from_colmajor(a, bsz, rows):
    n = a.shape[-1]
    return a.reshape(bsz, GRID_W, rows, n).transpose(0, 2, 1, 3).reshape(bsz, rows * GRID_W, n)


def kernel(x, c, ctx, c_ctx, norm1_g, norm2_g, w_mod, b_mod, w_in, w_out, ssd_conv_w, ssd_conv_b, ssd_dt_bias, ssd_a_log, ssd_d, ssd_norm_g, ml_conv_w, ml_conv_b, ml_igate_b, ml_fgate_b, ml_norm_g, lru_conv_w, lru_conv_b, lru_wa, lru_ba, lru_wx, lru_bx, lru_lambda, gla_wg2, gla_bg, gla_norm_g, w_gate, w_up, w_down, final_g):
    bsz, length, dm = x.shape
    lctx = ctx.shape[1]
    depth = w_in.shape[0]
    rows = length // GRID_W
    assert dm == D_MODEL and length % SSD_CHUNK == 0 and lctx % SSD_CHUNK == 0

    cvec = jnp.concatenate([c, c_ctx[None, :], jnp.zeros((8 - bsz - 1, dm), F32)], axis=0)
    mods = _mods(cvec, w_mod, b_mod)

    xl = x.reshape(bsz * length, dm)
    xt = ctx.reshape(bsz * lctx, dm)
    row = lambda a: a.reshape(1, -1).astype(F32)

    for i in range(depth):
        need_ctx = i < depth - 1
        m_l = [mods[i, :bsz, k * dm:(k + 1) * dm].reshape(bsz, 1, dm) for k in range(N_MOD)]
        m_t = [mods[i, bsz:bsz + 1, k * dm:(k + 1) * dm].reshape(1, 1, dm) for k in range(N_MOD)]
        w_row, w_col = _split_w_in(w_in[i])
        g1 = row(norm1_g[i])
        xl_cm = _to_colmajor(xl.reshape(bsz, length, dm), bsz, rows).reshape(bsz * length, dm)
        ul_row = _inproj(xl, g1, m_l[0], m_l[1], w_row, length).reshape(bsz, length, N_ROW)
        ul_col = _inproj(xl_cm, g1, m_l[0], m_l[1], w_col, length).reshape(bsz, length, N_COL)
        ut_row = _inproj(xt, g1, m_t[0], m_t[1], w_row, bsz * lctx).reshape(bsz, lctx, N_ROW)
        ut_col = _inproj(xt, g1, m_t[0], m_t[1], w_col, bsz * lctx).reshape(bsz, lctx, N_COL)

        ssd_params = [ssd_conv_w[i], row(ssd_conv_b[i]), _row128(ssd_dt_bias[i]), _row128(ssd_a_log[i]),
                      row(jnp.repeat(ssd_d[i], GROUP_W // SSD_HEADS)), row(ssd_norm_g[i])]
        ssd_bwd = [("u", 1024, OFF_SSD_XBC, True), ("u", 128, OFF_SSD_DT, False)]
        ssd_fwd = [("saved", 1024, 0, False), ("u", 128, OFF_SSD_DT, False), ("u", 512, OFF_SSD_Z, False)]
        ssd_zero = [jnp.zeros((bsz, SSD_HEADS // 2, 128, 128), F32)]
        ya_t, ya_l = _bidir_mixer(_ssd_kernel, ssd_bwd, ssd_fwd, 1024, SSD_CHUNK, ut_row, ul_row, ssd_zero,
                                  ssd_params, "ssd")

        ml_gb = _row128(jnp.concatenate([ml_igate_b[i].reshape(-1), ml_fgate_b[i].reshape(-1)]))
        ml_params = [ml_conv_w[i], row(ml_conv_b[i]), ml_gb, row(ml_norm_g[i])]
        ml_bwd = [("u", 512, OFF_ML_Q, True), ("u", 512, OFF_ML_K, True), ("u", 512, OFF_ML_V, False),
                  ("u", 128, OFF_ML_G, False)]
        ml_fwd = [("saved", 1024, 0, False), ("u", 512, OFF_ML_V, False), ("u", 128, OFF_ML_G, False),
                  ("u", 512, OFF_ML_O, False)]
        ml_zero = [jnp.zeros((bsz, ML_HEADS, 128, 128), F32), jnp.zeros((bsz, 2 * ML_HEADS, 128), F32)]
        yb_t, yb_l = _bidir_mixer(_mlstm_kernel, ml_bwd, ml_fwd, 1024, ML_CHUNK, ut_col, ul_col, ml_zero,
                                  ml_params, "mlstm")

        lru_w = jnp.stack([jnp.concatenate([_block_diag(lru_wa[i, dd]), _block_diag(lru_wx[i, dd])], axis=1)
                           for dd in range(2)]).astype(BF16)
        lru_bias = jnp.concatenate([lru_ba[i], lru_bx[i]], axis=1).reshape(2, 1, 2 * GROUP_W)
        lru_params = [lru_conv_w[i], row(lru_conv_b[i]), lru_w, lru_bias, lru_lambda[i].reshape(2, 1, GROUP_W)]
        lru_bwd = [("u", 512, OFF_LRU_X, True)]
        lru_fwd = [("saved", 512, 0, False), ("u", 512, OFF_LRU_GATE, False)]
        lru_zero = [jnp.zeros((bsz, 1, GROUP_W), F32)]
        yc_t, yc_l = _bidir_mixer(_lru_kernel, lru_bwd, lru_fwd, 512, LRU_CHUNK, ut_row, ul_row, lru_zero,
                                  lru_params, "lru")

        wg = jnp.zeros((2, 128, GLA_HEADS * GLA_DK), F32)
        wg = wg.at[0, :GLA_RANK].set(gla_wg2[i, 0]).at[1, GLA_RANK:2 * GLA_RANK].set(gla_wg2[i, 1])
        gla_params = [wg.astype(BF16), gla_bg[i].reshape(2, 1, GLA_HEADS * GLA_DK), row(gla_norm_g[i])]
        gla_bwd = [("u", 512, OFF_GLA_QK, False), ("u", 512, OFF_GLA_V, False), ("u", 128, OFF_GLA_G1, False)]
        gla_fwd = gla_bwd + [("u", 512, OFF_GLA_R, False)]
        gla_zero = [jnp.zeros((bsz, GROUP_W, GLA_HEADS * GLA_DK), F32)]
        yd_t, yd_l = _bidir_mixer(_gla_kernel, gla_bwd, gla_fwd, 0, GLA_CHUNK, ut_col, ul_col, gla_zero,
                                  gla_params, "gla")

        yb_l = _from_colmajor(yb_l, bsz, rows)
        yd_l = _from_colmajor(yd_l, bsz, rows)

        w_out_b = w_out[i].astype(BF16).reshape(4, GROUP_W, dm)
        wg_b, wu_b, wd_b = w_gate[i].astype(BF16), w_up[i].astype(BF16), w_down[i].astype(BF16)
        fg = row(final_g)

        def tail(xs, ys, m, rows_per_group, final):
            ys = [y.reshape(-1, GROUP_W) for y in ys]
            xs, h2 = _outproj(ys, w_out_b, xs, m[2], row(norm2_g[i]), m[3], m[4], rows_per_group)
            act = _ffn_up(h2, wg_b, wu_b)
            return _ffn_down(act, wd_b, xs, m[5], fg, final, rows_per_group)

        xl = tail(xl, [ya_l, yb_l, yc_l, yd_l], m_l, length, i == depth - 1)
        if need_ctx:
            xt = tail(xt, [ya_t, yb_t, yc_t, yd_t], m_t, bsz * lctx, False)
    return xl.reshape(bsz, length, dm)
```

```python
import functools

import jax
import jax.numpy as jnp
from jax import lax
from jax.experimental import pallas as pl
from jax.experimental.pallas import tpu as pltpu

F32 = jnp.float32
BF16 = jnp.bfloat16
EPS = 1e-6
NEG = -1e30

D_MODEL = 2048
GRID_W = 64
GROUP_W = 512
N_MOD = 6
CONV_W = 5
HALO = 8

SSD_HEADS = 8
SSD_CHUNK = 128
ML_HEADS = 4
ML_CHUNK = 64
ML_HEAD_GROUP = 4
LRU_C = 8.0
LRU_CHUNK = 128
GLA_HEADS = 4
GLA_DK = 64
GLA_RANK = 16
GLA_TAU = 16.0
GLA_CHUNK = 64
GLA_SUB = 16

OFF_SSD_XBC = 0
OFF_LRU_X = 1024
OFF_LRU_GATE = 1536
OFF_SSD_Z = 2048
OFF_SSD_DT = 2560
N_ROW = 2688
OFF_ML_Q = 0
OFF_ML_K = 512
OFF_ML_V = 1024
OFF_ML_O = 1536
OFF_GLA_QK = 2048
OFF_GLA_V = 2560
OFF_GLA_R = 3072
OFF_ML_G = 3584
OFF_GLA_G1 = 3712
N_COL = 3840

VMEM_LIMIT = 56 * 1024 * 1024
TM_PROJ = 256
TM_OUT = 512
TM_FFN_UP = 2048
TN_FFN_UP = 512
TM_FFN_DOWN = 256


def _cparams(sem):
    return pltpu.CompilerParams(dimension_semantics=sem, vmem_limit_bytes=VMEM_LIMIT)


def _resident(shape, layer=None):
    if layer is None:
        nd = len(shape)
        return pl.BlockSpec(shape, lambda *_: (0,) * nd, pipeline_mode=pl.Buffered(1))
    nd = len(shape)
    return pl.BlockSpec((None,) + tuple(shape), lambda *_: (layer,) + (0,) * nd, pipeline_mode=pl.Buffered(1))


def _sigmoid(x):
    return 0.5 + 0.5 * jnp.tanh(0.5 * x)


def _silu(x):
    return x * _sigmoid(x)


def _softplus(x):
    return jnp.maximum(x, 0.0) + jnp.log(1.0 + jnp.exp(-jnp.abs(x)))


def _log_sigmoid(x):
    return jnp.minimum(x, 0.0) - jnp.log(1.0 + jnp.exp(-jnp.abs(x)))


def _gelu_tanh(x):
    return 0.5 * x * (1.0 + jnp.tanh(0.7978845608028654 * (x + 0.044715 * (x * x * x))))


def _rms(x):
    return x * lax.rsqrt(jnp.mean(x * x, axis=-1, keepdims=True) + EPS)


def _dot(a, b):
    return jnp.dot(a, b, preferred_element_type=F32)


def _dot_nt(a, b):
    return lax.dot_general(a, b, (((1,), (1,)), ((), ())), preferred_element_type=F32)


def _dot_exact(a, b):
    return jnp.dot(a, b, precision=lax.Precision.HIGHEST, preferred_element_type=F32)


def _scan_mask(t, rev):
    ri = lax.broadcasted_iota(jnp.int32, (t, t), 0)
    ci = lax.broadcasted_iota(jnp.int32, (t, t), 1)
    return (ci >= ri) if rev else (ci <= ri)


def _conv5(prev, main, nxt, w, bias, first, last):
    t = main.shape[0]
    prev = jnp.where(first, 0.0, prev)
    nxt = jnp.where(last, 0.0, nxt)
    ext = jnp.concatenate([prev, main, nxt], axis=0)
    n = t + 2 * HALO
    acc = None
    for k in range(CONV_W):
        sh = (CONV_W // 2 - k) % n
        r = ext if sh == 0 else pltpu.roll(ext, sh, axis=0)
        term = r[HALO:HALO + t] * w[k:k + 1, :]
        acc = term if acc is None else acc + term
    return acc + bias


def _mod_kernel(c_ref, w_ref, b_ref, o_ref):
    s = _silu(c_ref[...]).astype(BF16)
    o_ref[...] = _dot(s, w_ref[...].astype(BF16)) + b_ref[...]


def _inproj_kernel(x_ref, g_ref, sh_ref, sc_ref, w_ref, o_ref):
    y = _rms(x_ref[...]) * g_ref[...]
    h = (y * (1.0 + sc_ref[0]) + sh_ref[0]).astype(BF16)
    o_ref[...] = _dot(h, w_ref[...])


def _outproj_kernel(ya_ref, yb_ref, yc_ref, yd_ref, w_ref, x_ref, gate_ref, g_ref, sh_ref, sc_ref,
                    xo_ref, h_ref):
    acc = _dot(ya_ref[...], w_ref[0])
    acc = acc + _dot(yb_ref[...], w_ref[1])
    acc = acc + _dot(yc_ref[...], w_ref[2])
    acc = acc + _dot(yd_ref[...], w_ref[3])
    xn = x_ref[...] + gate_ref[0] * acc
    xo_ref[...] = xn
    y = _rms(xn) * g_ref[...]
    h_ref[...] = (y * (1.0 + sc_ref[0]) + sh_ref[0]).astype(BF16)


def _ffn_up_kernel(h_ref, wg_ref, wu_ref, o_ref):
    h = h_ref[...]
    g = _dot(h, wg_ref[...].astype(BF16))
    u = _dot(h, wu_ref[...].astype(BF16))
    o_ref[...] = (_silu(g) * u).astype(BF16)


def _ffn_down_kernel(final, a_ref, w_ref, x_ref, gate_ref, fg_ref, o_ref):
    xn = x_ref[...] + gate_ref[0] * _dot(a_ref[...], w_ref[...])
    if final:
        xn = _rms(xn) * fg_ref[...]
    o_ref[...] = xn


def _ssd_kernel(rev, finish, nchunks, nb, *refs):
    if finish:
        (xc_ref, dt_ref, z_ref, yp_ref, s0_ref, cw_ref, cb_ref, dtb_ref, alog_ref,
         dsk_ref, ng_ref, y_ref, s_ref) = refs
    else:
        (xbc_ref, xp_ref, xn_ref, dt_ref, s0_ref, cw_ref, cb_ref, dtb_ref, alog_ref,
         dsk_ref, ng_ref, y_ref, xc_ref, s_ref) = refs
    t = SSD_CHUNK
    d = 1 if rev else 0
    j = pl.program_id(0)
    c = (nchunks - 1 - j) if rev else j

    @pl.when(j == 0)
    def _():
        s_ref[...] = s0_ref[...]

    mask = _scan_mask(t, rev)
    maskf = mask.astype(F32)
    er = 0 if rev else t - 1
    lo = lax.broadcasted_iota(jnp.int32, (t, 128), 1) < 64
    lo_rows = lax.broadcasted_iota(jnp.int32, (128, 128), 0) < 64
    cw, cbias = cw_ref[...], cb_ref[...]
    nega = -jnp.exp(alog_ref[...])

    bs = range(nb)
    if finish:
        xbc = [xc_ref[bi] for bi in bs]
    else:
        xbc = [_silu(_conv5(xp_ref[bi], xbc_ref[bi], xn_ref[bi], cw, cbias, c == 0, c == nchunks - 1))
               for bi in bs]
        for bi in bs:
            xc_ref[bi] = xbc[bi]
    dt_all = [_softplus(dt_ref[bi] + dtb_ref[...]) for bi in bs]
    b_all = [_dot_exact(maskf, dt_all[bi] * nega) for bi in bs]
    b_t = [b_all[bi].T for bi in bs]
    dt_t = [dt_all[bi].T for bi in bs]
    ys = [[None] * (SSD_HEADS // 2) for _ in bs]
    for p in range(SSD_HEADS // 2):
        g = p // 2
        bm = [xbc[bi][:, 512 + 128 * g:640 + 128 * g].astype(BF16) for bi in bs]
        cm = [xbc[bi][:, 768 + 128 * g:896 + 128 * g].astype(BF16) for bi in bs]
        cb = [_dot_nt(cm[bi], bm[bi]) for bi in bs]
        xpair = [xbc[bi][:, 128 * p:128 * p + 128] for bi in bs]
        xpair_b = [xpair[bi].astype(BF16) for bi in bs]
        s_pair = [s_ref[bi, p] for bi in bs]
        y_inter = [_dot_nt(cm[bi], s_pair[bi].astype(BF16)) for bi in bs]
        yh, eb, coef, dec = [], [], [], []
        for hh in range(2):
            l = d * SSD_HEADS + 2 * p + hh
            bcol = [b_all[bi][:, l:l + 1] for bi in bs]
            seg = [jnp.exp(jnp.where(mask, bcol[bi] - b_t[bi][l:l + 1, :], NEG)) for bi in bs]
            w = [(cb[bi] * seg[bi] * dt_t[bi][l:l + 1, :]).astype(BF16) for bi in bs]
            yh.append([_dot(w[bi], xpair_b[bi]) for bi in bs])
            eb.append([jnp.exp(bcol[bi]) for bi in bs])
            bl = [b_all[bi][er:er + 1, l:l + 1] for bi in bs]
            coef.append([jnp.exp(bl[bi] - bcol[bi]) * dt_all[bi][:, l:l + 1] for bi in bs])
            dec.append([jnp.exp(bl[bi]) for bi in bs])
        xw = [(xpair[bi] * jnp.where(lo, coef[0][bi], coef[1][bi])).T.astype(BF16) for bi in bs]
        upd = [_dot(xw[bi], bm[bi]) for bi in bs]
        for bi in bs:
            ys[bi][p] = (jnp.where(lo, yh[0][bi], yh[1][bi])
                         + y_inter[bi] * jnp.where(lo, eb[0][bi], eb[1][bi]))
            s_ref[bi, p] = jnp.where(lo_rows, dec[0][bi], dec[1][bi]) * s_pair[bi] + upd[bi]
    for bi in bs:
        y = jnp.concatenate(ys[bi], axis=1)
        if finish:
            y = y + yp_ref[bi] + dsk_ref[...] * xbc[bi][:, :GROUP_W]
            y = _rms(y * _silu(z_ref[bi])) * ng_ref[...]
            y_ref[bi] = y.astype(y_ref.dtype)
        else:
            y_ref[bi] = y


def _mlstm_kernel(rev, finish, nchunks, nb, *refs):
    if finish:
        (qkc_ref, v_ref, g_ref, o_ref, yp_ref, cs0_ref, sm0_ref,
         cw_ref, cb_ref, gb_ref, ng_ref, y_ref, cs_ref, sm_ref) = refs
    else:
        (q_ref, qp_ref, qn_ref, k_ref, kp_ref, kn_ref, v_ref, g_ref, cs0_ref, sm0_ref,
         cw_ref, cb_ref, gb_ref, ng_ref, y_ref, qkc_ref, cs_ref, sm_ref) = refs
    t = ML_CHUNK
    d = 1 if rev else 0
    j = pl.program_id(0)
    c = (nchunks - 1 - j) if rev else j
    first, last = c == 0, c == nchunks - 1

    @pl.when(j == 0)
    def _():
        cs_ref[...] = cs0_ref[...]
        sm_ref[...] = sm0_ref[...]

    cw = cw_ref[...]
    cbias = cb_ref[...]
    maskf = _scan_mask(t, rev).astype(F32)
    mask_t = _scan_mask(t, not rev)
    er = 0 if rev else t - 1

    bs = range(nb)
    if finish:
        q = [qkc_ref[bi][:, :GROUP_W] for bi in bs]
        k = [qkc_ref[bi][:, GROUP_W:] for bi in bs]
    else:
        q = [_silu(_conv5(qp_ref[bi], q_ref[bi], qn_ref[bi], cw[:, :GROUP_W], cbias[:, :GROUP_W], first, last))
             for bi in bs]
        k = [_silu(_conv5(kp_ref[bi], k_ref[bi], kn_ref[bi], cw[:, GROUP_W:], cbias[:, GROUP_W:], first, last))
             * (128.0 ** -0.5) for bi in bs]
        for bi in bs:
            qkc_ref[bi, :, :GROUP_W] = q[bi]
            qkc_ref[bi, :, GROUP_W:] = k[bi]
    v = [v_ref[bi] for bi in bs]
    gts = [g_ref[bi] + gb_ref[...] for bi in bs]
    b_all = [_dot_exact(maskf, _log_sigmoid(gts[bi])) for bi in bs]
    g_t = [gts[bi].T for bi in bs]
    b_t = [b_all[bi].T for bi in bs]
    ys = [[None] * ML_HEADS for _ in bs]
    for h0 in range(0, ML_HEADS, ML_HEAD_GROUP):
        ch = [(bi, h) for h in range(h0, h0 + ML_HEAD_GROUP) for bi in bs]
        cs_ = range(len(ch))
        li = [d * ML_HEADS + h for _, h in ch]
        lf = [2 * ML_HEADS + l for l in li]
        sl = [slice(128 * h, 128 * h + 128) for _, h in ch]
        brow = [b_t[bi][lf[c]:lf[c] + 1, :] for c, (bi, h) in enumerate(ch)]
        igrow = [g_t[bi][li[c]:li[c] + 1, :] for c, (bi, h) in enumerate(ch)]
        ccol = [gts[bi][:, li[c]:li[c] + 1] - b_all[bi][:, lf[c]:lf[c] + 1]
                for c, (bi, h) in enumerate(ch)]
        m_prev = [sm_ref[bi, ML_HEADS + h:ML_HEADS + h + 1, 0:1] for bi, h in ch]
        ns = [sm_ref[bi, h:h + 1, :] for bi, h in ch]
        cs = [cs_ref[bi, h] for bi, h in ch]
        dmat = [jnp.where(mask_t, brow[c] + ccol[c], NEG) for c in cs_]
        inter = [brow[c] + m_prev[c] for c in cs_]
        mt = [jnp.maximum(inter[c], jnp.max(dmat[c], axis=0, keepdims=True)) for c in cs_]
        w = [jnp.exp(dmat[c] - mt[c]) for c in cs_]
        sc = [jnp.exp(inter[c] - mt[c]) for c in cs_]
        qb = [q[bi][:, sl[c]].astype(BF16) for c, (bi, h) in enumerate(ch)]
        kb = [k[bi][:, sl[c]].astype(BF16) for c, (bi, h) in enumerate(ch)]
        sw = [_dot_nt(kb[c], qb[c]) * w[c] for c in cs_]
        v_t = [v[bi][:, sl[c]].T for c, (bi, h) in enumerate(ch)]
        num = [_dot(v_t[c].astype(BF16), sw[c].astype(BF16)) for c in cs_]
        qc = [_dot_nt(cs[c].astype(BF16), qb[c]) for c in cs_]
        qn = [_dot_nt(jnp.broadcast_to(ns[c], (8, 128)).astype(BF16), qb[c])[0:1] for c in cs_]
        den = [jnp.sum(sw[c], axis=0, keepdims=True) + sc[c] * qn[c] for c in cs_]
        for c, (bi, h) in enumerate(ch):
            inv = 1.0 / jnp.maximum(jnp.abs(den[c]), jnp.exp(-mt[c]))
            ys[bi][h] = ((num[c] + sc[c] * qc[c]) * inv).T
        bl = [b_all[bi][er:er + 1, lf[c]:lf[c] + 1] for c, (bi, h) in enumerate(ch)]
        tail = [bl[c] - brow[c] + igrow[c] for c in cs_]
        m_new = [jnp.maximum(bl[c] + m_prev[c], jnp.max(tail[c], axis=1, keepdims=True)) for c in cs_]
        ws = [jnp.exp(tail[c] - m_new[c]) for c in cs_]
        sc_end = [jnp.exp(bl[c] + m_prev[c] - m_new[c]) for c in cs_]
        upd = [_dot((v_t[c] * ws[c]).astype(BF16), kb[c]) for c in cs_]
        nup = [_dot(jnp.broadcast_to(ws[c], (8, t)).astype(BF16), kb[c])[0:1] for c in cs_]
        for c, (bi, h) in enumerate(ch):
            cs_ref[bi, h] = sc_end[c] * cs[c] + upd[c]
            sm_ref[bi, h:h + 1, :] = sc_end[c] * ns[c] + nup[c]
            sm_ref[bi, ML_HEADS + h:ML_HEADS + h + 1, :] = jnp.broadcast_to(m_new[c], (1, 128))
    for bi in bs:
        if finish:
            yp = yp_ref[bi]
            outs = [_rms(ys[bi][h] + yp[:, 128 * h:128 * h + 128]) for h in range(ML_HEADS)]
            y = jnp.concatenate(outs, axis=1) * ng_ref[...]
            y_ref[bi] = (_sigmoid(o_ref[bi]) * y).astype(y_ref.dtype)
        else:
            y_ref[bi] = jnp.concatenate(ys[bi], axis=1)


def _lru_kernel(rev, finish, nchunks, nb, *refs):
    if finish:
        (xf_ref, gate_ref, yp_ref, h0_ref, cw_ref, cb_ref, w_ref, bias_ref, lam_ref,
         y_ref, h_ref) = refs
    else:
        (x_ref, xp_ref, xn_ref, h0_ref, cw_ref, cb_ref, w_ref, bias_ref, lam_ref,
         y_ref, xf_ref, h_ref) = refs
    t = LRU_CHUNK
    d = 1 if rev else 0
    j = pl.program_id(0)
    c = (nchunks - 1 - j) if rev else j

    @pl.when(j == 0)
    def _():
        h_ref[...] = h0_ref[...]

    cw, cbias = cw_ref[...], cb_ref[...]
    lsl = LRU_C * _log_sigmoid(lam_ref[d])
    sub = lax.broadcasted_iota(jnp.int32, (8, GROUP_W), 0)
    n = t // 8

    for bi in range(nb):
        if finish:
            xf = xf_ref[bi]
        else:
            xf = _conv5(xp_ref[bi], x_ref[bi], xn_ref[bi], cw, cbias, c == 0, c == nchunks - 1)
            xf_ref[bi] = xf
        pre = _dot(xf.astype(BF16), w_ref[d]) + bias_ref[d]
        r = _sigmoid(pre[:, :GROUP_W])
        ig = _sigmoid(pre[:, GROUP_W:])
        loga = r * lsl
        a = jnp.exp(loga)
        bx = jnp.sqrt(-jnp.tanh(loga) * (1.0 + a * a)) * ig * xf

        a_loc, h_loc = [], []
        for i in range(n):
            ai = a[8 * i:8 * i + 8]
            hi = bx[8 * i:8 * i + 8]
            for sh in (1, 2, 4):
                if rev:
                    valid = sub < 8 - sh
                    a_s = pltpu.roll(ai, 8 - sh, axis=0)
                    h_s = pltpu.roll(hi, 8 - sh, axis=0)
                else:
                    valid = sub >= sh
                    a_s = pltpu.roll(ai, sh, axis=0)
                    h_s = pltpu.roll(hi, sh, axis=0)
                hi = jnp.where(valid, ai * h_s + hi, hi)
                ai = jnp.where(valid, ai * a_s, ai)
            a_loc.append(ai)
            h_loc.append(hi)
        carry = h_ref[bi]
        outs = [None] * n
        for i in (range(n - 1, -1, -1) if rev else range(n)):
            hi = h_loc[i] + a_loc[i] * carry
            outs[i] = hi
            carry = hi[0:1] if rev else hi[7:8]
        h_ref[bi] = carry
        hs = jnp.concatenate(outs, axis=0)
        if finish:
            y_ref[bi] = ((hs + yp_ref[bi]) * _gelu_tanh(gate_ref[bi])).astype(y_ref.dtype)
        else:
            y_ref[bi] = hs


def _gla_kernel(rev, finish, nchunks, nb, *refs):
    if finish:
        (qk_ref, v_ref, g1_ref, r_ref, yp_ref, s0_ref, wg_ref, bg_ref, ng_ref, y_ref, s_ref) = refs
    else:
        (qk_ref, v_ref, g1_ref, s0_ref, wg_ref, bg_ref, ng_ref, y_ref, s_ref) = refs
    t = GLA_CHUNK
    d = 1 if rev else 0
    dkk = GLA_HEADS * GLA_DK
    j = pl.program_id(0)

    @pl.when(j == 0)
    def _():
        s_ref[...] = s0_ref[...]

    mask = _scan_mask(t, rev)
    maskf = mask.astype(F32)
    rows = lax.broadcasted_iota(jnp.int32, (t, 1), 0)
    lane_head = lax.broadcasted_iota(jnp.int32, (GLA_SUB, dkk), 1) // GLA_DK
    blk = (lax.broadcasted_iota(jnp.int32, (GROUP_W, dkk), 0) // 128
           == lax.broadcasted_iota(jnp.int32, (GROUP_W, dkk), 1) // GLA_DK)
    er = 0 if rev else t - 1

    bs = range(nb)
    q = [qk_ref[bi][:, :dkk] * (GLA_DK ** -0.5) for bi in bs]
    k = [qk_ref[bi][:, dkk:] for bi in bs]
    v = [v_ref[bi] for bi in bs]
    vb = [v[bi].astype(BF16) for bi in bs]
    glog = [_dot(g1_ref[bi].astype(BF16), wg_ref[d]) + bg_ref[d] for bi in bs]
    la = [_log_sigmoid(glog[bi]) * (1.0 / GLA_TAU) for bi in bs]
    b = [_dot_exact(maskf, la[bi]) for bi in bs]
    excl = [b[bi] - la[bi] for bi in bs]
    s_t = [s_ref[bi] for bi in bs]
    o_inter = [_dot_nt((q[bi] * jnp.exp(b[bi])).astype(BF16), s_t[bi].astype(BF16)) for bi in bs]

    att_blocks = [[None] * (t // GLA_SUB) for _ in bs]
    for i in range(t // GLA_SUB):
        r0 = GLA_SUB * i
        if rev:
            ref_row = [excl[bi][r0 + GLA_SUB - 1:r0 + GLA_SUB] for bi in bs]
            kvalid = rows >= r0
        else:
            ref_row = [excl[bi][r0:r0 + 1] for bi in bs]
            kvalid = rows < r0 + GLA_SUB
        qi = [q[bi][r0:r0 + GLA_SUB] * jnp.exp(b[bi][r0:r0 + GLA_SUB] - ref_row[bi]) for bi in bs]
        ki = [(k[bi] * jnp.exp(jnp.where(kvalid, ref_row[bi] - b[bi], NEG))).astype(BF16) for bi in bs]
        qs = [jnp.concatenate([jnp.where(lane_head == h, qi[bi], 0.0) for h in range(GLA_HEADS)],
                              axis=0).astype(BF16) for bi in bs]
        for bi in bs:
            att_blocks[bi][i] = _dot_nt(qs[bi], ki[bi])
    outs = [[None] * GLA_HEADS for _ in bs]
    for h in range(GLA_HEADS):
        att = [jnp.where(mask, jnp.concatenate([ab[GLA_SUB * h:GLA_SUB * h + GLA_SUB] for ab in att_blocks[bi]],
                                               axis=0), 0.0).astype(BF16) for bi in bs]
        for bi in bs:
            outs[bi][h] = _dot(att[bi], vb[bi][:, 128 * h:128 * h + 128])

    bl = [b[bi][er:er + 1] for bi in bs]
    kd = [(k[bi] * jnp.exp(bl[bi] - b[bi])).astype(BF16) for bi in bs]
    v_t = [jnp.concatenate([v[bi][:, 128 * h:128 * h + 128].T for h in range(GLA_HEADS)], axis=0).astype(BF16)
           for bi in bs]
    upd = [_dot(v_t[bi], kd[bi]) for bi in bs]
    for bi in bs:
        s_ref[bi] = s_t[bi] * jnp.exp(bl[bi]) + jnp.where(blk, upd[bi], 0.0)
    for bi in bs:
        o = jnp.concatenate(outs[bi], axis=1) + o_inter[bi]
        if finish:
            o = o + yp_ref[bi]
            o = jnp.concatenate([_rms(o[:, 128 * h:128 * h + 128]) for h in range(GLA_HEADS)], axis=1)
            y_ref[bi] = (o * ng_ref[...] * _silu(r_ref[bi])).astype(y_ref.dtype)
        else:
            y_ref[bi] = o


def _chunk_specs(nb, width, off, t, length, rev, nchunks, halo):
    ob = off // width
    tb = t // HALO

    def cj(j):
        return (nchunks - 1 - j) if rev else j

    main = pl.BlockSpec((nb, t, width), lambda j: (0, cj(j), ob))
    if not halo:
        return [main]
    prev = pl.BlockSpec((nb, HALO, width), lambda j: (0, jnp.maximum(cj(j) * tb - 1, 0), ob))
    nxt = pl.BlockSpec((nb, HALO, width), lambda j: (0, jnp.minimum((cj(j) + 1) * tb, length // HALO - 1), ob))
    return [main, prev, nxt]


def _run_mixer(body, pieces, t, rev, finish, ypart, states, params, name, save_width=0):
    nb, length = pieces[0][0].shape[0], pieces[0][0].shape[1]
    nchunks = length // t
    cj = (lambda j: nchunks - 1 - j) if rev else (lambda j: j)
    row_spec = lambda width: pl.BlockSpec((nb, t, width), lambda j: (0, cj(j), 0))
    in_specs, args = [], []
    for arr, width, off, halo in pieces:
        sp = _chunk_specs(nb, width, off, t, length, rev, nchunks, halo)
        in_specs += sp
        args += [arr] * len(sp)
    if finish:
        in_specs.append(row_spec(GROUP_W))
        args.append(ypart)
    for a in list(states) + list(params):
        in_specs.append(pl.BlockSpec(a.shape, lambda j, _nd=a.ndim: (0,) * _nd))
        args.append(a)
    out_shape = [jax.ShapeDtypeStruct((nb, length, GROUP_W), BF16 if finish else F32)]
    out_specs = [row_spec(GROUP_W)]
    if save_width:
        out_shape.append(jax.ShapeDtypeStruct((nb, length, save_width), F32))
        out_specs.append(row_spec(save_width))
    for s in states:
        out_shape.append(jax.ShapeDtypeStruct(s.shape, s.dtype))
        out_specs.append(pl.BlockSpec(s.shape, lambda j, _nd=s.ndim: (0,) * _nd))
    res = pl.pallas_call(
        functools.partial(body, rev, finish, nchunks, nb),
        grid=(nchunks,),
        in_specs=in_specs,
        out_specs=out_specs,
        out_shape=out_shape,
        compiler_params=_cparams(("arbitrary",)),
        name=name,
    )(*args)
    nfix = 2 if save_width else 1
    return res[0], (res[1] if save_width else None), list(res[nfix:])


def _bidir_mixer(body, bwd_pieces, fwd_pieces, save_width, t, u_ctx, u_lat, zero_states, params, name):
    def run(u, rev, ypart, saved, states, tag):
        src = {"u": u, "saved": saved}
        pieces = [(src[s], w, o, h) for s, w, o, h in (bwd_pieces if rev else fwd_pieces)]
        return _run_mixer(body, pieces, t, rev, not rev, ypart, states, params, name + tag,
                          save_width if rev else 0)

    yb_c, sv_c, st_b = run(u_ctx, True, None, None, zero_states, "_ctx_bwd")
    y_c, _, st_f = run(u_ctx, False, yb_c, sv_c, zero_states, "_ctx_fwd")
    yb_l, sv_l, _ = run(u_lat, True, None, None, st_b, "_lat_bwd")
    y_l, _, _ = run(u_lat, False, yb_l, sv_l, st_f, "_lat_fwd")
    return y_c, y_l


def _mods(cvec, w_mod, b_mod):
    depth, dm, nm = w_mod.shape
    tn = 1024
    return pl.pallas_call(
        _mod_kernel,
        grid=(depth, nm // tn),
        in_specs=[pl.BlockSpec((8, dm), lambda l, j: (0, 0)),
                  pl.BlockSpec((None, dm, tn), lambda l, j: (l, 0, j)),
                  pl.BlockSpec((None, 1, tn), lambda l, j: (l, 0, j))],
        out_specs=pl.BlockSpec((None, 8, tn), lambda l, j: (l, 0, j)),
        out_shape=jax.ShapeDtypeStruct((depth, 8, nm), F32),
        compiler_params=_cparams(("parallel", "parallel")),
        name="mods",
    )(cvec, w_mod, b_mod.reshape(depth, 1, nm))


def _inproj(x, g, shift, scale, w, layer, rows_per_group):
    m, dm = x.shape
    n = w.shape[2]
    tm = TM_PROJ
    grp = lambda i: (i * tm // rows_per_group, 0, 0)
    return pl.pallas_call(
        _inproj_kernel,
        grid=(m // tm,),
        in_specs=[pl.BlockSpec((tm, dm), lambda i: (i, 0)),
                  _resident((1, dm)),
                  pl.BlockSpec((1, 1, dm), grp),
                  pl.BlockSpec((1, 1, dm), grp),
                  _resident((dm, n), layer)],
        out_specs=pl.BlockSpec((tm, n), lambda i: (i, 0)),
        out_shape=jax.ShapeDtypeStruct((m, n), F32),
        compiler_params=_cparams(("parallel",)),
        name="inproj",
    )(x, g, shift, scale, w)


def _outproj(ys, w, layer, x, gate, g, shift, scale, rows_per_group):
    m, dm = x.shape
    tm = min(TM_OUT, rows_per_group)
    grp = lambda i: (i * tm // rows_per_group, 0, 0)
    yspec = pl.BlockSpec((tm, GROUP_W), lambda i: (i, 0))
    return pl.pallas_call(
        _outproj_kernel,
        grid=(m // tm,),
        in_specs=[yspec, yspec, yspec, yspec,
                  _resident((4, GROUP_W, dm), layer),
                  pl.BlockSpec((tm, dm), lambda i: (i, 0)),
                  pl.BlockSpec((1, 1, dm), grp),
                  _resident((1, dm)),
                  pl.BlockSpec((1, 1, dm), grp),
                  pl.BlockSpec((1, 1, dm), grp)],
        out_specs=[pl.BlockSpec((tm, dm), lambda i: (i, 0)), pl.BlockSpec((tm, dm), lambda i: (i, 0))],
        out_shape=[jax.ShapeDtypeStruct((m, dm), F32), jax.ShapeDtypeStruct((m, dm), BF16)],
        compiler_params=_cparams(("parallel",)),
        name="outproj",
    )(*ys, w, x, gate, g, shift, scale)


def _ffn_up(h, wg, wu, layer):
    m, dm = h.shape
    dff = wg.shape[2]
    tm = min(TM_FFN_UP, m)
    tn = TN_FFN_UP
    return pl.pallas_call(
        _ffn_up_kernel,
        grid=(m // tm, dff // tn),
        in_specs=[pl.BlockSpec((tm, dm), lambda i, j: (i, 0)),
                  pl.BlockSpec((None, dm, tn), lambda i, j: (layer, 0, j)),
                  pl.BlockSpec((None, dm, tn), lambda i, j: (layer, 0, j))],
        out_specs=pl.BlockSpec((tm, tn), lambda i, j: (i, j)),
        out_shape=jax.ShapeDtypeStruct((m, dff), BF16),
        compiler_params=_cparams(("parallel", "arbitrary")),
        name="ffn_up",
    )(h, wg, wu)


def _ffn_down(a, w, layer, x, gate, final_g, final, rows_per_group):
    m, dm = x.shape
    dff = a.shape[1]
    tm = TM_FFN_DOWN
    grp = lambda i: (i * tm // rows_per_group, 0, 0)
    return pl.pallas_call(
        functools.partial(_ffn_down_kernel, final),
        grid=(m // tm,),
        in_specs=[pl.BlockSpec((tm, dff), lambda i: (i, 0)),
                  _resident((dff, dm), layer),
                  pl.BlockSpec((tm, dm), lambda i: (i, 0)),
                  pl.BlockSpec((1, 1, dm), grp),
                  _resident((1, dm))],
        out_specs=pl.BlockSpec((tm, dm), lambda i: (i, 0)),
        out_shape=jax.ShapeDtypeStruct((m, dm), F32),
        compiler_params=_cparams(("parallel",)),
        name="ffn_down",
    )(a, w, x, gate, final_g)


def _pad_cols(a, width):
    return jnp.pad(a, ((0, 0), (0, width - a.shape[1])))


def _split_w_in(w):
    w = w.astype(BF16)
    ssd, ml, lru, gla = jnp.split(w, [1552, 1552 + 2064, 1552 + 2064 + 1024], axis=2)
    ssd_z, ssd_xbc, ssd_dt = ssd[..., :512], ssd[..., 512:1536], ssd[..., 1536:]
    ml_qkvo, ml_g = ml[..., :2048], ml[..., 2048:]
    lru_gate, lru_x = lru[..., :512], lru[..., 512:]
    gla_qkvr, gla_g1 = gla[..., :1536], gla[..., 1536:]
    pad = lambda a: jnp.pad(a, ((0, 0), (0, 0), (0, 128 - a.shape[2])))
    w_row = jnp.concatenate([ssd_xbc, lru_x, lru_gate, ssd_z, pad(ssd_dt)], axis=2)
    w_col = jnp.concatenate([ml_qkvo, gla_qkvr, pad(ml_g), pad(gla_g1)], axis=2)
    return w_row, w_col


def _row128(a):
    return _pad_cols(a.reshape(1, -1).astype(F32), 128)


def _block_diag(w):
    nb, bi, bj = w.shape
    eye = jnp.eye(nb, dtype=w.dtype)
    return (eye[:, None, :, None] * w[:, :, None, :]).reshape(nb * bi, nb * bj)


def _to_colmajor(a, bsz, rows):
    n = a.shape[-1]
    return a.reshape(bsz, rows, GRID_W, n).transpose(0, 2, 1, 3).reshape(bsz, rows * GRID_W, n)


def _from_colmajor(a, bsz, rows):
    n = a.shape[-1]
    return a.reshape(bsz, GRID_W, rows, n).transpose(0, 2, 1, 3).reshape(bsz, rows * GRID_W, n)


def kernel(x, c, ctx, c_ctx, norm1_g, norm2_g, w_mod, b_mod, w_in, w_out, ssd_conv_w, ssd_conv_b, ssd_dt_bias, ssd_a_log, ssd_d, ssd_norm_g, ml_conv_w, ml_conv_b, ml_igate_b, ml_fgate_b, ml_norm_g, lru_conv_w, lru_conv_b, lru_wa, lru_ba, lru_wx, lru_bx, lru_lambda, gla_wg2, gla_bg, gla_norm_g, w_gate, w_up, w_down, final_g):
    bsz, length, dm = x.shape
    lctx = ctx.shape[1]
    depth = w_in.shape[0]
    rows = length // GRID_W
    assert dm == D_MODEL and length % SSD_CHUNK == 0 and lctx % SSD_CHUNK == 0

    cvec = jnp.concatenate([c, c_ctx[None, :], jnp.zeros((8 - bsz - 1, dm), F32)], axis=0)
    mods = _mods(cvec, w_mod, b_mod)

    xl = x.reshape(bsz * length, dm)
    xt = ctx.reshape(bsz * lctx, dm)
    row = lambda a: a.reshape(1, -1).astype(F32)
    w_row, w_col = _split_w_in(w_in)
    w_out_b = w_out.astype(BF16).reshape(depth, 4, GROUP_W, dm)
    w_down_b = w_down.astype(BF16)

    for i in range(depth):
        need_ctx = i < depth - 1
        m_l = [mods[i, :bsz, k * dm:(k + 1) * dm].reshape(bsz, 1, dm) for k in range(N_MOD)]
        m_t = [mods[i, bsz:bsz + 1, k * dm:(k + 1) * dm].reshape(1, 1, dm) for k in range(N_MOD)]
        g1 = row(norm1_g[i])
        xl_cm = _to_colmajor(xl.reshape(bsz, length, dm), bsz, rows).reshape(bsz * length, dm)
        ul_row = _inproj(xl, g1, m_l[0], m_l[1], w_row, i, length).reshape(bsz, length, N_ROW)
        ul_col = _inproj(xl_cm, g1, m_l[0], m_l[1], w_col, i, length).reshape(bsz, length, N_COL)
        ut_row = _inproj(xt, g1, m_t[0], m_t[1], w_row, i, bsz * lctx).reshape(bsz, lctx, N_ROW)
        ut_col = _inproj(xt, g1, m_t[0], m_t[1], w_col, i, bsz * lctx).reshape(bsz, lctx, N_COL)

        ssd_params = [ssd_conv_w[i], row(ssd_conv_b[i]), _row128(ssd_dt_bias[i]), _row128(ssd_a_log[i]),
                      row(jnp.repeat(ssd_d[i], GROUP_W // SSD_HEADS)), row(ssd_norm_g[i])]
        ssd_bwd = [("u", 1024, OFF_SSD_XBC, True), ("u", 128, OFF_SSD_DT, False)]
        ssd_fwd = [("saved", 1024, 0, False), ("u", 128, OFF_SSD_DT, False), ("u", 512, OFF_SSD_Z, False)]
        ssd_zero = [jnp.zeros((bsz, SSD_HEADS // 2, 128, 128), F32)]
        ya_t, ya_l = _bidir_mixer(_ssd_kernel, ssd_bwd, ssd_fwd, 1024, SSD_CHUNK, ut_row, ul_row, ssd_zero,
                                  ssd_params, "ssd")

        ml_gb = _row128(jnp.concatenate([ml_igate_b[i].reshape(-1), ml_fgate_b[i].reshape(-1)]))
        ml_params = [ml_conv_w[i], row(ml_conv_b[i]), ml_gb, row(ml_norm_g[i])]
        ml_bwd = [("u", 512, OFF_ML_Q, True), ("u", 512, OFF_ML_K, True), ("u", 512, OFF_ML_V, False),
                  ("u", 128, OFF_ML_G, False)]
        ml_fwd = [("saved", 1024, 0, False), ("u", 512, OFF_ML_V, False), ("u", 128, OFF_ML_G, False),
                  ("u", 512, OFF_ML_O, False)]
        ml_zero = [jnp.zeros((bsz, ML_HEADS, 128, 128), F32), jnp.zeros((bsz, 2 * ML_HEADS, 128), F32)]
        yb_t, yb_l = _bidir_mixer(_mlstm_kernel, ml_bwd, ml_fwd, 1024, ML_CHUNK, ut_col, ul_col, ml_zero,
                                  ml_params, "mlstm")

        lru_w = jnp.stack([jnp.concatenate([_block_diag(lru_wa[i, dd]), _block_diag(lru_wx[i, dd])], axis=1)
                           for dd in range(2)]).astype(BF16)
        lru_bias = jnp.concatenate([lru_ba[i], lru_bx[i]], axis=1).reshape(2, 1, 2 * GROUP_W)
        lru_params = [lru_conv_w[i], row(lru_conv_b[i]), lru_w, lru_bias, lru_lambda[i].reshape(2, 1, GROUP_W)]
        lru_bwd = [("u", 512, OFF_LRU_X, True)]
        lru_fwd = [("saved", 512, 0, False), ("u", 512, OFF_LRU_GATE, False)]
        lru_zero = [jnp.zeros((bsz, 1, GROUP_W), F32)]
        yc_t, yc_l = _bidir_mixer(_lru_kernel, lru_bwd, lru_fwd, 512, LRU_CHUNK, ut_row, ul_row, lru_zero,
                                  lru_params, "lru")

        wg = jnp.zeros((2, 128, GLA_HEADS * GLA_DK), F32)
        wg = wg.at[0, :GLA_RANK].set(gla_wg2[i, 0]).at[1, GLA_RANK:2 * GLA_RANK].set(gla_wg2[i, 1])
        gla_params = [wg.astype(BF16), gla_bg[i].reshape(2, 1, GLA_HEADS * GLA_DK), row(gla_norm_g[i])]
        gla_bwd = [("u", 512, OFF_GLA_QK, False), ("u", 512, OFF_GLA_V, False), ("u", 128, OFF_GLA_G1, False)]
        gla_fwd = gla_bwd + [("u", 512, OFF_GLA_R, False)]
        gla_zero = [jnp.zeros((bsz, GROUP_W, GLA_HEADS * GLA_DK), F32)]
        yd_t, yd_l = _bidir_mixer(_gla_kernel, gla_bwd, gla_fwd, 0, GLA_CHUNK, ut_col, ul_col, gla_zero,
                                  gla_params, "gla")

        yb_l = _from_colmajor(yb_l, bsz, rows)
        yd_l = _from_colmajor(yd_l, bsz, rows)

        fg = row(final_g)

        def tail(xs, ys, m, rows_per_group, final):
            ys = [y.reshape(-1, GROUP_W) for y in ys]
            xs, h2 = _outproj(ys, w_out_b, i, xs, m[2], row(norm2_g[i]), m[3], m[4], rows_per_group)
            act = _ffn_up(h2, w_gate, w_up, i)
            return _ffn_down(act, w_down_b, i, xs, m[5], fg, final, rows_per_group)

        xl = tail(xl, [ya_l, yb_l, yc_l, yd_l], m_l, length, i == depth - 1)
        if need_ctx:
            xt = tail(xt, [ya_t, yb_t, yc_t, yd_t], m_t, bsz * lctx, False)
    return xl.reshape(bsz, length, dm)
```

```python
import functools

import jax
import jax.numpy as jnp
from jax import lax
from jax.experimental import pallas as pl
from jax.experimental.pallas import tpu as pltpu

F32 = jnp.float32
BF16 = jnp.bfloat16
EPS = 1e-6
NEG = -1e30

D_MODEL = 2048
GRID_W = 64
GROUP_W = 512
N_MOD = 6
CONV_W = 5
HALO = 8

SSD_HEADS = 8
SSD_CHUNK = 128
ML_HEADS = 4
ML_CHUNK = 64
ML_HEAD_GROUP = 4
LRU_C = 8.0
LRU_CHUNK = 128
GLA_HEADS = 4
GLA_DK = 64
GLA_RANK = 16
GLA_TAU = 16.0
GLA_CHUNK = 64
GLA_SUB = 16

OFF_SSD_XBC = 0
OFF_LRU_X = 1024
OFF_LRU_GATE = 1536
OFF_SSD_Z = 2048
OFF_SSD_DT = 2560
N_ROW = 2688
OFF_ML_Q = 0
OFF_ML_K = 512
OFF_ML_V = 1024
OFF_ML_O = 1536
OFF_GLA_QK = 2048
OFF_GLA_V = 2560
OFF_GLA_R = 3072
OFF_ML_G = 3584
OFF_GLA_G1 = 3712
N_COL = 3840

VMEM_LIMIT = 56 * 1024 * 1024
TM_PROJ = 256
COLS_PER_TILE = 8
ROWS_PER_TILE = 8
TM_OUT = 512
TM_FFN_UP = 2048
TN_FFN_UP = 512
TM_FFN_DOWN = 256


def _cparams(sem):
    return pltpu.CompilerParams(dimension_semantics=sem, vmem_limit_bytes=VMEM_LIMIT)


def _resident(shape, layer=None):
    if layer is None:
        nd = len(shape)
        return pl.BlockSpec(shape, lambda *_: (0,) * nd, pipeline_mode=pl.Buffered(1))
    nd = len(shape)
    return pl.BlockSpec((None,) + tuple(shape), lambda *_: (layer,) + (0,) * nd, pipeline_mode=pl.Buffered(1))


def _sigmoid(x):
    return 0.5 + 0.5 * jnp.tanh(0.5 * x)


def _silu(x):
    return x * _sigmoid(x)


def _softplus(x):
    return jnp.maximum(x, 0.0) + jnp.log(1.0 + jnp.exp(-jnp.abs(x)))


def _log_sigmoid(x):
    return jnp.minimum(x, 0.0) - jnp.log(1.0 + jnp.exp(-jnp.abs(x)))


def _gelu_tanh(x):
    return 0.5 * x * (1.0 + jnp.tanh(0.7978845608028654 * (x + 0.044715 * (x * x * x))))


def _rms(x):
    return x * lax.rsqrt(jnp.mean(x * x, axis=-1, keepdims=True) + EPS)


def _dot(a, b):
    return jnp.dot(a, b, preferred_element_type=F32)


def _dot_nt(a, b):
    return lax.dot_general(a, b, (((1,), (1,)), ((), ())), preferred_element_type=F32)


def _dot_exact(a, b):
    return jnp.dot(a, b, precision=lax.Precision.HIGHEST, preferred_element_type=F32)


def _scan_mask(t, rev):
    ri = lax.broadcasted_iota(jnp.int32, (t, t), 0)
    ci = lax.broadcasted_iota(jnp.int32, (t, t), 1)
    return (ci >= ri) if rev else (ci <= ri)


def _conv5(prev, main, nxt, w, bias, first, last):
    t = main.shape[0]
    prev = jnp.where(first, 0.0, prev)
    nxt = jnp.where(last, 0.0, nxt)
    ext = jnp.concatenate([prev, main, nxt], axis=0)
    n = t + 2 * HALO
    acc = None
    for k in range(CONV_W):
        sh = (CONV_W // 2 - k) % n
        r = ext if sh == 0 else pltpu.roll(ext, sh, axis=0)
        term = r[HALO:HALO + t] * w[k:k + 1, :]
        acc = term if acc is None else acc + term
    return acc + bias


def _mod_kernel(c_ref, w_ref, b_ref, o_ref):
    s = _silu(c_ref[...]).astype(BF16)
    o_ref[...] = _dot(s, w_ref[...].astype(BF16)) + b_ref[...]


def _norm_mod(x, g_ref, sh_ref, sc_ref):
    return ((_rms(x) * g_ref[...]) * (1.0 + sc_ref[0]) + sh_ref[0]).astype(BF16)


def _inproj_kernel(x_ref, g_ref, sh_ref, sc_ref, w_ref, o_ref):
    o_ref[...] = _dot(_norm_mod(x_ref[...], g_ref, sh_ref, sc_ref), w_ref[...])


def _inproj_colmajor_kernel(x_ref, g_ref, sh_ref, sc_ref, w_ref, o_ref, xs_ref):
    rows = x_ref.shape[1]
    for wl in range(COLS_PER_TILE):
        xs_ref[wl * rows:(wl + 1) * rows, :] = x_ref[0, :, wl, :]
    o_ref[...] = _dot(_norm_mod(xs_ref[...], g_ref, sh_ref, sc_ref), w_ref[...])


def _outproj_kernel(colmajor, ya_ref, yb_ref, yc_ref, yd_ref, w_ref, x_ref, gate_ref, g_ref, sh_ref, sc_ref,
                    xo_ref, h_ref):
    if colmajor:
        def rowmajor(ref):
            return jnp.concatenate([ref[0, :, rl, :] for rl in range(ROWS_PER_TILE)], axis=0).astype(BF16)
        yb, yd = rowmajor(yb_ref), rowmajor(yd_ref)
    else:
        yb, yd = yb_ref[...], yd_ref[...]
    acc = _dot(ya_ref[...], w_ref[0])
    acc = acc + _dot(yc_ref[...], w_ref[2])
    acc = acc + _dot(yb, w_ref[1])
    acc = acc + _dot(yd, w_ref[3])
    xn = x_ref[...] + gate_ref[0] * acc
    xo_ref[...] = xn
    y = _rms(xn) * g_ref[...]
    h_ref[...] = (y * (1.0 + sc_ref[0]) + sh_ref[0]).astype(BF16)


def _ffn_up_kernel(h_ref, wg_ref, wu_ref, o_ref):
    h = h_ref[...]
    g = _dot(h, wg_ref[...].astype(BF16))
    u = _dot(h, wu_ref[...].astype(BF16))
    o_ref[...] = (_silu(g) * u).astype(BF16)


def _ffn_down_kernel(final, a_ref, w_ref, x_ref, gate_ref, fg_ref, o_ref):
    xn = x_ref[...] + gate_ref[0] * _dot(a_ref[...], w_ref[...])
    if final:
        xn = _rms(xn) * fg_ref[...]
    o_ref[...] = xn


def _ssd_kernel(rev, finish, nchunks, nb, *refs):
    if finish:
        (xc_ref, dt_ref, z_ref, yp_ref, s0_ref, cw_ref, cb_ref, dtb_ref, alog_ref,
         dsk_ref, ng_ref, y_ref, s_ref) = refs
    else:
        (xbc_ref, xp_ref, xn_ref, dt_ref, s0_ref, cw_ref, cb_ref, dtb_ref, alog_ref,
         dsk_ref, ng_ref, y_ref, xc_ref, s_ref) = refs
    t = SSD_CHUNK
    d = 1 if rev else 0
    j = pl.program_id(0)
    c = (nchunks - 1 - j) if rev else j

    @pl.when(j == 0)
    def _():
        s_ref[...] = s0_ref[...]

    mask = _scan_mask(t, rev)
    maskf = mask.astype(F32)
    er = 0 if rev else t - 1
    lo = lax.broadcasted_iota(jnp.int32, (t, 128), 1) < 64
    lo_rows = lax.broadcasted_iota(jnp.int32, (128, 128), 0) < 64
    cw, cbias = cw_ref[...], cb_ref[...]
    nega = -jnp.exp(alog_ref[...])

    bs = range(nb)
    if finish:
        xbc = [xc_ref[bi] for bi in bs]
    else:
        xbc = [_silu(_conv5(xp_ref[bi], xbc_ref[bi], xn_ref[bi], cw, cbias, c == 0, c == nchunks - 1))
               for bi in bs]
        for bi in bs:
            xc_ref[bi] = xbc[bi]
    dt_all = [_softplus(dt_ref[bi] + dtb_ref[...]) for bi in bs]
    b_all = [_dot_exact(maskf, dt_all[bi] * nega) for bi in bs]
    b_t = [b_all[bi].T for bi in bs]
    dt_t = [dt_all[bi].T for bi in bs]
    ys = [[None] * (SSD_HEADS // 2) for _ in bs]
    for p in range(SSD_HEADS // 2):
        g = p // 2
        bm = [xbc[bi][:, 512 + 128 * g:640 + 128 * g].astype(BF16) for bi in bs]
        cm = [xbc[bi][:, 768 + 128 * g:896 + 128 * g].astype(BF16) for bi in bs]
        cb = [_dot_nt(cm[bi], bm[bi]) for bi in bs]
        xpair = [xbc[bi][:, 128 * p:128 * p + 128] for bi in bs]
        xpair_b = [xpair[bi].astype(BF16) for bi in bs]
        s_pair = [s_ref[bi, p] for bi in bs]
        y_inter = [_dot_nt(cm[bi], s_pair[bi].astype(BF16)) for bi in bs]
        yh, eb, coef, dec = [], [], [], []
        for hh in range(2):
            l = d * SSD_HEADS + 2 * p + hh
            bcol = [b_all[bi][:, l:l + 1] for bi in bs]
            seg = [jnp.exp(jnp.where(mask, bcol[bi] - b_t[bi][l:l + 1, :], NEG)) for bi in bs]
            w = [(cb[bi] * seg[bi] * dt_t[bi][l:l + 1, :]).astype(BF16) for bi in bs]
            yh.append([_dot(w[bi], xpair_b[bi]) for bi in bs])
            eb.append([jnp.exp(bcol[bi]) for bi in bs])
            bl = [b_all[bi][er:er + 1, l:l + 1] for bi in bs]
            coef.append([jnp.exp(bl[bi] - bcol[bi]) * dt_all[bi][:, l:l + 1] for bi in bs])
            dec.append([jnp.exp(bl[bi]) for bi in bs])
        xw = [(xpair[bi] * jnp.where(lo, coef[0][bi], coef[1][bi])).T.astype(BF16) for bi in bs]
        upd = [_dot(xw[bi], bm[bi]) for bi in bs]
        for bi in bs:
            ys[bi][p] = (jnp.where(lo, yh[0][bi], yh[1][bi])
                         + y_inter[bi] * jnp.where(lo, eb[0][bi], eb[1][bi]))
            s_ref[bi, p] = jnp.where(lo_rows, dec[0][bi], dec[1][bi]) * s_pair[bi] + upd[bi]
    for bi in bs:
        y = jnp.concatenate(ys[bi], axis=1)
        if finish:
            y = y + yp_ref[bi] + dsk_ref[...] * xbc[bi][:, :GROUP_W]
            y = _rms(y * _silu(z_ref[bi])) * ng_ref[...]
            y_ref[bi] = y.astype(y_ref.dtype)
        else:
            y_ref[bi] = y


def _mlstm_kernel(rev, finish, nchunks, nb, *refs):
    if finish:
        (qkc_ref, v_ref, g_ref, o_ref, yp_ref, cs0_ref, sm0_ref,
         cw_ref, cb_ref, gb_ref, ng_ref, y_ref, cs_ref, sm_ref) = refs
    else:
        (q_ref, qp_ref, qn_ref, k_ref, kp_ref, kn_ref, v_ref, g_ref, cs0_ref, sm0_ref,
         cw_ref, cb_ref, gb_ref, ng_ref, y_ref, qkc_ref, cs_ref, sm_ref) = refs
    t = ML_CHUNK
    d = 1 if rev else 0
    j = pl.program_id(0)
    c = (nchunks - 1 - j) if rev else j
    first, last = c == 0, c == nchunks - 1

    @pl.when(j == 0)
    def _():
        cs_ref[...] = cs0_ref[...]
        sm_ref[...] = sm0_ref[...]

    cw = cw_ref[...]
    cbias = cb_ref[...]
    maskf = _scan_mask(t, rev).astype(F32)
    mask_t = _scan_mask(t, not rev)
    er = 0 if rev else t - 1

    bs = range(nb)
    if finish:
        q = [qkc_ref[bi][:, :GROUP_W] for bi in bs]
        k = [qkc_ref[bi][:, GROUP_W:] for bi in bs]
    else:
        q = [_silu(_conv5(qp_ref[bi], q_ref[bi], qn_ref[bi], cw[:, :GROUP_W], cbias[:, :GROUP_W], first, last))
             for bi in bs]
        k = [_silu(_conv5(kp_ref[bi], k_ref[bi], kn_ref[bi], cw[:, GROUP_W:], cbias[:, GROUP_W:], first, last))
             * (128.0 ** -0.5) for bi in bs]
        for bi in bs:
            qkc_ref[bi, :, :GROUP_W] = q[bi]
            qkc_ref[bi, :, GROUP_W:] = k[bi]
    v = [v_ref[bi] for bi in bs]
    gts = [g_ref[bi] + gb_ref[...] for bi in bs]
    b_all = [_dot_exact(maskf, _log_sigmoid(gts[bi])) for bi in bs]
    g_t = [gts[bi].T for bi in bs]
    b_t = [b_all[bi].T for bi in bs]
    ys = [[None] * ML_HEADS for _ in bs]
    for h0 in range(0, ML_HEADS, ML_HEAD_GROUP):
        ch = [(bi, h) for h in range(h0, h0 + ML_HEAD_GROUP) for bi in bs]
        cs_ = range(len(ch))
        li = [d * ML_HEADS + h for _, h in ch]
        lf = [2 * ML_HEADS + l for l in li]
        sl = [slice(128 * h, 128 * h + 128) for _, h in ch]
        brow = [b_t[bi][lf[c]:lf[c] + 1, :] for c, (bi, h) in enumerate(ch)]
        igrow = [g_t[bi][li[c]:li[c] + 1, :] for c, (bi, h) in enumerate(ch)]
        ccol = [gts[bi][:, li[c]:li[c] + 1] - b_all[bi][:, lf[c]:lf[c] + 1]
                for c, (bi, h) in enumerate(ch)]
        m_prev = [sm_ref[bi, ML_HEADS + h:ML_HEADS + h + 1, 0:1] for bi, h in ch]
        ns = [sm_ref[bi, h:h + 1, :] for bi, h in ch]
        cs = [cs_ref[bi, h] for bi, h in ch]
        dmat = [jnp.where(mask_t, brow[c] + ccol[c], NEG) for c in cs_]
        inter = [brow[c] + m_prev[c] for c in cs_]
        mt = [jnp.maximum(inter[c], jnp.max(dmat[c], axis=0, keepdims=True)) for c in cs_]
        w = [jnp.exp(dmat[c] - mt[c]) for c in cs_]
        sc = [jnp.exp(inter[c] - mt[c]) for c in cs_]
        qb = [q[bi][:, sl[c]].astype(BF16) for c, (bi, h) in enumerate(ch)]
        kb = [k[bi][:, sl[c]].astype(BF16) for c, (bi, h) in enumerate(ch)]
        sw = [_dot_nt(kb[c], qb[c]) * w[c] for c in cs_]
        v_t = [v[bi][:, sl[c]].T for c, (bi, h) in enumerate(ch)]
        num = [_dot(v_t[c].astype(BF16), sw[c].astype(BF16)) for c in cs_]
        qc = [_dot_nt(cs[c].astype(BF16), qb[c]) for c in cs_]
        qn = [_dot_nt(jnp.broadcast_to(ns[c], (8, 128)).astype(BF16), qb[c])[0:1] for c in cs_]
        den = [jnp.sum(sw[c], axis=0, keepdims=True) + sc[c] * qn[c] for c in cs_]
        for c, (bi, h) in enumerate(ch):
            inv = 1.0 / jnp.maximum(jnp.abs(den[c]), jnp.exp(-mt[c]))
            ys[bi][h] = ((num[c] + sc[c] * qc[c]) * inv).T
        bl = [b_all[bi][er:er + 1, lf[c]:lf[c] + 1] for c, (bi, h) in enumerate(ch)]
        tail = [bl[c] - brow[c] + igrow[c] for c in cs_]
        m_new = [jnp.maximum(bl[c] + m_prev[c], jnp.max(tail[c], axis=1, keepdims=True)) for c in cs_]
        ws = [jnp.exp(tail[c] - m_new[c]) for c in cs_]
        sc_end = [jnp.exp(bl[c] + m_prev[c] - m_new[c]) for c in cs_]
        upd = [_dot((v_t[c] * ws[c]).astype(BF16), kb[c]) for c in cs_]
        nup = [_dot(jnp.broadcast_to(ws[c], (8, t)).astype(BF16), kb[c])[0:1] for c in cs_]
        for c, (bi, h) in enumerate(ch):
            cs_ref[bi, h] = sc_end[c] * cs[c] + upd[c]
            sm_ref[bi, h:h + 1, :] = sc_end[c] * ns[c] + nup[c]
            sm_ref[bi, ML_HEADS + h:ML_HEADS + h + 1, :] = jnp.broadcast_to(m_new[c], (1, 128))
    for bi in bs:
        if finish:
            yp = yp_ref[bi]
            outs = [_rms(ys[bi][h] + yp[:, 128 * h:128 * h + 128]) for h in range(ML_HEADS)]
            y = jnp.concatenate(outs, axis=1) * ng_ref[...]
            y_ref[bi] = (_sigmoid(o_ref[bi]) * y).astype(y_ref.dtype)
        else:
            y_ref[bi] = jnp.concatenate(ys[bi], axis=1)


def _lru_kernel(rev, finish, nchunks, nb, *refs):
    if finish:
        (xf_ref, gate_ref, yp_ref, h0_ref, cw_ref, cb_ref, w_ref, bias_ref, lam_ref,
         y_ref, h_ref) = refs
    else:
        (x_ref, xp_ref, xn_ref, h0_ref, cw_ref, cb_ref, w_ref, bias_ref, lam_ref,
         y_ref, xf_ref, h_ref) = refs
    t = LRU_CHUNK
    d = 1 if rev else 0
    j = pl.program_id(0)
    c = (nchunks - 1 - j) if rev else j

    @pl.when(j == 0)
    def _():
        h_ref[...] = h0_ref[...]

    cw, cbias = cw_ref[...], cb_ref[...]
    lsl = LRU_C * _log_sigmoid(lam_ref[d])
    sub = lax.broadcasted_iota(jnp.int32, (8, GROUP_W), 0)
    n = t // 8

    for bi in range(nb):
        if finish:
            xf = xf_ref[bi]
        else:
            xf = _conv5(xp_ref[bi], x_ref[bi], xn_ref[bi], cw, cbias, c == 0, c == nchunks - 1)
            xf_ref[bi] = xf
        pre = _dot(xf.astype(BF16), w_ref[d]) + bias_ref[d]
        r = _sigmoid(pre[:, :GROUP_W])
        ig = _sigmoid(pre[:, GROUP_W:])
        loga = r * lsl
        a = jnp.exp(loga)
        bx = jnp.sqrt(-jnp.tanh(loga) * (1.0 + a * a)) * ig * xf

        a_loc, h_loc = [], []
        for i in range(n):
            ai = a[8 * i:8 * i + 8]
            hi = bx[8 * i:8 * i + 8]
            for sh in (1, 2, 4):
                if rev:
                    valid = sub < 8 - sh
                    a_s = pltpu.roll(ai, 8 - sh, axis=0)
                    h_s = pltpu.roll(hi, 8 - sh, axis=0)
                else:
                    valid = sub >= sh
                    a_s = pltpu.roll(ai, sh, axis=0)
                    h_s = pltpu.roll(hi, sh, axis=0)
                hi = jnp.where(valid, ai * h_s + hi, hi)
                ai = jnp.where(valid, ai * a_s, ai)
            a_loc.append(ai)
            h_loc.append(hi)
        carry = h_ref[bi]
        outs = [None] * n
        for i in (range(n - 1, -1, -1) if rev else range(n)):
            hi = h_loc[i] + a_loc[i] * carry
            outs[i] = hi
            carry = hi[0:1] if rev else hi[7:8]
        h_ref[bi] = carry
        hs = jnp.concatenate(outs, axis=0)
        if finish:
            y_ref[bi] = ((hs + yp_ref[bi]) * _gelu_tanh(gate_ref[bi])).astype(y_ref.dtype)
        else:
            y_ref[bi] = hs


def _gla_kernel(rev, finish, nchunks, nb, *refs):
    if finish:
        (qk_ref, v_ref, g1_ref, r_ref, yp_ref, s0_ref, wg_ref, bg_ref, ng_ref, y_ref, s_ref) = refs
    else:
        (qk_ref, v_ref, g1_ref, s0_ref, wg_ref, bg_ref, ng_ref, y_ref, s_ref) = refs
    t = GLA_CHUNK
    d = 1 if rev else 0
    dkk = GLA_HEADS * GLA_DK
    j = pl.program_id(0)

    @pl.when(j == 0)
    def _():
        s_ref[...] = s0_ref[...]

    mask = _scan_mask(t, rev)
    maskf = mask.astype(F32)
    rows = lax.broadcasted_iota(jnp.int32, (t, 1), 0)
    lane_head = lax.broadcasted_iota(jnp.int32, (GLA_SUB, dkk), 1) // GLA_DK
    blk = (lax.broadcasted_iota(jnp.int32, (GROUP_W, dkk), 0) // 128
           == lax.broadcasted_iota(jnp.int32, (GROUP_W, dkk), 1) // GLA_DK)
    er = 0 if rev else t - 1

    bs = range(nb)
    q = [qk_ref[bi][:, :dkk] * (GLA_DK ** -0.5) for bi in bs]
    k = [qk_ref[bi][:, dkk:] for bi in bs]
    v = [v_ref[bi] for bi in bs]
    vb = [v[bi].astype(BF16) for bi in bs]
    glog = [_dot(g1_ref[bi].astype(BF16), wg_ref[d]) + bg_ref[d] for bi in bs]
    la = [_log_sigmoid(glog[bi]) * (1.0 / GLA_TAU) for bi in bs]
    b = [_dot_exact(maskf, la[bi]) for bi in bs]
    excl = [b[bi] - la[bi] for bi in bs]
    s_t = [s_ref[bi] for bi in bs]
    o_inter = [_dot_nt((q[bi] * jnp.exp(b[bi])).astype(BF16), s_t[bi].astype(BF16)) for bi in bs]

    att_blocks = [[None] * (t // GLA_SUB) for _ in bs]
    for i in range(t // GLA_SUB):
        r0 = GLA_SUB * i
        if rev:
            ref_row = [excl[bi][r0 + GLA_SUB - 1:r0 + GLA_SUB] for bi in bs]
            kvalid = rows >= r0
        else:
            ref_row = [excl[bi][r0:r0 + 1] for bi in bs]
            kvalid = rows < r0 + GLA_SUB
        qi = [q[bi][r0:r0 + GLA_SUB] * jnp.exp(b[bi][r0:r0 + GLA_SUB] - ref_row[bi]) for bi in bs]
        ki = [(k[bi] * jnp.exp(jnp.where(kvalid, ref_row[bi] - b[bi], NEG))).astype(BF16) for bi in bs]
        qs = [jnp.concatenate([jnp.where(lane_head == h, qi[bi], 0.0) for h in range(GLA_HEADS)],
                              axis=0).astype(BF16) for bi in bs]
        for bi in bs:
            att_blocks[bi][i] = _dot_nt(qs[bi], ki[bi])
    outs = [[None] * GLA_HEADS for _ in bs]
    for h in range(GLA_HEADS):
        att = [jnp.where(mask, jnp.concatenate([ab[GLA_SUB * h:GLA_SUB * h + GLA_SUB] for ab in att_blocks[bi]],
                                               axis=0), 0.0).astype(BF16) for bi in bs]
        for bi in bs:
            outs[bi][h] = _dot(att[bi], vb[bi][:, 128 * h:128 * h + 128])

    bl = [b[bi][er:er + 1] for bi in bs]
    kd = [(k[bi] * jnp.exp(bl[bi] - b[bi])).astype(BF16) for bi in bs]
    v_t = [jnp.concatenate([v[bi][:, 128 * h:128 * h + 128].T for h in range(GLA_HEADS)], axis=0).astype(BF16)
           for bi in bs]
    upd = [_dot(v_t[bi], kd[bi]) for bi in bs]
    for bi in bs:
        s_ref[bi] = s_t[bi] * jnp.exp(bl[bi]) + jnp.where(blk, upd[bi], 0.0)
    for bi in bs:
        o = jnp.concatenate(outs[bi], axis=1) + o_inter[bi]
        if finish:
            o = o + yp_ref[bi]
            o = jnp.concatenate([_rms(o[:, 128 * h:128 * h + 128]) for h in range(GLA_HEADS)], axis=1)
            y_ref[bi] = (o * ng_ref[...] * _silu(r_ref[bi])).astype(y_ref.dtype)
        else:
            y_ref[bi] = o


def _chunk_specs(nb, width, off, t, length, rev, nchunks, halo):
    ob = off // width
    tb = t // HALO

    def cj(j):
        return (nchunks - 1 - j) if rev else j

    main = pl.BlockSpec((nb, t, width), lambda j: (0, cj(j), ob))
    if not halo:
        return [main]
    prev = pl.BlockSpec((nb, HALO, width), lambda j: (0, jnp.maximum(cj(j) * tb - 1, 0), ob))
    nxt = pl.BlockSpec((nb, HALO, width), lambda j: (0, jnp.minimum((cj(j) + 1) * tb, length // HALO - 1), ob))
    return [main, prev, nxt]


def _run_mixer(body, pieces, t, rev, finish, ypart, states, params, name, save_width=0, out_dtype=BF16):
    nb, length = pieces[0][0].shape[0], pieces[0][0].shape[1]
    nchunks = length // t
    cj = (lambda j: nchunks - 1 - j) if rev else (lambda j: j)
    row_spec = lambda width: pl.BlockSpec((nb, t, width), lambda j: (0, cj(j), 0))
    in_specs, args = [], []
    for arr, width, off, halo in pieces:
        sp = _chunk_specs(nb, width, off, t, length, rev, nchunks, halo)
        in_specs += sp
        args += [arr] * len(sp)
    if finish:
        in_specs.append(row_spec(GROUP_W))
        args.append(ypart)
    for a in list(states) + list(params):
        in_specs.append(pl.BlockSpec(a.shape, lambda j, _nd=a.ndim: (0,) * _nd))
        args.append(a)
    out_shape = [jax.ShapeDtypeStruct((nb, length, GROUP_W), out_dtype if finish else F32)]
    out_specs = [row_spec(GROUP_W)]
    if save_width:
        out_shape.append(jax.ShapeDtypeStruct((nb, length, save_width), F32))
        out_specs.append(row_spec(save_width))
    for s in states:
        out_shape.append(jax.ShapeDtypeStruct(s.shape, s.dtype))
        out_specs.append(pl.BlockSpec(s.shape, lambda j, _nd=s.ndim: (0,) * _nd))
    res = pl.pallas_call(
        functools.partial(body, rev, finish, nchunks, nb),
        grid=(nchunks,),
        in_specs=in_specs,
        out_specs=out_specs,
        out_shape=out_shape,
        compiler_params=_cparams(("arbitrary",)),
        name=name,
    )(*args)
    nfix = 2 if save_width else 1
    return res[0], (res[1] if save_width else None), list(res[nfix:])


def _bidir_mixer(body, bwd_pieces, fwd_pieces, save_width, t, u_ctx, u_lat, zero_states, params, name,
                 lat_dtype=BF16):
    def run(u, rev, ypart, saved, states, tag, out_dtype=BF16):
        src = {"u": u, "saved": saved}
        pieces = [(src[s], w, o, h) for s, w, o, h in (bwd_pieces if rev else fwd_pieces)]
        return _run_mixer(body, pieces, t, rev, not rev, ypart, states, params, name + tag,
                          save_width if rev else 0, out_dtype)

    yb_c, sv_c, st_b = run(u_ctx, True, None, None, zero_states, "_ctx_bwd")
    y_c, _, st_f = run(u_ctx, False, yb_c, sv_c, zero_states, "_ctx_fwd")
    yb_l, sv_l, _ = run(u_lat, True, None, None, st_b, "_lat_bwd")
    y_l, _, _ = run(u_lat, False, yb_l, sv_l, st_f, "_lat_fwd", lat_dtype)
    return y_c, y_l


def _mods(cvec, w_mod, b_mod):
    depth, dm, nm = w_mod.shape
    tn = 1024
    return pl.pallas_call(
        _mod_kernel,
        grid=(depth, nm // tn),
        in_specs=[pl.BlockSpec((8, dm), lambda l, j: (0, 0)),
                  pl.BlockSpec((None, dm, tn), lambda l, j: (l, 0, j)),
                  pl.BlockSpec((None, 1, tn), lambda l, j: (l, 0, j))],
        out_specs=pl.BlockSpec((None, 8, tn), lambda l, j: (l, 0, j)),
        out_shape=jax.ShapeDtypeStruct((depth, 8, nm), F32),
        compiler_params=_cparams(("parallel", "parallel")),
        name="mods",
    )(cvec, w_mod, b_mod.reshape(depth, 1, nm))


def _inproj(x, g, shift, scale, w, layer, rows_per_group):
    m, dm = x.shape
    n = w.shape[2]
    tm = TM_PROJ
    grp = lambda i: (i * tm // rows_per_group, 0, 0)
    return pl.pallas_call(
        _inproj_kernel,
        grid=(m // tm,),
        in_specs=[pl.BlockSpec((tm, dm), lambda i: (i, 0)),
                  _resident((1, dm)),
                  pl.BlockSpec((1, 1, dm), grp),
                  pl.BlockSpec((1, 1, dm), grp),
                  _resident((dm, n), layer)],
        out_specs=pl.BlockSpec((tm, n), lambda i: (i, 0)),
        out_shape=jax.ShapeDtypeStruct((m, n), F32),
        compiler_params=_cparams(("parallel",)),
        name="inproj",
    )(x, g, shift, scale, w)


def _outproj(ys, w, layer, x, gate, g, shift, scale, rows_per_group, colmajor):
    m, dm = x.shape
    tm = min(TM_OUT, rows_per_group)
    grp = lambda i: (i * tm // rows_per_group, 0, 0)
    yspec = pl.BlockSpec((tm, GROUP_W), lambda i: (i, 0))
    if colmajor:
        assert tm == ROWS_PER_TILE * GRID_W
        tiles = rows_per_group // tm
        cspec = pl.BlockSpec((1, GRID_W, ROWS_PER_TILE, GROUP_W), lambda i: (i // tiles, 0, i % tiles, 0))
    else:
        cspec = yspec
    return pl.pallas_call(
        functools.partial(_outproj_kernel, colmajor),
        grid=(m // tm,),
        in_specs=[yspec, cspec, yspec, cspec,
                  _resident((4, GROUP_W, dm), layer),
                  pl.BlockSpec((tm, dm), lambda i: (i, 0)),
                  pl.BlockSpec((1, 1, dm), grp),
                  _resident((1, dm)),
                  pl.BlockSpec((1, 1, dm), grp),
                  pl.BlockSpec((1, 1, dm), grp)],
        out_specs=[pl.BlockSpec((tm, dm), lambda i: (i, 0)), pl.BlockSpec((tm, dm), lambda i: (i, 0))],
        out_shape=[jax.ShapeDtypeStruct((m, dm), F32), jax.ShapeDtypeStruct((m, dm), BF16)],
        compiler_params=_cparams(("parallel",)),
        name="outproj",
    )(*ys, w, x, gate, g, shift, scale)


def _inproj_colmajor(x, g, shift, scale, w, layer, bsz, rows):
    dm = x.shape[1]
    n = w.shape[2]
    tiles = GRID_W // COLS_PER_TILE
    tm = COLS_PER_TILE * rows
    grp = lambda i: (i // tiles, 0, 0)
    return pl.pallas_call(
        _inproj_colmajor_kernel,
        grid=(bsz * tiles,),
        in_specs=[pl.BlockSpec((1, rows, COLS_PER_TILE, dm), lambda i: (i // tiles, 0, i % tiles, 0)),
                  _resident((1, dm)),
                  pl.BlockSpec((1, 1, dm), grp),
                  pl.BlockSpec((1, 1, dm), grp),
                  _resident((dm, n), layer)],
        out_specs=pl.BlockSpec((tm, n), lambda i: (i, 0)),
        out_shape=jax.ShapeDtypeStruct((bsz * rows * GRID_W, n), F32),
        scratch_shapes=[pltpu.VMEM((tm, dm), F32)],
        compiler_params=_cparams(("parallel",)),
        name="inproj_colmajor",
    )(x.reshape(bsz, rows, GRID_W, dm), g, shift, scale, w)


def _ffn_up(h, wg, wu, layer):
    m, dm = h.shape
    dff = wg.shape[2]
    tm = min(TM_FFN_UP, m)
    tn = TN_FFN_UP
    return pl.pallas_call(
        _ffn_up_kernel,
        grid=(m // tm, dff // tn),
        in_specs=[pl.BlockSpec((tm, dm), lambda i, j: (i, 0)),
                  pl.BlockSpec((None, dm, tn), lambda i, j: (layer, 0, j)),
                  pl.BlockSpec((None, dm, tn), lambda i, j: (layer, 0, j))],
        out_specs=pl.BlockSpec((tm, tn), lambda i, j: (i, j)),
        out_shape=jax.ShapeDtypeStruct((m, dff), BF16),
        compiler_params=_cparams(("parallel", "arbitrary")),
        name="ffn_up",
    )(h, wg, wu)


def _ffn_down(a, w, layer, x, gate, final_g, final, rows_per_group):
    m, dm = x.shape
    dff = a.shape[1]
    tm = TM_FFN_DOWN
    grp = lambda i: (i * tm // rows_per_group, 0, 0)
    return pl.pallas_call(
        functools.partial(_ffn_down_kernel, final),
        grid=(m // tm,),
        in_specs=[pl.BlockSpec((tm, dff), lambda i: (i, 0)),
                  _resident((dff, dm), layer),
                  pl.BlockSpec((tm, dm), lambda i: (i, 0)),
                  pl.BlockSpec((1, 1, dm), grp),
                  _resident((1, dm))],
        out_specs=pl.BlockSpec((tm, dm), lambda i: (i, 0)),
        out_shape=jax.ShapeDtypeStruct((m, dm), F32),
        compiler_params=_cparams(("parallel",)),
        name="ffn_down",
    )(a, w, x, gate, final_g)


def _pad_cols(a, width):
    return jnp.pad(a, ((0, 0), (0, width - a.shape[1])))


def _split_w_in(w):
    ssd, ml, lru, gla = jnp.split(w, [1552, 1552 + 2064, 1552 + 2064 + 1024], axis=2)
    ssd_z, ssd_xbc, ssd_dt = ssd[..., :512], ssd[..., 512:1536], ssd[..., 1536:]
    ml_qkvo, ml_g = ml[..., :2048], ml[..., 2048:]
    lru_gate, lru_x = lru[..., :512], lru[..., 512:]
    gla_qkvr, gla_g1 = gla[..., :1536], gla[..., 1536:]
    pad = lambda a: jnp.pad(a, ((0, 0), (0, 0), (0, 128 - a.shape[2])))
    w_row = jnp.concatenate([ssd_xbc, lru_x, lru_gate, ssd_z, pad(ssd_dt)], axis=2)
    w_col = jnp.concatenate([ml_qkvo, gla_qkvr, pad(ml_g), pad(gla_g1)], axis=2)
    return w_row.astype(BF16), w_col.astype(BF16)


def _row128(a):
    return _pad_cols(a.reshape(1, -1).astype(F32), 128)


def _block_diag(w):
    nb, bi, bj = w.shape
    eye = jnp.eye(nb, dtype=w.dtype)
    return (eye[:, None, :, None] * w[:, :, None, :]).reshape(nb * bi, nb * bj)


def kernel(x, c, ctx, c_ctx, norm1_g, norm2_g, w_mod, b_mod, w_in, w_out, ssd_conv_w, ssd_conv_b, ssd_dt_bias, ssd_a_log, ssd_d, ssd_norm_g, ml_conv_w, ml_conv_b, ml_igate_b, ml_fgate_b, ml_norm_g, lru_conv_w, lru_conv_b, lru_wa, lru_ba, lru_wx, lru_bx, lru_lambda, gla_wg2, gla_bg, gla_norm_g, w_gate, w_up, w_down, final_g):
    bsz, length, dm = x.shape
    lctx = ctx.shape[1]
    depth = w_in.shape[0]
    rows = length // GRID_W
    assert dm == D_MODEL and length % SSD_CHUNK == 0 and lctx % SSD_CHUNK == 0
    assert rows == ML_CHUNK == GLA_CHUNK

    cvec = jnp.concatenate([c, c_ctx[None, :], jnp.zeros((8 - bsz - 1, dm), F32)], axis=0)
    mods = _mods(cvec, w_mod, b_mod)

    xl = x.reshape(bsz * length, dm)
    xt = ctx.reshape(bsz * lctx, dm)
    row = lambda a: a.reshape(1, -1).astype(F32)
    w_row, w_col = _split_w_in(w_in)
    w_out_b = w_out.astype(BF16).reshape(depth, 4, GROUP_W, dm)
    w_down_b = w_down.astype(BF16)

    for i in range(depth):
        need_ctx = i < depth - 1
        m_l = [mods[i, :bsz, k * dm:(k + 1) * dm].reshape(bsz, 1, dm) for k in range(N_MOD)]
        m_t = [mods[i, bsz:bsz + 1, k * dm:(k + 1) * dm].reshape(1, 1, dm) for k in range(N_MOD)]
        g1 = row(norm1_g[i])
        ul_row = _inproj(xl, g1, m_l[0], m_l[1], w_row, i, length).reshape(bsz, length, N_ROW)
        ul_col = _inproj_colmajor(xl, g1, m_l[0], m_l[1], w_col, i, bsz, rows).reshape(bsz, length, N_COL)
        ut_row = _inproj(xt, g1, m_t[0], m_t[1], w_row, i, bsz * lctx).reshape(bsz, lctx, N_ROW)
        ut_col = _inproj(xt, g1, m_t[0], m_t[1], w_col, i, bsz * lctx).reshape(bsz, lctx, N_COL)

        ssd_params = [ssd_conv_w[i], row(ssd_conv_b[i]), _row128(ssd_dt_bias[i]), _row128(ssd_a_log[i]),
                      row(jnp.repeat(ssd_d[i], GROUP_W // SSD_HEADS)), row(ssd_norm_g[i])]
        ssd_bwd = [("u", 1024, OFF_SSD_XBC, True), ("u", 128, OFF_SSD_DT, False)]
        ssd_fwd = [("saved", 1024, 0, False), ("u", 128, OFF_SSD_DT, False), ("u", 512, OFF_SSD_Z, False)]
        ssd_zero = [jnp.zeros((bsz, SSD_HEADS // 2, 128, 128), F32)]
        ya_t, ya_l = _bidir_mixer(_ssd_kernel, ssd_bwd, ssd_fwd, 1024, SSD_CHUNK, ut_row, ul_row, ssd_zero,
                                  ssd_params, "ssd")

        ml_gb = _row128(jnp.concatenate([ml_igate_b[i].reshape(-1), ml_fgate_b[i].reshape(-1)]))
        ml_params = [ml_conv_w[i], row(ml_conv_b[i]), ml_gb, row(ml_norm_g[i])]
        ml_bwd = [("u", 512, OFF_ML_Q, True), ("u", 512, OFF_ML_K, True), ("u", 512, OFF_ML_V, False),
                  ("u", 128, OFF_ML_G, False)]
        ml_fwd = [("saved", 1024, 0, False), ("u", 512, OFF_ML_V, False), ("u", 128, OFF_ML_G, False),
                  ("u", 512, OFF_ML_O, False)]
        ml_zero = [jnp.zeros((bsz, ML_HEADS, 128, 128), F32), jnp.zeros((bsz, 2 * ML_HEADS, 128), F32)]
        yb_t, yb_l = _bidir_mixer(_mlstm_kernel, ml_bwd, ml_fwd, 1024, ML_CHUNK, ut_col, ul_col, ml_zero,
                                  ml_params, "mlstm", F32)

        lru_w = jnp.stack([jnp.concatenate([_block_diag(lru_wa[i, dd]), _block_diag(lru_wx[i, dd])], axis=1)
                           for dd in range(2)]).astype(BF16)
        lru_bias = jnp.concatenate([lru_ba[i], lru_bx[i]], axis=1).reshape(2, 1, 2 * GROUP_W)
        lru_params = [lru_conv_w[i], row(lru_conv_b[i]), lru_w, lru_bias, lru_lambda[i].reshape(2, 1, GROUP_W)]
        lru_bwd = [("u", 512, OFF_LRU_X, True)]
        lru_fwd = [("saved", 512, 0, False), ("u", 512, OFF_LRU_GATE, False)]
        lru_zero = [jnp.zeros((bsz, 1, GROUP_W), F32)]
        yc_t, yc_l = _bidir_mixer(_lru_kernel, lru_bwd, lru_fwd, 512, LRU_CHUNK, ut_row, ul_row, lru_zero,
                                  lru_params, "lru")

        wg = jnp.zeros((2, 128, GLA_HEADS * GLA_DK), F32)
        wg = wg.at[0, :GLA_RANK].set(gla_wg2[i, 0]).at[1, GLA_RANK:2 * GLA_RANK].set(gla_wg2[i, 1])
        gla_params = [wg.astype(BF16), gla_bg[i].reshape(2, 1, GLA_HEADS * GLA_DK), row(gla_norm_g[i])]
        gla_bwd = [("u", 512, OFF_GLA_QK, False), ("u", 512, OFF_GLA_V, False), ("u", 128, OFF_GLA_G1, False)]
        gla_fwd = gla_bwd + [("u", 512, OFF_GLA_R, False)]
        gla_zero = [jnp.zeros((bsz, GROUP_W, GLA_HEADS * GLA_DK), F32)]
        yd_t, yd_l = _bidir_mixer(_gla_kernel, gla_bwd, gla_fwd, 0, GLA_CHUNK, ut_col, ul_col, gla_zero,
                                  gla_params, "gla", F32)

        fg = row(final_g)

        def tail(xs, ys, m, rows_per_group, final, colmajor):
            cm = lambda y: y.reshape(bsz, GRID_W, rows, GROUP_W) if colmajor else y.reshape(-1, GROUP_W)
            ys = [ys[0].reshape(-1, GROUP_W), cm(ys[1]), ys[2].reshape(-1, GROUP_W), cm(ys[3])]
            xs, h2 = _outproj(ys, w_out_b, i, xs, m[2], row(norm2_g[i]), m[3], m[4], rows_per_group, colmajor)
            act = _ffn_up(h2, w_gate, w_up, i)
            return _ffn_down(act, w_down_b, i, xs, m[5], fg, final, rows_per_group)

        xl = tail(xl, [ya_l, yb_l, yc_l, yd_l], m_l, length, i == depth - 1, True)
        if need_ctx:
            xt = tail(xt, [ya_t, yb_t, yc_t, yd_t], m_t, bsz * lctx, False, False)
    return xl.reshape(bsz, length, dm)
```

```python
import functools

import jax
import jax.numpy as jnp
from jax import lax
from jax.experimental import pallas as pl
from jax.experimental.pallas import tpu as pltpu

F32 = jnp.float32
BF16 = jnp.bfloat16
EPS = 1e-6
NEG = -1e30

D_MODEL = 2048
GRID_W = 64
GROUP_W = 512
N_MOD = 6
CONV_W = 5
HALO = 8

SSD_HEADS = 8
SSD_CHUNK = 128
ML_HEADS = 4
ML_CHUNK = 64
ML_HEAD_GROUP = 4
LRU_C = 8.0
LRU_CHUNK = 256
LRU_GROUP = 64
GLA_HEADS = 4
GLA_DK = 64
GLA_RANK = 16
GLA_TAU = 16.0
GLA_CHUNK = 64
GLA_SUB = 16

OFF_SSD_XBC = 0
OFF_LRU_X = 1024
OFF_LRU_GATE = 1536
OFF_SSD_Z = 2048
OFF_SSD_DT = 2560
N_ROW = 2688
OFF_ML_Q = 0
OFF_ML_K = 512
OFF_ML_V = 1024
OFF_ML_O = 1536
OFF_GLA_QK = 2048
OFF_GLA_V = 2560
OFF_GLA_R = 3072
OFF_ML_G = 3584
OFF_GLA_G1 = 3712
N_COL = 3840

VMEM_LIMIT = 56 * 1024 * 1024
TM_PROJ = 256
COLS_PER_TILE = 8
ROWS_PER_TILE = 8
TM_OUT = 512
TM_FFN_UP = 2048
TN_FFN_UP = 512
TM_FFN_DOWN = 256


def _cparams(sem):
    return pltpu.CompilerParams(dimension_semantics=sem, vmem_limit_bytes=VMEM_LIMIT)


def _resident(shape, layer=None):
    if layer is None:
        nd = len(shape)
        return pl.BlockSpec(shape, lambda *_: (0,) * nd, pipeline_mode=pl.Buffered(1))
    nd = len(shape)
    return pl.BlockSpec((None,) + tuple(shape), lambda *_: (layer,) + (0,) * nd, pipeline_mode=pl.Buffered(1))


def _sigmoid(x):
    return 0.5 + 0.5 * jnp.tanh(0.5 * x)


def _silu(x):
    return x * _sigmoid(x)


def _softplus(x):
    return jnp.maximum(x, 0.0) + jnp.log(1.0 + jnp.exp(-jnp.abs(x)))


def _log_sigmoid(x):
    return jnp.minimum(x, 0.0) - jnp.log(1.0 + jnp.exp(-jnp.abs(x)))


def _gelu_tanh(x):
    return 0.5 * x * (1.0 + jnp.tanh(0.7978845608028654 * (x + 0.044715 * (x * x * x))))


def _rms(x):
    return x * lax.rsqrt(jnp.mean(x * x, axis=-1, keepdims=True) + EPS)


def _dot(a, b):
    return jnp.dot(a, b, preferred_element_type=F32)


def _dot_nt(a, b):
    return lax.dot_general(a, b, (((1,), (1,)), ((), ())), preferred_element_type=F32)


def _dot_exact(a, b):
    return jnp.dot(a, b, precision=lax.Precision.HIGHEST, preferred_element_type=F32)


def _scan_mask(t, rev):
    ri = lax.broadcasted_iota(jnp.int32, (t, t), 0)
    ci = lax.broadcasted_iota(jnp.int32, (t, t), 1)
    return (ci >= ri) if rev else (ci <= ri)


def _conv5(prev, main, nxt, w, bias, first, last):
    t = main.shape[0]
    prev = jnp.where(first, 0.0, prev)
    nxt = jnp.where(last, 0.0, nxt)
    ext = jnp.concatenate([prev, main, nxt], axis=0)
    n = t + 2 * HALO
    acc = None
    for k in range(CONV_W):
        sh = (CONV_W // 2 - k) % n
        r = ext if sh == 0 else pltpu.roll(ext, sh, axis=0)
        term = r[HALO:HALO + t] * w[k:k + 1, :]
        acc = term if acc is None else acc + term
    return acc + bias


def _mod_kernel(c_ref, w_ref, b_ref, o_ref):
    s = _silu(c_ref[...]).astype(BF16)
    o_ref[...] = _dot(s, w_ref[...].astype(BF16)) + b_ref[...]


def _norm_mod(x, g_ref, sh_ref, sc_ref):
    return ((_rms(x) * g_ref[...]) * (1.0 + sc_ref[0]) + sh_ref[0]).astype(BF16)


def _inproj_kernel(x_ref, g_ref, sh_ref, sc_ref, w_ref, o_ref):
    o_ref[...] = _dot(_norm_mod(x_ref[...], g_ref, sh_ref, sc_ref), w_ref[...])


def _inproj_colmajor_kernel(x_ref, g_ref, sh_ref, sc_ref, w_ref, o_ref, xs_ref):
    rows = x_ref.shape[1]
    for wl in range(COLS_PER_TILE):
        xs_ref[wl * rows:(wl + 1) * rows, :] = x_ref[0, :, wl, :]
    o_ref[...] = _dot(_norm_mod(xs_ref[...], g_ref, sh_ref, sc_ref), w_ref[...])


def _outproj_kernel(colmajor, ya_ref, yb_ref, yc_ref, yd_ref, w_ref, x_ref, gate_ref, g_ref, sh_ref, sc_ref,
                    xo_ref, h_ref):
    if colmajor:
        def rowmajor(ref):
            return jnp.concatenate([ref[0, :, rl, :] for rl in range(ROWS_PER_TILE)], axis=0).astype(BF16)
        yb, yd = rowmajor(yb_ref), rowmajor(yd_ref)
    else:
        yb, yd = yb_ref[...], yd_ref[...]
    acc = _dot(ya_ref[...], w_ref[0])
    acc = acc + _dot(yc_ref[...], w_ref[2])
    acc = acc + _dot(yb, w_ref[1])
    acc = acc + _dot(yd, w_ref[3])
    xn = x_ref[...] + gate_ref[0] * acc
    xo_ref[...] = xn
    y = _rms(xn) * g_ref[...]
    h_ref[...] = (y * (1.0 + sc_ref[0]) + sh_ref[0]).astype(BF16)


def _ffn_up_kernel(h_ref, wg_ref, wu_ref, o_ref):
    h = h_ref[...]
    g = _dot(h, wg_ref[...].astype(BF16))
    u = _dot(h, wu_ref[...].astype(BF16))
    o_ref[...] = (_silu(g) * u).astype(BF16)


def _ffn_down_kernel(final, a_ref, w_ref, x_ref, gate_ref, fg_ref, o_ref):
    xn = x_ref[...] + gate_ref[0] * _dot(a_ref[...], w_ref[...])
    if final:
        xn = _rms(xn) * fg_ref[...]
    o_ref[...] = xn


def _ssd_kernel(rev, finish, nchunks, nb, *refs):
    if finish:
        (xc_ref, dt_ref, z_ref, yp_ref, s0_ref, cw_ref, cb_ref, dtb_ref, alog_ref,
         dsk_ref, ng_ref, y_ref, s_ref) = refs
    else:
        (xbc_ref, xp_ref, xn_ref, dt_ref, s0_ref, cw_ref, cb_ref, dtb_ref, alog_ref,
         dsk_ref, ng_ref, y_ref, xc_ref, s_ref) = refs
    t = SSD_CHUNK
    d = 1 if rev else 0
    j = pl.program_id(0)
    c = (nchunks - 1 - j) if rev else j

    @pl.when(j == 0)
    def _():
        s_ref[...] = s0_ref[...]

    mask = _scan_mask(t, rev)
    maskf = mask.astype(F32)
    er = 0 if rev else t - 1
    lo = lax.broadcasted_iota(jnp.int32, (t, 128), 1) < 64
    lo_rows = lax.broadcasted_iota(jnp.int32, (128, 128), 0) < 64
    cw, cbias = cw_ref[...], cb_ref[...]
    nega = -jnp.exp(alog_ref[...])

    bs = range(nb)
    if finish:
        xbc = [xc_ref[bi] for bi in bs]
    else:
        xbc = [_silu(_conv5(xp_ref[bi], xbc_ref[bi], xn_ref[bi], cw, cbias, c == 0, c == nchunks - 1))
               for bi in bs]
        for bi in bs:
            xc_ref[bi] = xbc[bi]
    dt_all = [_softplus(dt_ref[bi] + dtb_ref[...]) for bi in bs]
    b_all = [_dot_exact(maskf, dt_all[bi] * nega) for bi in bs]
    b_t = [b_all[bi].T for bi in bs]
    dt_t = [dt_all[bi].T for bi in bs]
    ys = [[None] * (SSD_HEADS // 2) for _ in bs]
    for p in range(SSD_HEADS // 2):
        g = p // 2
        bm = [xbc[bi][:, 512 + 128 * g:640 + 128 * g].astype(BF16) for bi in bs]
        cm = [xbc[bi][:, 768 + 128 * g:896 + 128 * g].astype(BF16) for bi in bs]
        cb = [_dot_nt(cm[bi], bm[bi]) for bi in bs]
        xpair = [xbc[bi][:, 128 * p:128 * p + 128] for bi in bs]
        xpair_b = [xpair[bi].astype(BF16) for bi in bs]
        s_pair = [s_ref[bi, p] for bi in bs]
        y_inter = [_dot_nt(cm[bi], s_pair[bi].astype(BF16)) for bi in bs]
        yh, eb, coef, dec = [], [], [], []
        for hh in range(2):
            l = d * SSD_HEADS + 2 * p + hh
            bcol = [b_all[bi][:, l:l + 1] for bi in bs]
            seg = [jnp.exp(jnp.where(mask, bcol[bi] - b_t[bi][l:l + 1, :], NEG)) for bi in bs]
            w = [(cb[bi] * seg[bi] * dt_t[bi][l:l + 1, :]).astype(BF16) for bi in bs]
            yh.append([_dot(w[bi], xpair_b[bi]) for bi in bs])
            eb.append([jnp.exp(bcol[bi]) for bi in bs])
            bl = [b_all[bi][er:er + 1, l:l + 1] for bi in bs]
            coef.append([jnp.exp(bl[bi] - bcol[bi]) * dt_all[bi][:, l:l + 1] for bi in bs])
            dec.append([jnp.exp(bl[bi]) for bi in bs])
        xw = [(xpair[bi] * jnp.where(lo, coef[0][bi], coef[1][bi])).T.astype(BF16) for bi in bs]
        upd = [_dot(xw[bi], bm[bi]) for bi in bs]
        for bi in bs:
            ys[bi][p] = (jnp.where(lo, yh[0][bi], yh[1][bi])
                         + y_inter[bi] * jnp.where(lo, eb[0][bi], eb[1][bi]))
            s_ref[bi, p] = jnp.where(lo_rows, dec[0][bi], dec[1][bi]) * s_pair[bi] + upd[bi]
    for bi in bs:
        y = jnp.concatenate(ys[bi], axis=1)
        if finish:
            y = y + yp_ref[bi] + dsk_ref[...] * xbc[bi][:, :GROUP_W]
            y = _rms(y * _silu(z_ref[bi])) * ng_ref[...]
            y_ref[bi] = y.astype(y_ref.dtype)
        else:
            y_ref[bi] = y


def _mlstm_kernel(rev, finish, nchunks, nb, *refs):
    if finish:
        (qkc_ref, v_ref, g_ref, o_ref, yp_ref, cs0_ref, sm0_ref,
         cw_ref, cb_ref, gb_ref, ng_ref, y_ref, cs_ref, sm_ref) = refs
    else:
        (q_ref, qp_ref, qn_ref, k_ref, kp_ref, kn_ref, v_ref, g_ref, cs0_ref, sm0_ref,
         cw_ref, cb_ref, gb_ref, ng_ref, y_ref, qkc_ref, cs_ref, sm_ref) = refs
    t = ML_CHUNK
    d = 1 if rev else 0
    j = pl.program_id(0)
    c = (nchunks - 1 - j) if rev else j
    first, last = c == 0, c == nchunks - 1

    @pl.when(j == 0)
    def _():
        cs_ref[...] = cs0_ref[...]
        sm_ref[...] = sm0_ref[...]

    cw = cw_ref[...]
    cbias = cb_ref[...]
    maskf = _scan_mask(t, rev).astype(F32)
    mask_t = _scan_mask(t, not rev)
    er = 0 if rev else t - 1

    bs = range(nb)
    if finish:
        q = [qkc_ref[bi][:, :GROUP_W] for bi in bs]
        k = [qkc_ref[bi][:, GROUP_W:] for bi in bs]
    else:
        q = [_silu(_conv5(qp_ref[bi], q_ref[bi], qn_ref[bi], cw[:, :GROUP_W], cbias[:, :GROUP_W], first, last))
             for bi in bs]
        k = [_silu(_conv5(kp_ref[bi], k_ref[bi], kn_ref[bi], cw[:, GROUP_W:], cbias[:, GROUP_W:], first, last))
             * (128.0 ** -0.5) for bi in bs]
        for bi in bs:
            qkc_ref[bi, :, :GROUP_W] = q[bi]
            qkc_ref[bi, :, GROUP_W:] = k[bi]
    v = [v_ref[bi] for bi in bs]
    gts = [g_ref[bi] + gb_ref[...] for bi in bs]
    b_all = [_dot_exact(maskf, _log_sigmoid(gts[bi])) for bi in bs]
    g_t = [gts[bi].T for bi in bs]
    b_t = [b_all[bi].T for bi in bs]
    ys = [[None] * ML_HEADS for _ in bs]
    for h0 in range(0, ML_HEADS, ML_HEAD_GROUP):
        ch = [(bi, h) for h in range(h0, h0 + ML_HEAD_GROUP) for bi in bs]
        cs_ = range(len(ch))
        li = [d * ML_HEADS + h for _, h in ch]
        lf = [2 * ML_HEADS + l for l in li]
        sl = [slice(128 * h, 128 * h + 128) for _, h in ch]
        brow = [b_t[bi][lf[c]:lf[c] + 1, :] for c, (bi, h) in enumerate(ch)]
        igrow = [g_t[bi][li[c]:li[c] + 1, :] for c, (bi, h) in enumerate(ch)]
        ccol = [gts[bi][:, li[c]:li[c] + 1] - b_all[bi][:, lf[c]:lf[c] + 1]
                for c, (bi, h) in enumerate(ch)]
        m_prev = [sm_ref[bi, ML_HEADS + h:ML_HEADS + h + 1, 0:1] for bi, h in ch]
        ns = [sm_ref[bi, h:h + 1, :] for bi, h in ch]
        cs = [cs_ref[bi, h] for bi, h in ch]
        dmat = [jnp.where(mask_t, brow[c] + ccol[c], NEG) for c in cs_]
        inter = [brow[c] + m_prev[c] for c in cs_]
        mt = [jnp.maximum(inter[c], jnp.max(dmat[c], axis=0, keepdims=True)) for c in cs_]
        w = [jnp.exp(dmat[c] - mt[c]) for c in cs_]
        sc = [jnp.exp(inter[c] - mt[c]) for c in cs_]
        qb = [q[bi][:, sl[c]].astype(BF16) for c, (bi, h) in enumerate(ch)]
        kb = [k[bi][:, sl[c]].astype(BF16) for c, (bi, h) in enumerate(ch)]
        sw = [_dot_nt(kb[c], qb[c]) * w[c] for c in cs_]
        v_t = [v[bi][:, sl[c]].T for c, (bi, h) in enumerate(ch)]
        num = [_dot(v_t[c].astype(BF16), sw[c].astype(BF16)) for c in cs_]
        qc = [_dot_nt(cs[c].astype(BF16), qb[c]) for c in cs_]
        qn = [_dot_nt(jnp.broadcast_to(ns[c], (8, 128)).astype(BF16), qb[c])[0:1] for c in cs_]
        den = [jnp.sum(sw[c], axis=0, keepdims=True) + sc[c] * qn[c] for c in cs_]
        for c, (bi, h) in enumerate(ch):
            inv = 1.0 / jnp.maximum(jnp.abs(den[c]), jnp.exp(-mt[c]))
            ys[bi][h] = ((num[c] + sc[c] * qc[c]) * inv).T
        bl = [b_all[bi][er:er + 1, lf[c]:lf[c] + 1] for c, (bi, h) in enumerate(ch)]
        tail = [bl[c] - brow[c] + igrow[c] for c in cs_]
        m_new = [jnp.maximum(bl[c] + m_prev[c], jnp.max(tail[c], axis=1, keepdims=True)) for c in cs_]
        ws = [jnp.exp(tail[c] - m_new[c]) for c in cs_]
        sc_end = [jnp.exp(bl[c] + m_prev[c] - m_new[c]) for c in cs_]
        upd = [_dot((v_t[c] * ws[c]).astype(BF16), kb[c]) for c in cs_]
        nup = [_dot(jnp.broadcast_to(ws[c], (8, t)).astype(BF16), kb[c])[0:1] for c in cs_]
        for c, (bi, h) in enumerate(ch):
            cs_ref[bi, h] = sc_end[c] * cs[c] + upd[c]
            sm_ref[bi, h:h + 1, :] = sc_end[c] * ns[c] + nup[c]
            sm_ref[bi, ML_HEADS + h:ML_HEADS + h + 1, :] = jnp.broadcast_to(m_new[c], (1, 128))
    for bi in bs:
        if finish:
            yp = yp_ref[bi]
            outs = [_rms(ys[bi][h] + yp[:, 128 * h:128 * h + 128]) for h in range(ML_HEADS)]
            y = jnp.concatenate(outs, axis=1) * ng_ref[...]
            y_ref[bi] = (_sigmoid(o_ref[bi]) * y).astype(y_ref.dtype)
        else:
            y_ref[bi] = jnp.concatenate(ys[bi], axis=1)


def _lru_kernel(rev, finish, nchunks, nb, *refs):
    if finish:
        (xf_ref, gate_ref, yp_ref, h0_ref, cw_ref, cb_ref, w_ref, bias_ref, lam_ref,
         y_ref, h_ref, sa_ref, sh_ref) = refs
    else:
        (x_ref, xp_ref, xn_ref, h0_ref, cw_ref, cb_ref, w_ref, bias_ref, lam_ref,
         y_ref, xf_ref, h_ref, sa_ref, sh_ref) = refs
    t = LRU_CHUNK
    d = 1 if rev else 0
    j = pl.program_id(0)
    c = (nchunks - 1 - j) if rev else j

    @pl.when(j == 0)
    def _():
        h_ref[...] = h0_ref[...]

    cw, cbias = cw_ref[...], cb_ref[...]
    lsl = LRU_C * _log_sigmoid(lam_ref[d])
    sub = lax.broadcasted_iota(jnp.int32, (8, 128), 0)
    ng = GROUP_W // 128

    for bi in range(nb):
        if finish:
            xf = xf_ref[bi]
        else:
            xf = _conv5(xp_ref[bi], x_ref[bi], xn_ref[bi], cw, cbias, c == 0, c == nchunks - 1)
            xf_ref[bi] = xf
        pre = _dot(xf.astype(BF16), w_ref[d]) + bias_ref[d]
        r = _sigmoid(pre[:, :GROUP_W])
        ig = _sigmoid(pre[:, GROUP_W:])
        loga = r * lsl
        a = jnp.exp(loga)
        bx = jnp.sqrt(-jnp.tanh(loga) * (1.0 + a * a)) * ig * xf
        for g in range(ng):
            sa_ref[bi, g] = a[:, 128 * g:128 * g + 128]
            sh_ref[bi, g] = bx[:, 128 * g:128 * g + 128]

    chains = [(bi, g) for bi in range(nb) for g in range(ng)]
    carry = [h_ref[bi, :, 128 * g:128 * g + 128] for bi, g in chains]
    for grp in (range(t // LRU_GROUP - 1, -1, -1) if rev else range(t // LRU_GROUP)):
        base = LRU_GROUP * grp
        a_run, h_run = [None] * len(chains), [None] * len(chains)
        a_cum = [[None] * 8 for _ in chains]
        h_loc = [[None] * 8 for _ in chains]
        for v in (range(7, -1, -1) if rev else range(8)):
            for ci, (bi, g) in enumerate(chains):
                a_v = sa_ref[bi, g, pl.ds(base + v, 8, stride=8), :]
                b_v = sh_ref[bi, g, pl.ds(base + v, 8, stride=8), :]
                if a_run[ci] is None:
                    a_run[ci], h_run[ci] = a_v, b_v
                else:
                    h_run[ci] = a_v * h_run[ci] + b_v
                    a_run[ci] = a_v * a_run[ci]
                a_cum[ci][v], h_loc[ci][v] = a_run[ci], h_run[ci]
        for ci, (bi, g) in enumerate(chains):
            pa, ph = a_run[ci], h_run[ci]
            for sh in (1, 2, 4):
                if rev:
                    valid = sub < 8 - sh
                    a_s = pltpu.roll(pa, 8 - sh, axis=0)
                    h_s = pltpu.roll(ph, 8 - sh, axis=0)
                else:
                    valid = sub >= sh
                    a_s = pltpu.roll(pa, sh, axis=0)
                    h_s = pltpu.roll(ph, sh, axis=0)
                ph = jnp.where(valid, pa * h_s + ph, ph)
                pa = jnp.where(valid, pa * a_s, pa)
            after = pa * carry[ci] + ph
            if rev:
                cin = jnp.where(sub == 7, carry[ci], pltpu.roll(after, 7, axis=0))
                carry[ci] = after[0:1]
            else:
                cin = jnp.where(sub == 0, carry[ci], pltpu.roll(after, 1, axis=0))
                carry[ci] = after[7:8]
            for v in range(8):
                sh_ref[bi, g, pl.ds(base + v, 8, stride=8), :] = h_loc[ci][v] + a_cum[ci][v] * cin
    for ci, (bi, g) in enumerate(chains):
        h_ref[bi, :, 128 * g:128 * g + 128] = carry[ci]
    for bi in range(nb):
        hs = jnp.concatenate([sh_ref[bi, g] for g in range(ng)], axis=1)
        if finish:
            y_ref[bi] = ((hs + yp_ref[bi]) * _gelu_tanh(gate_ref[bi])).astype(y_ref.dtype)
        else:
            y_ref[bi] = hs


def _gla_kernel(rev, finish, nchunks, nb, *refs):
    if finish:
        (qk_ref, v_ref, g1_ref, r_ref, yp_ref, s0_ref, wg_ref, bg_ref, ng_ref, y_ref, s_ref) = refs
    else:
        (qk_ref, v_ref, g1_ref, s0_ref, wg_ref, bg_ref, ng_ref, y_ref, s_ref) = refs
    t = GLA_CHUNK
    d = 1 if rev else 0
    dkk = GLA_HEADS * GLA_DK
    j = pl.program_id(0)

    @pl.when(j == 0)
    def _():
        s_ref[...] = s0_ref[...]

    mask = _scan_mask(t, rev)
    maskf = mask.astype(F32)
    rows = lax.broadcasted_iota(jnp.int32, (t, 1), 0)
    lane_head = lax.broadcasted_iota(jnp.int32, (GLA_SUB, dkk), 1) // GLA_DK
    blk = (lax.broadcasted_iota(jnp.int32, (GROUP_W, dkk), 0) // 128
           == lax.broadcasted_iota(jnp.int32, (GROUP_W, dkk), 1) // GLA_DK)
    er = 0 if rev else t - 1

    bs = range(nb)
    q = [qk_ref[bi][:, :dkk] * (GLA_DK ** -0.5) for bi in bs]
    k = [qk_ref[bi][:, dkk:] for bi in bs]
    v = [v_ref[bi] for bi in bs]
    vb = [v[bi].astype(BF16) for bi in bs]
    glog = [_dot(g1_ref[bi].astype(BF16), wg_ref[d]) + bg_ref[d] for bi in bs]
    la = [_log_sigmoid(glog[bi]) * (1.0 / GLA_TAU) for bi in bs]
    b = [_dot_exact(maskf, la[bi]) for bi in bs]
    excl = [b[bi] - la[bi] for bi in bs]
    s_t = [s_ref[bi] for bi in bs]
    o_inter = [_dot_nt((q[bi] * jnp.exp(b[bi])).astype(BF16), s_t[bi].astype(BF16)) for bi in bs]

    att_blocks = [[None] * (t // GLA_SUB) for _ in bs]
    for i in range(t // GLA_SUB):
        r0 = GLA_SUB * i
        if rev:
            ref_row = [excl[bi][r0 + GLA_SUB - 1:r0 + GLA_SUB] for bi in bs]
            kvalid = rows >= r0
        else:
            ref_row = [excl[bi][r0:r0 + 1] for bi in bs]
            kvalid = rows < r0 + GLA_SUB
        qi = [q[bi][r0:r0 + GLA_SUB] * jnp.exp(b[bi][r0:r0 + GLA_SUB] - ref_row[bi]) for bi in bs]
        ki = [(k[bi] * jnp.exp(jnp.where(kvalid, ref_row[bi] - b[bi], NEG))).astype(BF16) for bi in bs]
        qs = [jnp.concatenate([jnp.where(lane_head == h, qi[bi], 0.0) for h in range(GLA_HEADS)],
                              axis=0).astype(BF16) for bi in bs]
        for bi in bs:
            att_blocks[bi][i] = _dot_nt(qs[bi], ki[bi])
    outs = [[None] * GLA_HEADS for _ in bs]
    for h in range(GLA_HEADS):
        att = [jnp.where(mask, jnp.concatenate([ab[GLA_SUB * h:GLA_SUB * h + GLA_SUB] for ab in att_blocks[bi]],
                                               axis=0), 0.0).astype(BF16) for bi in bs]
        for bi in bs:
            outs[bi][h] = _dot(att[bi], vb[bi][:, 128 * h:128 * h + 128])

    bl = [b[bi][er:er + 1] for bi in bs]
    kd = [(k[bi] * jnp.exp(bl[bi] - b[bi])).astype(BF16) for bi in bs]
    v_t = [jnp.concatenate([v[bi][:, 128 * h:128 * h + 128].T for h in range(GLA_HEADS)], axis=0).astype(BF16)
           for bi in bs]
    upd = [_dot(v_t[bi], kd[bi]) for bi in bs]
    for bi in bs:
        s_ref[bi] = s_t[bi] * jnp.exp(bl[bi]) + jnp.where(blk, upd[bi], 0.0)
    for bi in bs:
        o = jnp.concatenate(outs[bi], axis=1) + o_inter[bi]
        if finish:
            o = o + yp_ref[bi]
            o = jnp.concatenate([_rms(o[:, 128 * h:128 * h + 128]) for h in range(GLA_HEADS)], axis=1)
            y_ref[bi] = (o * ng_ref[...] * _silu(r_ref[bi])).astype(y_ref.dtype)
        else:
            y_ref[bi] = o


def _chunk_specs(nb, width, off, t, length, rev, nchunks, halo):
    ob = off // width
    tb = t // HALO

    def cj(j):
        return (nchunks - 1 - j) if rev else j

    main = pl.BlockSpec((nb, t, width), lambda j: (0, cj(j), ob))
    if not halo:
        return [main]
    prev = pl.BlockSpec((nb, HALO, width), lambda j: (0, jnp.maximum(cj(j) * tb - 1, 0), ob))
    nxt = pl.BlockSpec((nb, HALO, width), lambda j: (0, jnp.minimum((cj(j) + 1) * tb, length // HALO - 1), ob))
    return [main, prev, nxt]


def _run_mixer(body, pieces, t, rev, finish, ypart, states, params, name, save_width=0, out_dtype=BF16,
               scratch=()):
    nb, length = pieces[0][0].shape[0], pieces[0][0].shape[1]
    nchunks = length // t
    cj = (lambda j: nchunks - 1 - j) if rev else (lambda j: j)
    row_spec = lambda width: pl.BlockSpec((nb, t, width), lambda j: (0, cj(j), 0))
    in_specs, args = [], []
    for arr, width, off, halo in pieces:
        sp = _chunk_specs(nb, width, off, t, length, rev, nchunks, halo)
        in_specs += sp
        args += [arr] * len(sp)
    if finish:
        in_specs.append(row_spec(GROUP_W))
        args.append(ypart)
    for a in list(states) + list(params):
        in_specs.append(pl.BlockSpec(a.shape, lambda j, _nd=a.ndim: (0,) * _nd))
        args.append(a)
    out_shape = [jax.ShapeDtypeStruct((nb, length, GROUP_W), out_dtype if finish else F32)]
    out_specs = [row_spec(GROUP_W)]
    if save_width:
        out_shape.append(jax.ShapeDtypeStruct((nb, length, save_width), F32))
        out_specs.append(row_spec(save_width))
    for s in states:
        out_shape.append(jax.ShapeDtypeStruct(s.shape, s.dtype))
        out_specs.append(pl.BlockSpec(s.shape, lambda j, _nd=s.ndim: (0,) * _nd))
    res = pl.pallas_call(
        functools.partial(body, rev, finish, nchunks, nb),
        grid=(nchunks,),
        in_specs=in_specs,
        out_specs=out_specs,
        out_shape=out_shape,
        scratch_shapes=list(scratch),
        compiler_params=_cparams(("arbitrary",)),
        name=name,
    )(*args)
    nfix = 2 if save_width else 1
    return res[0], (res[1] if save_width else None), list(res[nfix:])


def _bidir_mixer(body, bwd_pieces, fwd_pieces, save_width, t, u_ctx, u_lat, zero_states, params, name,
                 lat_dtype=BF16, scratch=()):
    def run(u, rev, ypart, saved, states, tag, out_dtype=BF16):
        src = {"u": u, "saved": saved}
        pieces = [(src[s], w, o, h) for s, w, o, h in (bwd_pieces if rev else fwd_pieces)]
        return _run_mixer(body, pieces, t, rev, not rev, ypart, states, params, name + tag,
                          save_width if rev else 0, out_dtype, scratch)

    yb_c, sv_c, st_b = run(u_ctx, True, None, None, zero_states, "_ctx_bwd")
    y_c, _, st_f = run(u_ctx, False, yb_c, sv_c, zero_states, "_ctx_fwd")
    yb_l, sv_l, _ = run(u_lat, True, None, None, st_b, "_lat_bwd")
    y_l, _, _ = run(u_lat, False, yb_l, sv_l, st_f, "_lat_fwd", lat_dtype)
    return y_c, y_l


def _mods(cvec, w_mod, b_mod):
    depth, dm, nm = w_mod.shape
    tn = 1024
    return pl.pallas_call(
        _mod_kernel,
        grid=(depth, nm // tn),
        in_specs=[pl.BlockSpec((8, dm), lambda l, j: (0, 0)),
                  pl.BlockSpec((None, dm, tn), lambda l, j: (l, 0, j)),
                  pl.BlockSpec((None, 1, tn), lambda l, j: (l, 0, j))],
        out_specs=pl.BlockSpec((None, 8, tn), lambda l, j: (l, 0, j)),
        out_shape=jax.ShapeDtypeStruct((depth, 8, nm), F32),
        compiler_params=_cparams(("parallel", "parallel")),
        name="mods",
    )(cvec, w_mod, b_mod.reshape(depth, 1, nm))


def _inproj(x, g, shift, scale, w, layer, rows_per_group):
    m, dm = x.shape
    n = w.shape[2]
    tm = TM_PROJ
    grp = lambda i: (i * tm // rows_per_group, 0, 0)
    return pl.pallas_call(
        _inproj_kernel,
        grid=(m // tm,),
        in_specs=[pl.BlockSpec((tm, dm), lambda i: (i, 0)),
                  _resident((1, dm)),
                  pl.BlockSpec((1, 1, dm), grp),
                  pl.BlockSpec((1, 1, dm), grp),
                  _resident((dm, n), layer)],
        out_specs=pl.BlockSpec((tm, n), lambda i: (i, 0)),
        out_shape=jax.ShapeDtypeStruct((m, n), F32),
        compiler_params=_cparams(("parallel",)),
        name="inproj",
    )(x, g, shift, scale, w)


def _outproj(ys, w, layer, x, gate, g, shift, scale, rows_per_group, colmajor):
    m, dm = x.shape
    tm = min(TM_OUT, rows_per_group)
    grp = lambda i: (i * tm // rows_per_group, 0, 0)
    yspec = pl.BlockSpec((tm, GROUP_W), lambda i: (i, 0))
    if colmajor:
        assert tm == ROWS_PER_TILE * GRID_W
        tiles = rows_per_group // tm
        cspec = pl.BlockSpec((1, GRID_W, ROWS_PER_TILE, GROUP_W), lambda i: (i // tiles, 0, i % tiles, 0))
    else:
        cspec = yspec
    return pl.pallas_call(
        functools.partial(_outproj_kernel, colmajor),
        grid=(m // tm,),
        in_specs=[yspec, cspec, yspec, cspec,
                  _resident((4, GROUP_W, dm), layer),
                  pl.BlockSpec((tm, dm), lambda i: (i, 0)),
                  pl.BlockSpec((1, 1, dm), grp),
                  _resident((1, dm)),
                  pl.BlockSpec((1, 1, dm), grp),
                  pl.BlockSpec((1, 1, dm), grp)],
        out_specs=[pl.BlockSpec((tm, dm), lambda i: (i, 0)), pl.BlockSpec((tm, dm), lambda i: (i, 0))],
        out_shape=[jax.ShapeDtypeStruct((m, dm), F32), jax.ShapeDtypeStruct((m, dm), BF16)],
        compiler_params=_cparams(("parallel",)),
        name="outproj",
    )(*ys, w, x, gate, g, shift, scale)


def _inproj_colmajor(x, g, shift, scale, w, layer, bsz, rows):
    dm = x.shape[1]
    n = w.shape[2]
    tiles = GRID_W // COLS_PER_TILE
    tm = COLS_PER_TILE * rows
    grp = lambda i: (i // tiles, 0, 0)
    return pl.pallas_call(
        _inproj_colmajor_kernel,
        grid=(bsz * tiles,),
        in_specs=[pl.BlockSpec((1, rows, COLS_PER_TILE, dm), lambda i: (i // tiles, 0, i % tiles, 0)),
                  _resident((1, dm)),
                  pl.BlockSpec((1, 1, dm), grp),
                  pl.BlockSpec((1, 1, dm), grp),
                  _resident((dm, n), layer)],
        out_specs=pl.BlockSpec((tm, n), lambda i: (i, 0)),
        out_shape=jax.ShapeDtypeStruct((bsz * rows * GRID_W, n), F32),
        scratch_shapes=[pltpu.VMEM((tm, dm), F32)],
        compiler_params=_cparams(("parallel",)),
        name="inproj_colmajor",
    )(x.reshape(bsz, rows, GRID_W, dm), g, shift, scale, w)


def _ffn_up(h, wg, wu, layer):
    m, dm = h.shape
    dff = wg.shape[2]
    tm = min(TM_FFN_UP, m)
    tn = TN_FFN_UP
    return pl.pallas_call(
        _ffn_up_kernel,
        grid=(m // tm, dff // tn),
        in_specs=[pl.BlockSpec((tm, dm), lambda i, j: (i, 0)),
                  pl.BlockSpec((None, dm, tn), lambda i, j: (layer, 0, j)),
                  pl.BlockSpec((None, dm, tn), lambda i, j: (layer, 0, j))],
        out_specs=pl.BlockSpec((tm, tn), lambda i, j: (i, j)),
        out_shape=jax.ShapeDtypeStruct((m, dff), BF16),
        compiler_params=_cparams(("parallel", "arbitrary")),
        name="ffn_up",
    )(h, wg, wu)


def _ffn_down(a, w, layer, x, gate, final_g, final, rows_per_group):
    m, dm = x.shape
    dff = a.shape[1]
    tm = TM_FFN_DOWN
    grp = lambda i: (i * tm // rows_per_group, 0, 0)
    return pl.pallas_call(
        functools.partial(_ffn_down_kernel, final),
        grid=(m // tm,),
        in_specs=[pl.BlockSpec((tm, dff), lambda i: (i, 0)),
                  _resident((dff, dm), layer),
                  pl.BlockSpec((tm, dm), lambda i: (i, 0)),
                  pl.BlockSpec((1, 1, dm), grp),
                  _resident((1, dm))],
        out_specs=pl.BlockSpec((tm, dm), lambda i: (i, 0)),
        out_shape=jax.ShapeDtypeStruct((m, dm), F32),
        compiler_params=_cparams(("parallel",)),
        name="ffn_down",
    )(a, w, x, gate, final_g)


def _pad_cols(a, width):
    return jnp.pad(a, ((0, 0), (0, width - a.shape[1])))


def _split_w_in(w):
    ssd, ml, lru, gla = jnp.split(w, [1552, 1552 + 2064, 1552 + 2064 + 1024], axis=2)
    ssd_z, ssd_xbc, ssd_dt = ssd[..., :512], ssd[..., 512:1536], ssd[..., 1536:]
    ml_qkvo, ml_g = ml[..., :2048], ml[..., 2048:]
    lru_gate, lru_x = lru[..., :512], lru[..., 512:]
    gla_qkvr, gla_g1 = gla[..., :1536], gla[..., 1536:]
    pad = lambda a: jnp.pad(a, ((0, 0), (0, 0), (0, 128 - a.shape[2])))
    w_row = jnp.concatenate([ssd_xbc, lru_x, lru_gate, ssd_z, pad(ssd_dt)], axis=2)
    w_col = jnp.concatenate([ml_qkvo, gla_qkvr, pad(ml_g), pad(gla_g1)], axis=2)
    return w_row.astype(BF16), w_col.astype(BF16)


def _row128(a):
    return _pad_cols(a.reshape(1, -1).astype(F32), 128)


def _block_diag(w):
    nb, bi, bj = w.shape
    eye = jnp.eye(nb, dtype=w.dtype)
    return (eye[:, None, :, None] * w[:, :, None, :]).reshape(nb * bi, nb * bj)


def kernel(x, c, ctx, c_ctx, norm1_g, norm2_g, w_mod, b_mod, w_in, w_out, ssd_conv_w, ssd_conv_b, ssd_dt_bias, ssd_a_log, ssd_d, ssd_norm_g, ml_conv_w, ml_conv_b, ml_igate_b, ml_fgate_b, ml_norm_g, lru_conv_w, lru_conv_b, lru_wa, lru_ba, lru_wx, lru_bx, lru_lambda, gla_wg2, gla_bg, gla_norm_g, w_gate, w_up, w_down, final_g):
    bsz, length, dm = x.shape
    lctx = ctx.shape[1]
    depth = w_in.shape[0]
    rows = length // GRID_W
    assert dm == D_MODEL and length % SSD_CHUNK == 0 and lctx % SSD_CHUNK == 0
    assert length % LRU_CHUNK == 0 and lctx % LRU_CHUNK == 0 and lctx % ML_CHUNK == 0
    assert rows == ML_CHUNK == GLA_CHUNK

    cvec = jnp.concatenate([c, c_ctx[None, :], jnp.zeros((8 - bsz - 1, dm), F32)], axis=0)
    mods = _mods(cvec, w_mod, b_mod)

    xl = x.reshape(bsz * length, dm)
    xt = ctx.reshape(bsz * lctx, dm)
    row = lambda a: a.reshape(1, -1).astype(F32)
    w_row, w_col = _split_w_in(w_in)
    w_out_b = w_out.astype(BF16).reshape(depth, 4, GROUP_W, dm)
    w_down_b = w_down.astype(BF16)

    for i in range(depth):
        need_ctx = i < depth - 1
        m_l = [mods[i, :bsz, k * dm:(k + 1) * dm].reshape(bsz, 1, dm) for k in range(N_MOD)]
        m_t = [mods[i, bsz:bsz + 1, k * dm:(k + 1) * dm].reshape(1, 1, dm) for k in range(N_MOD)]
        g1 = row(norm1_g[i])
        ul_row = _inproj(xl, g1, m_l[0], m_l[1], w_row, i, length).reshape(bsz, length, N_ROW)
        ul_col = _inproj_colmajor(xl, g1, m_l[0], m_l[1], w_col, i, bsz, rows).reshape(bsz, length, N_COL)
        ut_row = _inproj(xt, g1, m_t[0], m_t[1], w_row, i, bsz * lctx).reshape(bsz, lctx, N_ROW)
        ut_col = _inproj(xt, g1, m_t[0], m_t[1], w_col, i, bsz * lctx).reshape(bsz, lctx, N_COL)

        ssd_params = [ssd_conv_w[i], row(ssd_conv_b[i]), _row128(ssd_dt_bias[i]), _row128(ssd_a_log[i]),
                      row(jnp.repeat(ssd_d[i], GROUP_W // SSD_HEADS)), row(ssd_norm_g[i])]
        ssd_bwd = [("u", 1024, OFF_SSD_XBC, True), ("u", 128, OFF_SSD_DT, False)]
        ssd_fwd = [("saved", 1024, 0, False), ("u", 128, OFF_SSD_DT, False), ("u", 512, OFF_SSD_Z, False)]
        ssd_zero = [jnp.zeros((bsz, SSD_HEADS // 2, 128, 128), F32)]
        ya_t, ya_l = _bidir_mixer(_ssd_kernel, ssd_bwd, ssd_fwd, 1024, SSD_CHUNK, ut_row, ul_row, ssd_zero,
                                  ssd_params, "ssd")

        ml_gb = _row128(jnp.concatenate([ml_igate_b[i].reshape(-1), ml_fgate_b[i].reshape(-1)]))
        ml_params = [ml_conv_w[i], row(ml_conv_b[i]), ml_gb, row(ml_norm_g[i])]
        ml_bwd = [("u", 512, OFF_ML_Q, True), ("u", 512, OFF_ML_K, True), ("u", 512, OFF_ML_V, False),
                  ("u", 128, OFF_ML_G, False)]
        ml_fwd = [("saved", 1024, 0, False), ("u", 512, OFF_ML_V, False), ("u", 128, OFF_ML_G, False),
                  ("u", 512, OFF_ML_O, False)]
        ml_zero = [jnp.zeros((bsz, ML_HEADS, 128, 128), F32), jnp.zeros((bsz, 2 * ML_HEADS, 128), F32)]
        yb_t, yb_l = _bidir_mixer(_mlstm_kernel, ml_bwd, ml_fwd, 1024, ML_CHUNK, ut_col, ul_col, ml_zero,
                                  ml_params, "mlstm", F32)

        lru_w = jnp.stack([jnp.concatenate([_block_diag(lru_wa[i, dd]), _block_diag(lru_wx[i, dd])], axis=1)
                           for dd in range(2)]).astype(BF16)
        lru_bias = jnp.concatenate([lru_ba[i], lru_bx[i]], axis=1).reshape(2, 1, 2 * GROUP_W)
        lru_params = [lru_conv_w[i], row(lru_conv_b[i]), lru_w, lru_bias, lru_lambda[i].reshape(2, 1, GROUP_W)]
        lru_bwd = [("u", 512, OFF_LRU_X, True)]
        lru_fwd = [("saved", 512, 0, False), ("u", 512, OFF_LRU_GATE, False)]
        lru_zero = [jnp.zeros((bsz, 1, GROUP_W), F32)]
        lru_scratch = [pltpu.VMEM((bsz, GROUP_W // 128, LRU_CHUNK, 128), F32)] * 2
        yc_t, yc_l = _bidir_mixer(_lru_kernel, lru_bwd, lru_fwd, 512, LRU_CHUNK, ut_row, ul_row, lru_zero,
                                  lru_params, "lru", scratch=lru_scratch)

        wg = jnp.zeros((2, 128, GLA_HEADS * GLA_DK), F32)
        wg = wg.at[0, :GLA_RANK].set(gla_wg2[i, 0]).at[1, GLA_RANK:2 * GLA_RANK].set(gla_wg2[i, 1])
        gla_params = [wg.astype(BF16), gla_bg[i].reshape(2, 1, GLA_HEADS * GLA_DK), row(gla_norm_g[i])]
        gla_bwd = [("u", 512, OFF_GLA_QK, False), ("u", 512, OFF_GLA_V, False), ("u", 128, OFF_GLA_G1, False)]
        gla_fwd = gla_bwd + [("u", 512, OFF_GLA_R, False)]
        gla_zero = [jnp.zeros((bsz, GROUP_W, GLA_HEADS * GLA_DK), F32)]
        yd_t, yd_l = _bidir_mixer(_gla_kernel, gla_bwd, gla_fwd, 0, GLA_CHUNK, ut_col, ul_col, gla_zero,
                                  gla_params, "gla", F32)

        fg = row(final_g)

        def tail(xs, ys, m, rows_per_group, final, colmajor):
            cm = lambda y: y.reshape(bsz, GRID_W, rows, GROUP_W) if colmajor else y.reshape(-1, GROUP_W)
            ys = [ys[0].reshape(-1, GROUP_W), cm(ys[1]), ys[2].reshape(-1, GROUP_W), cm(ys[3])]
            xs, h2 = _outproj(ys, w_out_b, i, xs, m[2], row(norm2_g[i]), m[3], m[4], rows_per_group, colmajor)
            act = _ffn_up(h2, w_gate, w_up, i)
            return _ffn_down(act, w_down_b, i, xs, m[5], fg, final, rows_per_group)

        xl = tail(xl, [ya_l, yb_l, yc_l, yd_l], m_l, length, i == depth - 1, True)
        if need_ctx:
            xt = tail(xt, [ya_t, yb_t, yc_t, yd_t], m_t, bsz * lctx, False, False)
    return xl.reshape(bsz, length, dm)
```

```python
import functools

import jax
import jax.numpy as jnp
from jax import lax
from jax.experimental import pallas as pl
from jax.experimental.pallas import tpu as pltpu

F32 = jnp.float32
BF16 = jnp.bfloat16
EPS = 1e-6
NEG = -1e30

D_MODEL = 2048
GRID_W = 64
GROUP_W = 512
N_MOD = 6
CONV_W = 5
HALO = 8

SSD_HEADS = 8
SSD_CHUNK = 128
ML_HEADS = 4
ML_CHUNK = 64
ML_HEAD_GROUP = 4
LRU_C = 8.0
LRU_CHUNK = 256
LRU_GROUP = 64
GLA_HEADS = 4
GLA_DK = 64
GLA_RANK = 16
GLA_TAU = 16.0
GLA_CHUNK = 64
GLA_SUB = 16

OFF_SSD_XBC = 0
OFF_LRU_X = 1024
OFF_LRU_GATE = 1536
OFF_SSD_Z = 2048
OFF_SSD_DT = 2560
N_ROW = 2688
OFF_ML_Q = 0
OFF_ML_K = 512
OFF_ML_V = 1024
OFF_ML_O = 1536
OFF_GLA_QK = 2048
OFF_GLA_V = 2560
OFF_GLA_R = 3072
OFF_ML_G = 3584
OFF_GLA_G1 = 3712
N_COL = 3840

VMEM_LIMIT = 56 * 1024 * 1024
TM_PROJ = 512
COLS_PER_TILE = 8
ROWS_PER_TILE = 8
TM_OUT = 512
TM_FFN_UP = 2048
TN_FFN_UP = 512
TM_FFN_DOWN = 512


def _cparams(sem):
    return pltpu.CompilerParams(dimension_semantics=sem, vmem_limit_bytes=VMEM_LIMIT)


def _resident(shape, layer=None):
    if layer is None:
        nd = len(shape)
        return pl.BlockSpec(shape, lambda *_: (0,) * nd, pipeline_mode=pl.Buffered(1))
    nd = len(shape)
    return pl.BlockSpec((None,) + tuple(shape), lambda *_: (layer,) + (0,) * nd, pipeline_mode=pl.Buffered(1))


def _sigmoid(x):
    return 0.5 + 0.5 * jnp.tanh(0.5 * x)


def _silu(x):
    return x * _sigmoid(x)


def _softplus(x):
    return jnp.maximum(x, 0.0) + jnp.log(1.0 + jnp.exp(-jnp.abs(x)))


def _log_sigmoid(x):
    return jnp.minimum(x, 0.0) - jnp.log(1.0 + jnp.exp(-jnp.abs(x)))


def _gelu_tanh(x):
    return 0.5 * x * (1.0 + jnp.tanh(0.7978845608028654 * (x + 0.044715 * (x * x * x))))


def _rms(x):
    return x * lax.rsqrt(jnp.mean(x * x, axis=-1, keepdims=True) + EPS)


def _dot(a, b):
    return jnp.dot(a, b, preferred_element_type=F32)


def _dot_nt(a, b):
    return lax.dot_general(a, b, (((1,), (1,)), ((), ())), preferred_element_type=F32)


def _dot_exact(a, b):
    return jnp.dot(a, b, precision=lax.Precision.HIGHEST, preferred_element_type=F32)


def _scan_mask(t, rev):
    ri = lax.broadcasted_iota(jnp.int32, (t, t), 0)
    ci = lax.broadcasted_iota(jnp.int32, (t, t), 1)
    return (ci >= ri) if rev else (ci <= ri)


def _conv5(prev, main, nxt, w, bias, first, last):
    t = main.shape[0]
    prev = jnp.where(first, 0.0, prev)
    nxt = jnp.where(last, 0.0, nxt)
    ext = jnp.concatenate([prev, main, nxt], axis=0)
    n = t + 2 * HALO
    acc = None
    for k in range(CONV_W):
        sh = (CONV_W // 2 - k) % n
        r = ext if sh == 0 else pltpu.roll(ext, sh, axis=0)
        term = r[HALO:HALO + t] * w[k:k + 1, :]
        acc = term if acc is None else acc + term
    return acc + bias


def _mod_kernel(c_ref, w_ref, b_ref, o_ref):
    s = _silu(c_ref[...]).astype(BF16)
    o_ref[...] = _dot(s, w_ref[...].astype(BF16)) + b_ref[...]


def _norm_mod(x, g_ref, sh_ref, sc_ref):
    return ((_rms(x) * g_ref[...]) * (1.0 + sc_ref[0]) + sh_ref[0]).astype(BF16)


def _inproj_kernel(x_ref, g_ref, sh_ref, sc_ref, w_ref, o_ref):
    o_ref[...] = _dot(_norm_mod(x_ref[...], g_ref, sh_ref, sc_ref), w_ref[...])


def _inproj_colmajor_kernel(x_ref, g_ref, sh_ref, sc_ref, w_ref, o_ref, xs_ref):
    rows = x_ref.shape[1]
    for wl in range(COLS_PER_TILE):
        xs_ref[wl * rows:(wl + 1) * rows, :] = x_ref[0, :, wl, :]
    o_ref[...] = _dot(_norm_mod(xs_ref[...], g_ref, sh_ref, sc_ref), w_ref[...])


def _outproj_kernel(colmajor, ya_ref, yb_ref, yc_ref, yd_ref, w_ref, x_ref, gate_ref, g_ref, sh_ref, sc_ref,
                    xo_ref, h_ref):
    if colmajor:
        def rowmajor(ref):
            return jnp.concatenate([ref[0, :, rl, :] for rl in range(ROWS_PER_TILE)], axis=0).astype(BF16)
        yb, yd = rowmajor(yb_ref), rowmajor(yd_ref)
    else:
        yb, yd = yb_ref[...], yd_ref[...]
    acc = _dot(ya_ref[...], w_ref[0])
    acc = acc + _dot(yc_ref[...], w_ref[2])
    acc = acc + _dot(yb, w_ref[1])
    acc = acc + _dot(yd, w_ref[3])
    xn = x_ref[...] + gate_ref[0] * acc
    xo_ref[...] = xn
    y = _rms(xn) * g_ref[...]
    h_ref[...] = (y * (1.0 + sc_ref[0]) + sh_ref[0]).astype(BF16)


def _ffn_up_kernel(h_ref, wg_ref, wu_ref, o_ref):
    h = h_ref[...]
    g = _dot(h, wg_ref[...].astype(BF16))
    u = _dot(h, wu_ref[...].astype(BF16))
    o_ref[...] = (_silu(g) * u).astype(BF16)


def _ffn_down_kernel(final, a_ref, w_ref, x_ref, gate_ref, fg_ref, o_ref):
    xn = x_ref[...] + gate_ref[0] * _dot(a_ref[...], w_ref[...])
    if final:
        xn = _rms(xn) * fg_ref[...]
    o_ref[...] = xn


def _ssd_kernel(rev, finish, nchunks, nb, *refs):
    if finish:
        (xc_ref, dt_ref, z_ref, yp_ref, s0_ref, cw_ref, cb_ref, dtb_ref, alog_ref,
         dsk_ref, ng_ref, y_ref, s_ref) = refs
    else:
        (xbc_ref, xp_ref, xn_ref, dt_ref, s0_ref, cw_ref, cb_ref, dtb_ref, alog_ref,
         dsk_ref, ng_ref, y_ref, xc_ref, s_ref) = refs
    t = SSD_CHUNK
    d = 1 if rev else 0
    j = pl.program_id(0)
    c = (nchunks - 1 - j) if rev else j

    @pl.when(j == 0)
    def _():
        s_ref[...] = s0_ref[...]

    mask = _scan_mask(t, rev)
    maskf = mask.astype(F32)
    er = 0 if rev else t - 1
    lo = lax.broadcasted_iota(jnp.int32, (t, 128), 1) < 64
    lo_rows = lax.broadcasted_iota(jnp.int32, (128, 128), 0) < 64
    cw, cbias = cw_ref[...], cb_ref[...]
    nega = -jnp.exp(alog_ref[...])

    bs = range(nb)
    if finish:
        xbc = [xc_ref[bi] for bi in bs]
    else:
        xbc = [_silu(_conv5(xp_ref[bi], xbc_ref[bi], xn_ref[bi], cw, cbias, c == 0, c == nchunks - 1))
               for bi in bs]
        for bi in bs:
            xc_ref[bi] = xbc[bi]
    dt_all = [_softplus(dt_ref[bi] + dtb_ref[...]) for bi in bs]
    b_all = [_dot_exact(maskf, dt_all[bi] * nega) for bi in bs]
    b_t = [b_all[bi].T for bi in bs]
    dt_t = [dt_all[bi].T for bi in bs]
    ys = [[None] * (SSD_HEADS // 2) for _ in bs]
    for p in range(SSD_HEADS // 2):
        g = p // 2
        bm = [xbc[bi][:, 512 + 128 * g:640 + 128 * g].astype(BF16) for bi in bs]
        cm = [xbc[bi][:, 768 + 128 * g:896 + 128 * g].astype(BF16) for bi in bs]
        cb = [_dot_nt(cm[bi], bm[bi]) for bi in bs]
        xpair = [xbc[bi][:, 128 * p:128 * p + 128] for bi in bs]
        xpair_b = [xpair[bi].astype(BF16) for bi in bs]
        s_pair = [s_ref[bi, p] for bi in bs]
        y_inter = [_dot_nt(cm[bi], s_pair[bi].astype(BF16)) for bi in bs]
        yh, eb, coef, dec = [], [], [], []
        for hh in range(2):
            l = d * SSD_HEADS + 2 * p + hh
            bcol = [b_all[bi][:, l:l + 1] for bi in bs]
            seg = [jnp.exp(jnp.where(mask, bcol[bi] - b_t[bi][l:l + 1, :], NEG)) for bi in bs]
            w = [(cb[bi] * seg[bi] * dt_t[bi][l:l + 1, :]).astype(BF16) for bi in bs]
            yh.append([_dot(w[bi], xpair_b[bi]) for bi in bs])
            eb.append([jnp.exp(bcol[bi]) for bi in bs])
            bl = [b_all[bi][er:er + 1, l:l + 1] for bi in bs]
            coef.append([jnp.exp(bl[bi] - bcol[bi]) * dt_all[bi][:, l:l + 1] for bi in bs])
            dec.append([jnp.exp(bl[bi]) for bi in bs])
        xw = [(xpair[bi] * jnp.where(lo, coef[0][bi], coef[1][bi])).T.astype(BF16) for bi in bs]
        upd = [_dot(xw[bi], bm[bi]) for bi in bs]
        for bi in bs:
            ys[bi][p] = (jnp.where(lo, yh[0][bi], yh[1][bi])
                         + y_inter[bi] * jnp.where(lo, eb[0][bi], eb[1][bi]))
            s_ref[bi, p] = jnp.where(lo_rows, dec[0][bi], dec[1][bi]) * s_pair[bi] + upd[bi]
    for bi in bs:
        y = jnp.concatenate(ys[bi], axis=1)
        if finish:
            y = y + yp_ref[bi] + dsk_ref[...] * xbc[bi][:, :GROUP_W]
            y = _rms(y * _silu(z_ref[bi])) * ng_ref[...]
            y_ref[bi] = y.astype(y_ref.dtype)
        else:
            y_ref[bi] = y


def _mlstm_kernel(rev, finish, nchunks, nb, *refs):
    if finish:
        (qkc_ref, v_ref, g_ref, o_ref, yp_ref, cs0_ref, sm0_ref,
         cw_ref, cb_ref, gb_ref, ng_ref, y_ref, cs_ref, sm_ref) = refs
    else:
        (q_ref, qp_ref, qn_ref, k_ref, kp_ref, kn_ref, v_ref, g_ref, cs0_ref, sm0_ref,
         cw_ref, cb_ref, gb_ref, ng_ref, y_ref, qkc_ref, cs_ref, sm_ref) = refs
    t = ML_CHUNK
    d = 1 if rev else 0
    j = pl.program_id(0)
    c = (nchunks - 1 - j) if rev else j
    first, last = c == 0, c == nchunks - 1

    @pl.when(j == 0)
    def _():
        cs_ref[...] = cs0_ref[...]
        sm_ref[...] = sm0_ref[...]

    cw = cw_ref[...]
    cbias = cb_ref[...]
    maskf = _scan_mask(t, rev).astype(F32)
    mask_t = _scan_mask(t, not rev)
    er = 0 if rev else t - 1

    bs = range(nb)
    if finish:
        q = [qkc_ref[bi][:, :GROUP_W] for bi in bs]
        k = [qkc_ref[bi][:, GROUP_W:] for bi in bs]
    else:
        q = [_silu(_conv5(qp_ref[bi], q_ref[bi], qn_ref[bi], cw[:, :GROUP_W], cbias[:, :GROUP_W], first, last))
             for bi in bs]
        k = [_silu(_conv5(kp_ref[bi], k_ref[bi], kn_ref[bi], cw[:, GROUP_W:], cbias[:, GROUP_W:], first, last))
             * (128.0 ** -0.5) for bi in bs]
        for bi in bs:
            qkc_ref[bi, :, :GROUP_W] = q[bi]
            qkc_ref[bi, :, GROUP_W:] = k[bi]
    v = [v_ref[bi] for bi in bs]
    gts = [g_ref[bi] + gb_ref[...] for bi in bs]
    b_all = [_dot_exact(maskf, _log_sigmoid(gts[bi])) for bi in bs]
    g_t = [gts[bi].T for bi in bs]
    b_t = [b_all[bi].T for bi in bs]
    ys = [[None] * ML_HEADS for _ in bs]
    for h0 in range(0, ML_HEADS, ML_HEAD_GROUP):
        ch = [(bi, h) for h in range(h0, h0 + ML_HEAD_GROUP) for bi in bs]
        cs_ = range(len(ch))
        li = [d * ML_HEADS + h for _, h in ch]
        lf = [2 * ML_HEADS + l for l in li]
        sl = [slice(128 * h, 128 * h + 128) for _, h in ch]
        brow = [b_t[bi][lf[c]:lf[c] + 1, :] for c, (bi, h) in enumerate(ch)]
        igrow = [g_t[bi][li[c]:li[c] + 1, :] for c, (bi, h) in enumerate(ch)]
        ccol = [gts[bi][:, li[c]:li[c] + 1] - b_all[bi][:, lf[c]:lf[c] + 1]
                for c, (bi, h) in enumerate(ch)]
        m_prev = [sm_ref[bi, ML_HEADS + h:ML_HEADS + h + 1, 0:1] for bi, h in ch]
        ns = [sm_ref[bi, h:h + 1, :] for bi, h in ch]
        cs = [cs_ref[bi, h] for bi, h in ch]
        dmat = [jnp.where(mask_t, brow[c] + ccol[c], NEG) for c in cs_]
        inter = [brow[c] + m_prev[c] for c in cs_]
        mt = [jnp.maximum(inter[c], jnp.max(dmat[c], axis=0, keepdims=True)) for c in cs_]
        w = [jnp.exp(dmat[c] - mt[c]) for c in cs_]
        sc = [jnp.exp(inter[c] - mt[c]) for c in cs_]
        qb = [q[bi][:, sl[c]].astype(BF16) for c, (bi, h) in enumerate(ch)]
        kb = [k[bi][:, sl[c]].astype(BF16) for c, (bi, h) in enumerate(ch)]
        sw = [_dot_nt(kb[c], qb[c]) * w[c] for c in cs_]
        v_t = [v[bi][:, sl[c]].T for c, (bi, h) in enumerate(ch)]
        num = [_dot(v_t[c].astype(BF16), sw[c].astype(BF16)) for c in cs_]
        qc = [_dot_nt(cs[c].astype(BF16), qb[c]) for c in cs_]
        qn = [_dot_nt(jnp.broadcast_to(ns[c], (8, 128)).astype(BF16), qb[c])[0:1] for c in cs_]
        den = [jnp.sum(sw[c], axis=0, keepdims=True) + sc[c] * qn[c] for c in cs_]
        for c, (bi, h) in enumerate(ch):
            inv = 1.0 / jnp.maximum(jnp.abs(den[c]), jnp.exp(-mt[c]))
            ys[bi][h] = ((num[c] + sc[c] * qc[c]) * inv).T
        bl = [b_all[bi][er:er + 1, lf[c]:lf[c] + 1] for c, (bi, h) in enumerate(ch)]
        tail = [bl[c] - brow[c] + igrow[c] for c in cs_]
        m_new = [jnp.maximum(bl[c] + m_prev[c], jnp.max(tail[c], axis=1, keepdims=True)) for c in cs_]
        ws = [jnp.exp(tail[c] - m_new[c]) for c in cs_]
        sc_end = [jnp.exp(bl[c] + m_prev[c] - m_new[c]) for c in cs_]
        upd = [_dot((v_t[c] * ws[c]).astype(BF16), kb[c]) for c in cs_]
        nup = [_dot(jnp.broadcast_to(ws[c], (8, t)).astype(BF16), kb[c])[0:1] for c in cs_]
        for c, (bi, h) in enumerate(ch):
            cs_ref[bi, h] = sc_end[c] * cs[c] + upd[c]
            sm_ref[bi, h:h + 1, :] = sc_end[c] * ns[c] + nup[c]
            sm_ref[bi, ML_HEADS + h:ML_HEADS + h + 1, :] = jnp.broadcast_to(m_new[c], (1, 128))
    for bi in bs:
        if finish:
            yp = yp_ref[bi]
            outs = [_rms(ys[bi][h] + yp[:, 128 * h:128 * h + 128]) for h in range(ML_HEADS)]
            y = jnp.concatenate(outs, axis=1) * ng_ref[...]
            y_ref[bi] = (_sigmoid(o_ref[bi]) * y).astype(y_ref.dtype)
        else:
            y_ref[bi] = jnp.concatenate(ys[bi], axis=1)


def _lru_kernel(rev, finish, nchunks, nb, *refs):
    if finish:
        (xf_ref, gate_ref, yp_ref, h0_ref, cw_ref, cb_ref, w_ref, bias_ref, lam_ref,
         y_ref, h_ref, sa_ref, sh_ref) = refs
    else:
        (x_ref, xp_ref, xn_ref, h0_ref, cw_ref, cb_ref, w_ref, bias_ref, lam_ref,
         y_ref, xf_ref, h_ref, sa_ref, sh_ref) = refs
    t = LRU_CHUNK
    d = 1 if rev else 0
    j = pl.program_id(0)
    c = (nchunks - 1 - j) if rev else j

    @pl.when(j == 0)
    def _():
        h_ref[...] = h0_ref[...]

    cw, cbias = cw_ref[...], cb_ref[...]
    lsl = LRU_C * _log_sigmoid(lam_ref[d])
    sub = lax.broadcasted_iota(jnp.int32, (8, 128), 0)
    ng = GROUP_W // 128

    for bi in range(nb):
        if finish:
            xf = xf_ref[bi]
        else:
            xf = _conv5(xp_ref[bi], x_ref[bi], xn_ref[bi], cw, cbias, c == 0, c == nchunks - 1)
            xf_ref[bi] = xf
        pre = _dot(xf.astype(BF16), w_ref[d]) + bias_ref[d]
        r = _sigmoid(pre[:, :GROUP_W])
        ig = _sigmoid(pre[:, GROUP_W:])
        loga = r * lsl
        a = jnp.exp(loga)
        bx = jnp.sqrt(-jnp.tanh(loga) * (1.0 + a * a)) * ig * xf
        for g in range(ng):
            sa_ref[bi, g] = a[:, 128 * g:128 * g + 128]
            sh_ref[bi, g] = bx[:, 128 * g:128 * g + 128]

    chains = [(bi, g) for bi in range(nb) for g in range(ng)]
    carry = [h_ref[bi, :, 128 * g:128 * g + 128] for bi, g in chains]
    for grp in (range(t // LRU_GROUP - 1, -1, -1) if rev else range(t // LRU_GROUP)):
        base = LRU_GROUP * grp
        a_run, h_run = [None] * len(chains), [None] * len(chains)
        a_cum = [[None] * 8 for _ in chains]
        h_loc = [[None] * 8 for _ in chains]
        for v in (range(7, -1, -1) if rev else range(8)):
            for ci, (bi, g) in enumerate(chains):
                a_v = sa_ref[bi, g, pl.ds(base + v, 8, stride=8), :]
                b_v = sh_ref[bi, g, pl.ds(base + v, 8, stride=8), :]
                if a_run[ci] is None:
                    a_run[ci], h_run[ci] = a_v, b_v
                else:
                    h_run[ci] = a_v * h_run[ci] + b_v
                    a_run[ci] = a_v * a_run[ci]
                a_cum[ci][v], h_loc[ci][v] = a_run[ci], h_run[ci]
        for ci, (bi, g) in enumerate(chains):
            pa, ph = a_run[ci], h_run[ci]
            for sh in (1, 2, 4):
                if rev:
                    valid = sub < 8 - sh
                    a_s = pltpu.roll(pa, 8 - sh, axis=0)
                    h_s = pltpu.roll(ph, 8 - sh, axis=0)
                else:
                    valid = sub >= sh
                    a_s = pltpu.roll(pa, sh, axis=0)
                    h_s = pltpu.roll(ph, sh, axis=0)
                ph = jnp.where(valid, pa * h_s + ph, ph)
                pa = jnp.where(valid, pa * a_s, pa)
            after = pa * carry[ci] + ph
            if rev:
                cin = jnp.where(sub == 7, carry[ci], pltpu.roll(after, 7, axis=0))
                carry[ci] = after[0:1]
            else:
                cin = jnp.where(sub == 0, carry[ci], pltpu.roll(after, 1, axis=0))
                carry[ci] = after[7:8]
            for v in range(8):
                sh_ref[bi, g, pl.ds(base + v, 8, stride=8), :] = h_loc[ci][v] + a_cum[ci][v] * cin
    for ci, (bi, g) in enumerate(chains):
        h_ref[bi, :, 128 * g:128 * g + 128] = carry[ci]
    for bi in range(nb):
        hs = jnp.concatenate([sh_ref[bi, g] for g in range(ng)], axis=1)
        if finish:
            y_ref[bi] = ((hs + yp_ref[bi]) * _gelu_tanh(gate_ref[bi])).astype(y_ref.dtype)
        else:
            y_ref[bi] = hs


def _gla_kernel(rev, finish, nchunks, nb, *refs):
    if finish:
        (qk_ref, v_ref, g1_ref, r_ref, yp_ref, s0_ref, wg_ref, bg_ref, ng_ref, y_ref, s_ref) = refs
    else:
        (qk_ref, v_ref, g1_ref, s0_ref, wg_ref, bg_ref, ng_ref, y_ref, s_ref) = refs
    t = GLA_CHUNK
    d = 1 if rev else 0
    dkk = GLA_HEADS * GLA_DK
    j = pl.program_id(0)

    @pl.when(j == 0)
    def _():
        s_ref[...] = s0_ref[...]

    mask = _scan_mask(t, rev)
    maskf = mask.astype(F32)
    rows = lax.broadcasted_iota(jnp.int32, (t, 1), 0)
    lane_head = lax.broadcasted_iota(jnp.int32, (GLA_SUB, dkk), 1) // GLA_DK
    blk = (lax.broadcasted_iota(jnp.int32, (GROUP_W, dkk), 0) // 128
           == lax.broadcasted_iota(jnp.int32, (GROUP_W, dkk), 1) // GLA_DK)
    er = 0 if rev else t - 1

    bs = range(nb)
    q = [qk_ref[bi][:, :dkk] * (GLA_DK ** -0.5) for bi in bs]
    k = [qk_ref[bi][:, dkk:] for bi in bs]
    v = [v_ref[bi] for bi in bs]
    vb = [v[bi].astype(BF16) for bi in bs]
    glog = [_dot(g1_ref[bi].astype(BF16), wg_ref[d]) + bg_ref[d] for bi in bs]
    la = [_log_sigmoid(glog[bi]) * (1.0 / GLA_TAU) for bi in bs]
    b = [_dot_exact(maskf, la[bi]) for bi in bs]
    excl = [b[bi] - la[bi] for bi in bs]
    s_t = [s_ref[bi] for bi in bs]
    o_inter = [_dot_nt((q[bi] * jnp.exp(b[bi])).astype(BF16), s_t[bi].astype(BF16)) for bi in bs]

    att_blocks = [[None] * (t // GLA_SUB) for _ in bs]
    for i in range(t // GLA_SUB):
        r0 = GLA_SUB * i
        if rev:
            ref_row = [excl[bi][r0 + GLA_SUB - 1:r0 + GLA_SUB] for bi in bs]
            kvalid = rows >= r0
        else:
            ref_row = [excl[bi][r0:r0 + 1] for bi in bs]
            kvalid = rows < r0 + GLA_SUB
        qi = [q[bi][r0:r0 + GLA_SUB] * jnp.exp(b[bi][r0:r0 + GLA_SUB] - ref_row[bi]) for bi in bs]
        ki = [(k[bi] * jnp.exp(jnp.where(kvalid, ref_row[bi] - b[bi], NEG))).astype(BF16) for bi in bs]
        qs = [jnp.concatenate([jnp.where(lane_head == h, qi[bi], 0.0) for h in range(GLA_HEADS)],
                              axis=0).astype(BF16) for bi in bs]
        for bi in bs:
            att_blocks[bi][i] = _dot_nt(qs[bi], ki[bi])
    outs = [[None] * GLA_HEADS for _ in bs]
    for h in range(GLA_HEADS):
        att = [jnp.where(mask, jnp.concatenate([ab[GLA_SUB * h:GLA_SUB * h + GLA_SUB] for ab in att_blocks[bi]],
                                               axis=0), 0.0).astype(BF16) for bi in bs]
        for bi in bs:
            outs[bi][h] = _dot(att[bi], vb[bi][:, 128 * h:128 * h + 128])

    bl = [b[bi][er:er + 1] for bi in bs]
    kd = [(k[bi] * jnp.exp(bl[bi] - b[bi])).astype(BF16) for bi in bs]
    v_t = [jnp.concatenate([v[bi][:, 128 * h:128 * h + 128].T for h in range(GLA_HEADS)], axis=0).astype(BF16)
           for bi in bs]
    upd = [_dot(v_t[bi], kd[bi]) for bi in bs]
    for bi in bs:
        s_ref[bi] = s_t[bi] * jnp.exp(bl[bi]) + jnp.where(blk, upd[bi], 0.0)
    for bi in bs:
        o = jnp.concatenate(outs[bi], axis=1) + o_inter[bi]
        if finish:
            o = o + yp_ref[bi]
            o = jnp.concatenate([_rms(o[:, 128 * h:128 * h + 128]) for h in range(GLA_HEADS)], axis=1)
            y_ref[bi] = (o * ng_ref[...] * _silu(r_ref[bi])).astype(y_ref.dtype)
        else:
            y_ref[bi] = o


def _chunk_specs(nb, width, off, t, length, rev, nchunks, halo):
    ob = off // width
    tb = t // HALO

    def cj(j):
        return (nchunks - 1 - j) if rev else j

    main = pl.BlockSpec((nb, t, width), lambda j: (0, cj(j), ob))
    if not halo:
        return [main]
    prev = pl.BlockSpec((nb, HALO, width), lambda j: (0, jnp.maximum(cj(j) * tb - 1, 0), ob))
    nxt = pl.BlockSpec((nb, HALO, width), lambda j: (0, jnp.minimum((cj(j) + 1) * tb, length // HALO - 1), ob))
    return [main, prev, nxt]


def _run_mixer(body, pieces, t, rev, finish, ypart, states, params, name, save_width=0, out_dtype=BF16,
               scratch=()):
    nb, length = pieces[0][0].shape[0], pieces[0][0].shape[1]
    nchunks = length // t
    cj = (lambda j: nchunks - 1 - j) if rev else (lambda j: j)
    row_spec = lambda width: pl.BlockSpec((nb, t, width), lambda j: (0, cj(j), 0))
    in_specs, args = [], []
    for arr, width, off, halo in pieces:
        sp = _chunk_specs(nb, width, off, t, length, rev, nchunks, halo)
        in_specs += sp
        args += [arr] * len(sp)
    if finish:
        in_specs.append(row_spec(GROUP_W))
        args.append(ypart)
    for a in list(states) + list(params):
        in_specs.append(pl.BlockSpec(a.shape, lambda j, _nd=a.ndim: (0,) * _nd))
        args.append(a)
    out_shape = [jax.ShapeDtypeStruct((nb, length, GROUP_W), out_dtype if finish else F32)]
    out_specs = [row_spec(GROUP_W)]
    if save_width:
        out_shape.append(jax.ShapeDtypeStruct((nb, length, save_width), F32))
        out_specs.append(row_spec(save_width))
    for s in states:
        out_shape.append(jax.ShapeDtypeStruct(s.shape, s.dtype))
        out_specs.append(pl.BlockSpec(s.shape, lambda j, _nd=s.ndim: (0,) * _nd))
    res = pl.pallas_call(
        functools.partial(body, rev, finish, nchunks, nb),
        grid=(nchunks,),
        in_specs=in_specs,
        out_specs=out_specs,
        out_shape=out_shape,
        scratch_shapes=list(scratch),
        compiler_params=_cparams(("arbitrary",)),
        name=name,
    )(*args)
    nfix = 2 if save_width else 1
    return res[0], (res[1] if save_width else None), list(res[nfix:])


def _bidir_mixer(body, bwd_pieces, fwd_pieces, save_width, t, u_ctx, u_lat, zero_states, params, name,
                 lat_dtype=BF16, scratch=()):
    def run(u, rev, ypart, saved, states, tag, out_dtype=BF16):
        src = {"u": u, "saved": saved}
        pieces = [(src[s], w, o, h) for s, w, o, h in (bwd_pieces if rev else fwd_pieces)]
        return _run_mixer(body, pieces, t, rev, not rev, ypart, states, params, name + tag,
                          save_width if rev else 0, out_dtype, scratch)

    yb_c, sv_c, st_b = run(u_ctx, True, None, None, zero_states, "_ctx_bwd")
    y_c, _, st_f = run(u_ctx, False, yb_c, sv_c, zero_states, "_ctx_fwd")
    yb_l, sv_l, _ = run(u_lat, True, None, None, st_b, "_lat_bwd")
    y_l, _, _ = run(u_lat, False, yb_l, sv_l, st_f, "_lat_fwd", lat_dtype)
    return y_c, y_l


def _mods(cvec, w_mod, b_mod):
    depth, dm, nm = w_mod.shape
    tn = 1024
    return pl.pallas_call(
        _mod_kernel,
        grid=(depth, nm // tn),
        in_specs=[pl.BlockSpec((8, dm), lambda l, j: (0, 0)),
                  pl.BlockSpec((None, dm, tn), lambda l, j: (l, 0, j)),
                  pl.BlockSpec((None, 1, tn), lambda l, j: (l, 0, j))],
        out_specs=pl.BlockSpec((None, 8, tn), lambda l, j: (l, 0, j)),
        out_shape=jax.ShapeDtypeStruct((depth, 8, nm), F32),
        compiler_params=_cparams(("parallel", "parallel")),
        name="mods",
    )(cvec, w_mod, b_mod.reshape(depth, 1, nm))


def _inproj(x, g, shift, scale, w, layer, rows_per_group):
    m, dm = x.shape
    n = w.shape[2]
    tm = TM_PROJ
    grp = lambda i: (i * tm // rows_per_group, 0, 0)
    return pl.pallas_call(
        _inproj_kernel,
        grid=(m // tm,),
        in_specs=[pl.BlockSpec((tm, dm), lambda i: (i, 0)),
                  _resident((1, dm)),
                  pl.BlockSpec((1, 1, dm), grp),
                  pl.BlockSpec((1, 1, dm), grp),
                  _resident((dm, n), layer)],
        out_specs=pl.BlockSpec((tm, n), lambda i: (i, 0)),
        out_shape=jax.ShapeDtypeStruct((m, n), F32),
        compiler_params=_cparams(("parallel",)),
        name="inproj",
    )(x, g, shift, scale, w)


def _outproj(ys, w, layer, x, gate, g, shift, scale, rows_per_group, colmajor):
    m, dm = x.shape
    tm = min(TM_OUT, rows_per_group)
    grp = lambda i: (i * tm // rows_per_group, 0, 0)
    yspec = pl.BlockSpec((tm, GROUP_W), lambda i: (i, 0))
    if colmajor:
        assert tm == ROWS_PER_TILE * GRID_W
        tiles = rows_per_group // tm
        cspec = pl.BlockSpec((1, GRID_W, ROWS_PER_TILE, GROUP_W), lambda i: (i // tiles, 0, i % tiles, 0))
    else:
        cspec = yspec
    return pl.pallas_call(
        functools.partial(_outproj_kernel, colmajor),
        grid=(m // tm,),
        in_specs=[yspec, cspec, yspec, cspec,
                  _resident((4, GROUP_W, dm), layer),
                  pl.BlockSpec((tm, dm), lambda i: (i, 0)),
                  pl.BlockSpec((1, 1, dm), grp),
                  _resident((1, dm)),
                  pl.BlockSpec((1, 1, dm), grp),
                  pl.BlockSpec((1, 1, dm), grp)],
        out_specs=[pl.BlockSpec((tm, dm), lambda i: (i, 0)), pl.BlockSpec((tm, dm), lambda i: (i, 0))],
        out_shape=[jax.ShapeDtypeStruct((m, dm), F32), jax.ShapeDtypeStruct((m, dm), BF16)],
        compiler_params=_cparams(("parallel",)),
        name="outproj",
    )(*ys, w, x, gate, g, shift, scale)


def _inproj_colmajor(x, g, shift, scale, w, layer, bsz, rows):
    dm = x.shape[1]
    n = w.shape[2]
    tiles = GRID_W // COLS_PER_TILE
    tm = COLS_PER_TILE * rows
    grp = lambda i: (i // tiles, 0, 0)
    return pl.pallas_call(
        _inproj_colmajor_kernel,
        grid=(bsz * tiles,),
        in_specs=[pl.BlockSpec((1, rows, COLS_PER_TILE, dm), lambda i: (i // tiles, 0, i % tiles, 0)),
                  _resident((1, dm)),
                  pl.BlockSpec((1, 1, dm), grp),
                  pl.BlockSpec((1, 1, dm), grp),
                  _resident((dm, n), layer)],
        out_specs=pl.BlockSpec((tm, n), lambda i: (i, 0)),
        out_shape=jax.ShapeDtypeStruct((bsz * rows * GRID_W, n), F32),
        scratch_shapes=[pltpu.VMEM((tm, dm), F32)],
        compiler_params=_cparams(("parallel",)),
        name="inproj_colmajor",
    )(x.reshape(bsz, rows, GRID_W, dm), g, shift, scale, w)


def _ffn_up(h, wg, wu, layer):
    m, dm = h.shape
    dff = wg.shape[2]
    tm = min(TM_FFN_UP, m)
    tn = TN_FFN_UP
    return pl.pallas_call(
        _ffn_up_kernel,
        grid=(m // tm, dff // tn),
        in_specs=[pl.BlockSpec((tm, dm), lambda i, j: (i, 0)),
                  pl.BlockSpec((None, dm, tn), lambda i, j: (layer, 0, j)),
                  pl.BlockSpec((None, dm, tn), lambda i, j: (layer, 0, j))],
        out_specs=pl.BlockSpec((tm, tn), lambda i, j: (i, j)),
        out_shape=jax.ShapeDtypeStruct((m, dff), BF16),
        compiler_params=_cparams(("parallel", "arbitrary")),
        name="ffn_up",
    )(h, wg, wu)


def _ffn_down(a, w, layer, x, gate, final_g, final, rows_per_group):
    m, dm = x.shape
    dff = a.shape[1]
    tm = TM_FFN_DOWN
    grp = lambda i: (i * tm // rows_per_group, 0, 0)
    return pl.pallas_call(
        functools.partial(_ffn_down_kernel, final),
        grid=(m // tm,),
        in_specs=[pl.BlockSpec((tm, dff), lambda i: (i, 0)),
                  _resident((dff, dm), layer),
                  pl.BlockSpec((tm, dm), lambda i: (i, 0)),
                  pl.BlockSpec((1, 1, dm), grp),
                  _resident((1, dm))],
        out_specs=pl.BlockSpec((tm, dm), lambda i: (i, 0)),
        out_shape=jax.ShapeDtypeStruct((m, dm), F32),
        compiler_params=_cparams(("parallel",)),
        name="ffn_down",
    )(a, w, x, gate, final_g)


def _pad_cols(a, width):
    return jnp.pad(a, ((0, 0), (0, width - a.shape[1])))


def _split_w_in(w):
    ssd, ml, lru, gla = jnp.split(w, [1552, 1552 + 2064, 1552 + 2064 + 1024], axis=2)
    ssd_z, ssd_xbc, ssd_dt = ssd[..., :512], ssd[..., 512:1536], ssd[..., 1536:]
    ml_qkvo, ml_g = ml[..., :2048], ml[..., 2048:]
    lru_gate, lru_x = lru[..., :512], lru[..., 512:]
    gla_qkvr, gla_g1 = gla[..., :1536], gla[..., 1536:]
    pad = lambda a: jnp.pad(a, ((0, 0), (0, 0), (0, 128 - a.shape[2])))
    w_row = jnp.concatenate([ssd_xbc, lru_x, lru_gate, ssd_z, pad(ssd_dt)], axis=2)
    w_col = jnp.concatenate([ml_qkvo, gla_qkvr, pad(ml_g), pad(gla_g1)], axis=2)
    return w_row.astype(BF16), w_col.astype(BF16)


def _row128(a):
    return _pad_cols(a.reshape(1, -1).astype(F32), 128)


def _block_diag(w):
    nb, bi, bj = w.shape
    eye = jnp.eye(nb, dtype=w.dtype)
    return (eye[:, None, :, None] * w[:, :, None, :]).reshape(nb * bi, nb * bj)


def kernel(x, c, ctx, c_ctx, norm1_g, norm2_g, w_mod, b_mod, w_in, w_out, ssd_conv_w, ssd_conv_b, ssd_dt_bias, ssd_a_log, ssd_d, ssd_norm_g, ml_conv_w, ml_conv_b, ml_igate_b, ml_fgate_b, ml_norm_g, lru_conv_w, lru_conv_b, lru_wa, lru_ba, lru_wx, lru_bx, lru_lambda, gla_wg2, gla_bg, gla_norm_g, w_gate, w_up, w_down, final_g):
    bsz, length, dm = x.shape
    lctx = ctx.shape[1]
    depth = w_in.shape[0]
    rows = length // GRID_W
    assert dm == D_MODEL and length % SSD_CHUNK == 0 and lctx % SSD_CHUNK == 0
    assert length % LRU_CHUNK == 0 and lctx % LRU_CHUNK == 0 and lctx % ML_CHUNK == 0
    assert rows == ML_CHUNK == GLA_CHUNK

    cvec = jnp.concatenate([c, c_ctx[None, :], jnp.zeros((8 - bsz - 1, dm), F32)], axis=0)
    mods = _mods(cvec, w_mod, b_mod)

    xl = x.reshape(bsz * length, dm)
    xt = ctx.reshape(bsz * lctx, dm)
    row = lambda a: a.reshape(1, -1).astype(F32)
    w_row, w_col = _split_w_in(w_in)
    w_out_b = w_out.astype(BF16).reshape(depth, 4, GROUP_W, dm)
    w_down_b = w_down.astype(BF16)

    for i in range(depth):
        need_ctx = i < depth - 1
        m_l = [mods[i, :bsz, k * dm:(k + 1) * dm].reshape(bsz, 1, dm) for k in range(N_MOD)]
        m_t = [mods[i, bsz:bsz + 1, k * dm:(k + 1) * dm].reshape(1, 1, dm) for k in range(N_MOD)]
        g1 = row(norm1_g[i])
        ul_row = _inproj(xl, g1, m_l[0], m_l[1], w_row, i, length).reshape(bsz, length, N_ROW)
        ul_col = _inproj_colmajor(xl, g1, m_l[0], m_l[1], w_col, i, bsz, rows).reshape(bsz, length, N_COL)
        ut_row = _inproj(xt, g1, m_t[0], m_t[1], w_row, i, bsz * lctx).reshape(bsz, lctx, N_ROW)
        ut_col = _inproj(xt, g1, m_t[0], m_t[1], w_col, i, bsz * lctx).reshape(bsz, lctx, N_COL)

        ssd_params = [ssd_conv_w[i], row(ssd_conv_b[i]), _row128(ssd_dt_bias[i]), _row128(ssd_a_log[i]),
                      row(jnp.repeat(ssd_d[i], GROUP_W // SSD_HEADS)), row(ssd_norm_g[i])]
        ssd_bwd = [("u", 1024, OFF_SSD_XBC, True), ("u", 128, OFF_SSD_DT, False)]
        ssd_fwd = [("saved", 1024, 0, False), ("u", 128, OFF_SSD_DT, False), ("u", 512, OFF_SSD_Z, False)]
        ssd_zero = [jnp.zeros((bsz, SSD_HEADS // 2, 128, 128), F32)]
        ya_t, ya_l = _bidir_mixer(_ssd_kernel, ssd_bwd, ssd_fwd, 1024, SSD_CHUNK, ut_row, ul_row, ssd_zero,
                                  ssd_params, "ssd")

        ml_gb = _row128(jnp.concatenate([ml_igate_b[i].reshape(-1), ml_fgate_b[i].reshape(-1)]))
        ml_params = [ml_conv_w[i], row(ml_conv_b[i]), ml_gb, row(ml_norm_g[i])]
        ml_bwd = [("u", 512, OFF_ML_Q, True), ("u", 512, OFF_ML_K, True), ("u", 512, OFF_ML_V, False),
                  ("u", 128, OFF_ML_G, False)]
        ml_fwd = [("saved", 1024, 0, False), ("u", 512, OFF_ML_V, False), ("u", 128, OFF_ML_G, False),
                  ("u", 512, OFF_ML_O, False)]
        ml_zero = [jnp.zeros((bsz, ML_HEADS, 128, 128), F32), jnp.zeros((bsz, 2 * ML_HEADS, 128), F32)]
        yb_t, yb_l = _bidir_mixer(_mlstm_kernel, ml_bwd, ml_fwd, 1024, ML_CHUNK, ut_col, ul_col, ml_zero,
                                  ml_params, "mlstm", F32)

        lru_w = jnp.stack([jnp.concatenate([_block_diag(lru_wa[i, dd]), _block_diag(lru_wx[i, dd])], axis=1)
                           for dd in range(2)]).astype(BF16)
        lru_bias = jnp.concatenate([lru_ba[i], lru_bx[i]], axis=1).reshape(2, 1, 2 * GROUP_W)
        lru_params = [lru_conv_w[i], row(lru_conv_b[i]), lru_w, lru_bias, lru_lambda[i].reshape(2, 1, GROUP_W)]
        lru_bwd = [("u", 512, OFF_LRU_X, True)]
        lru_fwd = [("saved", 512, 0, False), ("u", 512, OFF_LRU_GATE, False)]
        lru_zero = [jnp.zeros((bsz, 1, GROUP_W), F32)]
        lru_scratch = [pltpu.VMEM((bsz, GROUP_W // 128, LRU_CHUNK, 128), F32)] * 2
        yc_t, yc_l = _bidir_mixer(_lru_kernel, lru_bwd, lru_fwd, 512, LRU_CHUNK, ut_row, ul_row, lru_zero,
                                  lru_params, "lru", scratch=lru_scratch)

        wg = jnp.zeros((2, 128, GLA_HEADS * GLA_DK), F32)
        wg = wg.at[0, :GLA_RANK].set(gla_wg2[i, 0]).at[1, GLA_RANK:2 * GLA_RANK].set(gla_wg2[i, 1])
        gla_params = [wg.astype(BF16), gla_bg[i].reshape(2, 1, GLA_HEADS * GLA_DK), row(gla_norm_g[i])]
        gla_bwd = [("u", 512, OFF_GLA_QK, False), ("u", 512, OFF_GLA_V, False), ("u", 128, OFF_GLA_G1, False)]
        gla_fwd = gla_bwd + [("u", 512, OFF_GLA_R, False)]
        gla_zero = [jnp.zeros((bsz, GROUP_W, GLA_HEADS * GLA_DK), F32)]
        yd_t, yd_l = _bidir_mixer(_gla_kernel, gla_bwd, gla_fwd, 0, GLA_CHUNK, ut_col, ul_col, gla_zero,
                                  gla_params, "gla", F32)

        fg = row(final_g)

        def tail(xs, ys, m, rows_per_group, final, colmajor):
            cm = lambda y: y.reshape(bsz, GRID_W, rows, GROUP_W) if colmajor else y.reshape(-1, GROUP_W)
            ys = [ys[0].reshape(-1, GROUP_W), cm(ys[1]), ys[2].reshape(-1, GROUP_W), cm(ys[3])]
            xs, h2 = _outproj(ys, w_out_b, i, xs, m[2], row(norm2_g[i]), m[3], m[4], rows_per_group, colmajor)
            act = _ffn_up(h2, w_gate, w_up, i)
            return _ffn_down(act, w_down_b, i, xs, m[5], fg, final, rows_per_group)

        xl = tail(xl, [ya_l, yb_l, yc_l, yd_l], m_l, length, i == depth - 1, True)
        if need_ctx:
            xt = tail(xt, [ya_t, yb_t, yc_t, yd_t], m_t, bsz * lctx, False, False)
    return xl.reshape(bsz, length, dm)
```

```python
import functools

import jax
import jax.numpy as jnp
from jax import lax
from jax.experimental import pallas as pl
from jax.experimental.pallas import tpu as pltpu

F32 = jnp.float32
BF16 = jnp.bfloat16
EPS = 1e-6
NEG = -1e30

D_MODEL = 2048
GRID_W = 64
GROUP_W = 512
N_MOD = 6
CONV_W = 5
HALO = 8

SSD_HEADS = 8
SSD_CHUNK = 128
ML_HEADS = 4
ML_CHUNK = 64
ML_HEAD_GROUP = 4
LRU_C = 8.0
LRU_CHUNK = 256
LRU_GROUP = 64
GLA_HEADS = 4
GLA_DK = 64
GLA_RANK = 16
GLA_TAU = 16.0
GLA_CHUNK = 64
GLA_SUB = 16

OFF_SSD_XBC = 0
OFF_LRU_X = 1024
OFF_LRU_GATE = 1536
OFF_SSD_Z = 2048
OFF_SSD_DT = 2560
N_ROW = 2688
OFF_ML_Q = 0
OFF_ML_K = 512
OFF_ML_V = 1024
OFF_ML_O = 1536
OFF_GLA_QK = 2048
OFF_GLA_V = 2560
OFF_GLA_R = 3072
OFF_ML_G = 3584
OFF_GLA_G1 = 3712
N_COL = 3840

VMEM_LIMIT = 56 * 1024 * 1024
TM_PROJ = 512
COLS_PER_TILE = 8
ROWS_PER_TILE = 8
TM_OUT = 512
TM_FFN_UP = 2048
TN_FFN_UP = 512
TM_FFN_DOWN = 512


def _cparams(sem):
    return pltpu.CompilerParams(dimension_semantics=sem, vmem_limit_bytes=VMEM_LIMIT)


def _resident(shape, layer=None):
    if layer is None:
        nd = len(shape)
        return pl.BlockSpec(shape, lambda *_: (0,) * nd, pipeline_mode=pl.Buffered(1))
    nd = len(shape)
    return pl.BlockSpec((None,) + tuple(shape), lambda *_: (layer,) + (0,) * nd, pipeline_mode=pl.Buffered(1))


def _sigmoid(x):
    return 0.5 + 0.5 * jnp.tanh(0.5 * x)


def _silu(x):
    return x * _sigmoid(x)


def _softplus(x):
    return jnp.maximum(x, 0.0) + jnp.log(1.0 + jnp.exp(-jnp.abs(x)))


def _log_sigmoid(x):
    return jnp.minimum(x, 0.0) - jnp.log(1.0 + jnp.exp(-jnp.abs(x)))


def _gelu_tanh(x):
    return 0.5 * x * (1.0 + jnp.tanh(0.7978845608028654 * (x + 0.044715 * (x * x * x))))


def _rms(x):
    return x * lax.rsqrt(jnp.mean(x * x, axis=-1, keepdims=True) + EPS)


def _dot(a, b):
    return jnp.dot(a, b, preferred_element_type=F32)


def _dot_nt(a, b):
    return lax.dot_general(a, b, (((1,), (1,)), ((), ())), preferred_element_type=F32)


def _dot_exact(a, b):
    return jnp.dot(a, b, precision=lax.Precision.HIGHEST, preferred_element_type=F32)


def _scan_mask(t, rev):
    ri = lax.broadcasted_iota(jnp.int32, (t, t), 0)
    ci = lax.broadcasted_iota(jnp.int32, (t, t), 1)
    return (ci >= ri) if rev else (ci <= ri)


def _conv5(prev, main, nxt, w, bias, first, last):
    t = main.shape[0]
    prev = jnp.where(first, 0.0, prev)
    nxt = jnp.where(last, 0.0, nxt)
    ext = jnp.concatenate([prev, main, nxt], axis=0)
    n = t + 2 * HALO
    acc = None
    for k in range(CONV_W):
        sh = (CONV_W // 2 - k) % n
        r = ext if sh == 0 else pltpu.roll(ext, sh, axis=0)
        term = r[HALO:HALO + t] * w[k:k + 1, :]
        acc = term if acc is None else acc + term
    return acc + bias


def _mod_kernel(c_ref, w_ref, b_ref, o_ref):
    s = _silu(c_ref[...]).astype(BF16)
    o_ref[...] = _dot(s, w_ref[...].astype(BF16)) + b_ref[...]


def _norm_mod(x, g_ref, sh_ref, sc_ref):
    return ((_rms(x) * g_ref[...]) * (1.0 + sc_ref[0]) + sh_ref[0]).astype(BF16)


def _inproj_kernel(x_ref, g_ref, sh_ref, sc_ref, w_ref, o_ref):
    o_ref[...] = _dot(_norm_mod(x_ref[...], g_ref, sh_ref, sc_ref), w_ref[...])


def _inproj_colmajor_kernel(x_ref, g_ref, sh_ref, sc_ref, w_ref, o_ref, xs_ref):
    rows = x_ref.shape[1]
    for wl in range(COLS_PER_TILE):
        xs_ref[wl * rows:(wl + 1) * rows, :] = x_ref[0, :, wl, :]
    o_ref[...] = _dot(_norm_mod(xs_ref[...], g_ref, sh_ref, sc_ref), w_ref[...])


def _outproj_kernel(colmajor, ya_ref, yb_ref, yc_ref, yd_ref, w_ref, x_ref, gate_ref, g_ref, sh_ref, sc_ref,
                    xo_ref, h_ref):
    if colmajor:
        def rowmajor(ref):
            return jnp.concatenate([ref[0, :, rl, :] for rl in range(ROWS_PER_TILE)], axis=0).astype(BF16)
        yb, yd = rowmajor(yb_ref), rowmajor(yd_ref)
    else:
        yb, yd = yb_ref[...], yd_ref[...]
    acc = _dot(ya_ref[...], w_ref[0])
    acc = acc + _dot(yc_ref[...], w_ref[2])
    acc = acc + _dot(yb, w_ref[1])
    acc = acc + _dot(yd, w_ref[3])
    xn = x_ref[...] + gate_ref[0] * acc
    xo_ref[...] = xn
    y = _rms(xn) * g_ref[...]
    h_ref[...] = (y * (1.0 + sc_ref[0]) + sh_ref[0]).astype(BF16)


def _ffn_up_kernel(h_ref, wg_ref, wu_ref, o_ref):
    h = h_ref[...]
    g = _dot(h, wg_ref[...].astype(BF16))
    u = _dot(h, wu_ref[...].astype(BF16))
    o_ref[...] = (_silu(g) * u).astype(BF16)


def _ffn_down_kernel(final, a_ref, w_ref, x_ref, gate_ref, fg_ref, o_ref):
    xn = x_ref[...] + gate_ref[0] * _dot(a_ref[...], w_ref[...])
    if final:
        xn = _rms(xn) * fg_ref[...]
    o_ref[...] = xn


def _ssd_kernel(rev, finish, nchunks, nb, *refs):
    if finish:
        (xc_ref, dt_ref, z_ref, yp_ref, s0_ref, cw_ref, cb_ref, dtb_ref, alog_ref,
         dsk_ref, ng_ref, y_ref, s_ref) = refs
    else:
        (xbc_ref, xp_ref, xn_ref, dt_ref, s0_ref, cw_ref, cb_ref, dtb_ref, alog_ref,
         dsk_ref, ng_ref, y_ref, xc_ref, s_ref) = refs
    t = SSD_CHUNK
    d = 1 if rev else 0
    j = pl.program_id(0)
    c = (nchunks - 1 - j) if rev else j

    @pl.when(j == 0)
    def _():
        s_ref[...] = s0_ref[...]

    mask = _scan_mask(t, rev)
    maskf = mask.astype(F32)
    er = 0 if rev else t - 1
    lo = lax.broadcasted_iota(jnp.int32, (t, 128), 1) < 64
    lo_state = lax.broadcasted_iota(jnp.int32, s_ref.shape[2:], 1) < 64
    cw, cbias = cw_ref[...], cb_ref[...]
    nega = -jnp.exp(alog_ref[...])

    bs = range(nb)
    if finish:
        xbc = [xc_ref[bi] for bi in bs]
    else:
        xbc = [_silu(_conv5(xp_ref[bi], xbc_ref[bi], xn_ref[bi], cw, cbias, c == 0, c == nchunks - 1))
               for bi in bs]
        for bi in bs:
            xc_ref[bi] = xbc[bi]
    dt_all = [_softplus(dt_ref[bi] + dtb_ref[...]) for bi in bs]
    b_all = [_dot_exact(maskf, dt_all[bi] * nega) for bi in bs]
    b_t = [b_all[bi].T for bi in bs]
    dt_t = [dt_all[bi].T for bi in bs]
    ys = [[None] * (SSD_HEADS // 2) for _ in bs]
    bm_t = [[xbc[bi][:, 512 + 128 * g:640 + 128 * g].T.astype(BF16) for bi in bs] for g in range(2)]
    for p in range(SSD_HEADS // 2):
        g = p // 2
        bm = [xbc[bi][:, 512 + 128 * g:640 + 128 * g].astype(BF16) for bi in bs]
        cm = [xbc[bi][:, 768 + 128 * g:896 + 128 * g].astype(BF16) for bi in bs]
        cb = [_dot_nt(cm[bi], bm[bi]) for bi in bs]
        xpair = [xbc[bi][:, 128 * p:128 * p + 128] for bi in bs]
        xpair_b = [xpair[bi].astype(BF16) for bi in bs]
        s_pair = [s_ref[bi, p] for bi in bs]
        y_inter = [_dot(cm[bi], s_pair[bi].astype(BF16)) for bi in bs]
        yh, eb, coef, dec = [], [], [], []
        for hh in range(2):
            l = d * SSD_HEADS + 2 * p + hh
            bcol = [b_all[bi][:, l:l + 1] for bi in bs]
            seg = [jnp.exp(jnp.where(mask, bcol[bi] - b_t[bi][l:l + 1, :], NEG)) for bi in bs]
            w = [(cb[bi] * seg[bi] * dt_t[bi][l:l + 1, :]).astype(BF16) for bi in bs]
            yh.append([_dot(w[bi], xpair_b[bi]) for bi in bs])
            eb.append([jnp.exp(bcol[bi]) for bi in bs])
            bl = [b_all[bi][er:er + 1, l:l + 1] for bi in bs]
            coef.append([jnp.exp(bl[bi] - bcol[bi]) * dt_all[bi][:, l:l + 1] for bi in bs])
            dec.append([jnp.exp(bl[bi]) for bi in bs])
        xw = [(xpair[bi] * jnp.where(lo, coef[0][bi], coef[1][bi])).astype(BF16) for bi in bs]
        upd = [_dot(bm_t[g][bi], xw[bi]) for bi in bs]
        for bi in bs:
            ys[bi][p] = (jnp.where(lo, yh[0][bi], yh[1][bi])
                         + y_inter[bi] * jnp.where(lo, eb[0][bi], eb[1][bi]))
            s_ref[bi, p] = jnp.where(lo_state, dec[0][bi], dec[1][bi]) * s_pair[bi] + upd[bi]
    for bi in bs:
        y = jnp.concatenate(ys[bi], axis=1)
        if finish:
            y = y + yp_ref[bi] + dsk_ref[...] * xbc[bi][:, :GROUP_W]
            y = _rms(y * _silu(z_ref[bi])) * ng_ref[...]
            y_ref[bi] = y.astype(y_ref.dtype)
        else:
            y_ref[bi] = y


def _mlstm_kernel(rev, finish, nchunks, nb, *refs):
    if finish:
        (qkc_ref, v_ref, g_ref, o_ref, yp_ref, cs0_ref, sm0_ref,
         cw_ref, cb_ref, gb_ref, ng_ref, y_ref, cs_ref, sm_ref) = refs
    else:
        (q_ref, qp_ref, qn_ref, k_ref, kp_ref, kn_ref, v_ref, g_ref, cs0_ref, sm0_ref,
         cw_ref, cb_ref, gb_ref, ng_ref, y_ref, qkc_ref, cs_ref, sm_ref) = refs
    t = ML_CHUNK
    d = 1 if rev else 0
    j = pl.program_id(0)
    c = (nchunks - 1 - j) if rev else j
    first, last = c == 0, c == nchunks - 1

    @pl.when(j == 0)
    def _():
        cs_ref[...] = cs0_ref[...]
        sm_ref[...] = sm0_ref[...]

    cw = cw_ref[...]
    cbias = cb_ref[...]
    maskf = _scan_mask(t, rev).astype(F32)
    mask_t = _scan_mask(t, not rev)
    er = 0 if rev else t - 1

    bs = range(nb)
    if finish:
        q = [qkc_ref[bi][:, :GROUP_W] for bi in bs]
        k = [qkc_ref[bi][:, GROUP_W:] for bi in bs]
    else:
        q = [_silu(_conv5(qp_ref[bi], q_ref[bi], qn_ref[bi], cw[:, :GROUP_W], cbias[:, :GROUP_W], first, last))
             for bi in bs]
        k = [_silu(_conv5(kp_ref[bi], k_ref[bi], kn_ref[bi], cw[:, GROUP_W:], cbias[:, GROUP_W:], first, last))
             * (128.0 ** -0.5) for bi in bs]
        for bi in bs:
            qkc_ref[bi, :, :GROUP_W] = q[bi]
            qkc_ref[bi, :, GROUP_W:] = k[bi]
    v = [v_ref[bi] for bi in bs]
    gts = [g_ref[bi] + gb_ref[...] for bi in bs]
    b_all = [_dot_exact(maskf, _log_sigmoid(gts[bi])) for bi in bs]
    g_t = [gts[bi].T for bi in bs]
    b_t = [b_all[bi].T for bi in bs]
    ys = [[None] * ML_HEADS for _ in bs]
    for h0 in range(0, ML_HEADS, ML_HEAD_GROUP):
        ch = [(bi, h) for h in range(h0, h0 + ML_HEAD_GROUP) for bi in bs]
        cs_ = range(len(ch))
        li = [d * ML_HEADS + h for _, h in ch]
        lf = [2 * ML_HEADS + l for l in li]
        sl = [slice(128 * h, 128 * h + 128) for _, h in ch]
        brow = [b_t[bi][lf[c]:lf[c] + 1, :] for c, (bi, h) in enumerate(ch)]
        igrow = [g_t[bi][li[c]:li[c] + 1, :] for c, (bi, h) in enumerate(ch)]
        ccol = [gts[bi][:, li[c]:li[c] + 1] - b_all[bi][:, lf[c]:lf[c] + 1]
                for c, (bi, h) in enumerate(ch)]
        m_prev = [sm_ref[bi, ML_HEADS + h:ML_HEADS + h + 1, 0:1] for bi, h in ch]
        ns = [sm_ref[bi, h:h + 1, :] for bi, h in ch]
        cs = [cs_ref[bi, h] for bi, h in ch]
        dmat = [jnp.where(mask_t, brow[c] + ccol[c], NEG) for c in cs_]
        inter = [brow[c] + m_prev[c] for c in cs_]
        mt = [jnp.maximum(inter[c], jnp.max(dmat[c], axis=0, keepdims=True)) for c in cs_]
        w = [jnp.exp(dmat[c] - mt[c]) for c in cs_]
        sc = [jnp.exp(inter[c] - mt[c]) for c in cs_]
        qb = [q[bi][:, sl[c]].astype(BF16) for c, (bi, h) in enumerate(ch)]
        kb = [k[bi][:, sl[c]].astype(BF16) for c, (bi, h) in enumerate(ch)]
        sw = [_dot_nt(kb[c], qb[c]) * w[c] for c in cs_]
        v_t = [v[bi][:, sl[c]].T for c, (bi, h) in enumerate(ch)]
        num = [_dot(v_t[c].astype(BF16), sw[c].astype(BF16)) for c in cs_]
        qc = [_dot_nt(cs[c].astype(BF16), qb[c]) for c in cs_]
        qn = [_dot_nt(jnp.broadcast_to(ns[c], (8, 128)).astype(BF16), qb[c])[0:1] for c in cs_]
        den = [jnp.sum(sw[c], axis=0, keepdims=True) + sc[c] * qn[c] for c in cs_]
        for c, (bi, h) in enumerate(ch):
            inv = 1.0 / jnp.maximum(jnp.abs(den[c]), jnp.exp(-mt[c]))
            ys[bi][h] = ((num[c] + sc[c] * qc[c]) * inv).T
        bl = [b_all[bi][er:er + 1, lf[c]:lf[c] + 1] for c, (bi, h) in enumerate(ch)]
        tail = [bl[c] - brow[c] + igrow[c] for c in cs_]
        m_new = [jnp.maximum(bl[c] + m_prev[c], jnp.max(tail[c], axis=1, keepdims=True)) for c in cs_]
        ws = [jnp.exp(tail[c] - m_new[c]) for c in cs_]
        sc_end = [jnp.exp(bl[c] + m_prev[c] - m_new[c]) for c in cs_]
        upd = [_dot((v_t[c] * ws[c]).astype(BF16), kb[c]) for c in cs_]
        nup = [_dot(jnp.broadcast_to(ws[c], (8, t)).astype(BF16), kb[c])[0:1] for c in cs_]
        for c, (bi, h) in enumerate(ch):
            cs_ref[bi, h] = sc_end[c] * cs[c] + upd[c]
            sm_ref[bi, h:h + 1, :] = sc_end[c] * ns[c] + nup[c]
            sm_ref[bi, ML_HEADS + h:ML_HEADS + h + 1, :] = jnp.broadcast_to(m_new[c], (1, 128))
    for bi in bs:
        if finish:
            yp = yp_ref[bi]
            outs = [_rms(ys[bi][h] + yp[:, 128 * h:128 * h + 128]) for h in range(ML_HEADS)]
            y = jnp.concatenate(outs, axis=1) * ng_ref[...]
            y_ref[bi] = (_sigmoid(o_ref[bi]) * y).astype(y_ref.dtype)
        else:
            y_ref[bi] = jnp.concatenate(ys[bi], axis=1)


def _lru_kernel(rev, finish, nchunks, nb, *refs):
    if finish:
        (xf_ref, gate_ref, yp_ref, h0_ref, cw_ref, cb_ref, w_ref, bias_ref, lam_ref,
         y_ref, h_ref, sa_ref, sh_ref) = refs
    else:
        (x_ref, xp_ref, xn_ref, h0_ref, cw_ref, cb_ref, w_ref, bias_ref, lam_ref,
         y_ref, xf_ref, h_ref, sa_ref, sh_ref) = refs
    t = LRU_CHUNK
    d = 1 if rev else 0
    j = pl.program_id(0)
    c = (nchunks - 1 - j) if rev else j

    @pl.when(j == 0)
    def _():
        h_ref[...] = h0_ref[...]

    cw, cbias = cw_ref[...], cb_ref[...]
    lsl = LRU_C * _log_sigmoid(lam_ref[d])
    sub = lax.broadcasted_iota(jnp.int32, (8, 128), 0)
    ng = GROUP_W // 128

    for bi in range(nb):
        if finish:
            xf = xf_ref[bi]
        else:
            xf = _conv5(xp_ref[bi], x_ref[bi], xn_ref[bi], cw, cbias, c == 0, c == nchunks - 1)
            xf_ref[bi] = xf
        pre = _dot(xf.astype(BF16), w_ref[d]) + bias_ref[d]
        r = _sigmoid(pre[:, :GROUP_W])
        ig = _sigmoid(pre[:, GROUP_W:])
        loga = r * lsl
        a = jnp.exp(loga)
        bx = jnp.sqrt(-jnp.tanh(loga) * (1.0 + a * a)) * ig * xf
        for g in range(ng):
            sa_ref[bi, g] = a[:, 128 * g:128 * g + 128]
            sh_ref[bi, g] = bx[:, 128 * g:128 * g + 128]

    chains = [(bi, g) for bi in range(nb) for g in range(ng)]
    carry = [h_ref[bi, :, 128 * g:128 * g + 128] for bi, g in chains]
    for grp in (range(t // LRU_GROUP - 1, -1, -1) if rev else range(t // LRU_GROUP)):
        base = LRU_GROUP * grp
        a_run, h_run = [None] * len(chains), [None] * len(chains)
        a_cum = [[None] * 8 for _ in chains]
        h_loc = [[None] * 8 for _ in chains]
        for v in (range(7, -1, -1) if rev else range(8)):
            for ci, (bi, g) in enumerate(chains):
                a_v = sa_ref[bi, g, pl.ds(base + v, 8, stride=8), :]
                b_v = sh_ref[bi, g, pl.ds(base + v, 8, stride=8), :]
                if a_run[ci] is None:
                    a_run[ci], h_run[ci] = a_v, b_v
                else:
                    h_run[ci] = a_v * h_run[ci] + b_v
                    a_run[ci] = a_v * a_run[ci]
                a_cum[ci][v], h_loc[ci][v] = a_run[ci], h_run[ci]
        for ci, (bi, g) in enumerate(chains):
            pa, ph = a_run[ci], h_run[ci]
            for sh in (1, 2, 4):
                if rev:
                    valid = sub < 8 - sh
                    a_s = pltpu.roll(pa, 8 - sh, axis=0)
                    h_s = pltpu.roll(ph, 8 - sh, axis=0)
                else:
                    valid = sub >= sh
                    a_s = pltpu.roll(pa, sh, axis=0)
                    h_s = pltpu.roll(ph, sh, axis=0)
                ph = jnp.where(valid, pa * h_s + ph, ph)
                pa = jnp.where(valid, pa * a_s, pa)
            after = pa * carry[ci] + ph
            if rev:
                cin = jnp.where(sub == 7, carry[ci], pltpu.roll(after, 7, axis=0))
                carry[ci] = after[0:1]
            else:
                cin = jnp.where(sub == 0, carry[ci], pltpu.roll(after, 1, axis=0))
                carry[ci] = after[7:8]
            for v in range(8):
                sh_ref[bi, g, pl.ds(base + v, 8, stride=8), :] = h_loc[ci][v] + a_cum[ci][v] * cin
    for ci, (bi, g) in enumerate(chains):
        h_ref[bi, :, 128 * g:128 * g + 128] = carry[ci]
    for bi in range(nb):
        hs = jnp.concatenate([sh_ref[bi, g] for g in range(ng)], axis=1)
        if finish:
            y_ref[bi] = ((hs + yp_ref[bi]) * _gelu_tanh(gate_ref[bi])).astype(y_ref.dtype)
        else:
            y_ref[bi] = hs


def _gla_kernel(rev, finish, nchunks, nb, *refs):
    if finish:
        (qk_ref, v_ref, g1_ref, r_ref, yp_ref, s0_ref, wg_ref, bg_ref, ng_ref, y_ref, s_ref) = refs
    else:
        (qk_ref, v_ref, g1_ref, s0_ref, wg_ref, bg_ref, ng_ref, y_ref, s_ref) = refs
    t = GLA_CHUNK
    d = 1 if rev else 0
    dkk = GLA_HEADS * GLA_DK
    j = pl.program_id(0)

    @pl.when(j == 0)
    def _():
        s_ref[...] = s0_ref[...]

    mask = _scan_mask(t, rev)
    maskf = mask.astype(F32)
    rows = lax.broadcasted_iota(jnp.int32, (t, 1), 0)
    lane_head = lax.broadcasted_iota(jnp.int32, (GLA_SUB, dkk), 1) // GLA_DK
    blk = (lax.broadcasted_iota(jnp.int32, (GROUP_W, dkk), 0) // 128
           == lax.broadcasted_iota(jnp.int32, (GROUP_W, dkk), 1) // GLA_DK)
    er = 0 if rev else t - 1

    bs = range(nb)
    q = [qk_ref[bi][:, :dkk] * (GLA_DK ** -0.5) for bi in bs]
    k = [qk_ref[bi][:, dkk:] for bi in bs]
    v = [v_ref[bi] for bi in bs]
    vb = [v[bi].astype(BF16) for bi in bs]
    glog = [_dot(g1_ref[bi].astype(BF16), wg_ref[d]) + bg_ref[d] for bi in bs]
    la = [_log_sigmoid(glog[bi]) * (1.0 / GLA_TAU) for bi in bs]
    b = [_dot_exact(maskf, la[bi]) for bi in bs]
    excl = [b[bi] - la[bi] for bi in bs]
    s_t = [s_ref[bi] for bi in bs]
    o_inter = [_dot_nt((q[bi] * jnp.exp(b[bi])).astype(BF16), s_t[bi].astype(BF16)) for bi in bs]

    att_blocks = [[None] * (t // GLA_SUB) for _ in bs]
    for i in range(t // GLA_SUB):
        r0 = GLA_SUB * i
        if rev:
            ref_row = [excl[bi][r0 + GLA_SUB - 1:r0 + GLA_SUB] for bi in bs]
            kvalid = rows >= r0
        else:
            ref_row = [excl[bi][r0:r0 + 1] for bi in bs]
            kvalid = rows < r0 + GLA_SUB
        qi = [q[bi][r0:r0 + GLA_SUB] * jnp.exp(b[bi][r0:r0 + GLA_SUB] - ref_row[bi]) for bi in bs]
        ki = [(k[bi] * jnp.exp(jnp.where(kvalid, ref_row[bi] - b[bi], NEG))).astype(BF16) for bi in bs]
        qs = [jnp.concatenate([jnp.where(lane_head == h, qi[bi], 0.0) for h in range(GLA_HEADS)],
                              axis=0).astype(BF16) for bi in bs]
        for bi in bs:
            att_blocks[bi][i] = _dot_nt(qs[bi], ki[bi])
    outs = [[None] * GLA_HEADS for _ in bs]
    for h in range(GLA_HEADS):
        att = [jnp.where(mask, jnp.concatenate([ab[GLA_SUB * h:GLA_SUB * h + GLA_SUB] for ab in att_blocks[bi]],
                                               axis=0), 0.0).astype(BF16) for bi in bs]
        for bi in bs:
            outs[bi][h] = _dot(att[bi], vb[bi][:, 128 * h:128 * h + 128])

    bl = [b[bi][er:er + 1] for bi in bs]
    kd = [(k[bi] * jnp.exp(bl[bi] - b[bi])).astype(BF16) for bi in bs]
    v_t = [jnp.concatenate([v[bi][:, 128 * h:128 * h + 128].T for h in range(GLA_HEADS)], axis=0).astype(BF16)
           for bi in bs]
    upd = [_dot(v_t[bi], kd[bi]) for bi in bs]
    for bi in bs:
        s_ref[bi] = s_t[bi] * jnp.exp(bl[bi]) + jnp.where(blk, upd[bi], 0.0)
    for bi in bs:
        o = jnp.concatenate(outs[bi], axis=1) + o_inter[bi]
        if finish:
            o = o + yp_ref[bi]
            o = jnp.concatenate([_rms(o[:, 128 * h:128 * h + 128]) for h in range(GLA_HEADS)], axis=1)
            y_ref[bi] = (o * ng_ref[...] * _silu(r_ref[bi])).astype(y_ref.dtype)
        else:
            y_ref[bi] = o


def _chunk_specs(nb, width, off, t, length, rev, nchunks, halo):
    ob = off // width
    tb = t // HALO

    def cj(j):
        return (nchunks - 1 - j) if rev else j

    main = pl.BlockSpec((nb, t, width), lambda j: (0, cj(j), ob))
    if not halo:
        return [main]
    prev = pl.BlockSpec((nb, HALO, width), lambda j: (0, jnp.maximum(cj(j) * tb - 1, 0), ob))
    nxt = pl.BlockSpec((nb, HALO, width), lambda j: (0, jnp.minimum((cj(j) + 1) * tb, length // HALO - 1), ob))
    return [main, prev, nxt]


def _run_mixer(body, pieces, t, rev, finish, ypart, states, params, name, save_width=0, out_dtype=BF16,
               scratch=()):
    nb, length = pieces[0][0].shape[0], pieces[0][0].shape[1]
    nchunks = length // t
    cj = (lambda j: nchunks - 1 - j) if rev else (lambda j: j)
    row_spec = lambda width: pl.BlockSpec((nb, t, width), lambda j: (0, cj(j), 0))
    in_specs, args = [], []
    for arr, width, off, halo in pieces:
        sp = _chunk_specs(nb, width, off, t, length, rev, nchunks, halo)
        in_specs += sp
        args += [arr] * len(sp)
    if finish:
        in_specs.append(row_spec(GROUP_W))
        args.append(ypart)
    for a in list(states) + list(params):
        in_specs.append(pl.BlockSpec(a.shape, lambda j, _nd=a.ndim: (0,) * _nd))
        args.append(a)
    out_shape = [jax.ShapeDtypeStruct((nb, length, GROUP_W), out_dtype if finish else F32)]
    out_specs = [row_spec(GROUP_W)]
    if save_width:
        out_shape.append(jax.ShapeDtypeStruct((nb, length, save_width), F32))
        out_specs.append(row_spec(save_width))
    for s in states:
        out_shape.append(jax.ShapeDtypeStruct(s.shape, s.dtype))
        out_specs.append(pl.BlockSpec(s.shape, lambda j, _nd=s.ndim: (0,) * _nd))
    res = pl.pallas_call(
        functools.partial(body, rev, finish, nchunks, nb),
        grid=(nchunks,),
        in_specs=in_specs,
        out_specs=out_specs,
        out_shape=out_shape,
        scratch_shapes=list(scratch),
        compiler_params=_cparams(("arbitrary",)),
        name=name,
    )(*args)
    nfix = 2 if save_width else 1
    return res[0], (res[1] if save_width else None), list(res[nfix:])


def _bidir_mixer(body, bwd_pieces, fwd_pieces, save_width, t, u_ctx, u_lat, zero_states, params, name,
                 lat_dtype=BF16, scratch=()):
    def run(u, rev, ypart, saved, states, tag, out_dtype=BF16):
        src = {"u": u, "saved": saved}
        pieces = [(src[s], w, o, h) for s, w, o, h in (bwd_pieces if rev else fwd_pieces)]
        return _run_mixer(body, pieces, t, rev, not rev, ypart, states, params, name + tag,
                          save_width if rev else 0, out_dtype, scratch)

    yb_c, sv_c, st_b = run(u_ctx, True, None, None, zero_states, "_ctx_bwd")
    y_c, _, st_f = run(u_ctx, False, yb_c, sv_c, zero_states, "_ctx_fwd")
    yb_l, sv_l, _ = run(u_lat, True, None, None, st_b, "_lat_bwd")
    y_l, _, _ = run(u_lat, False, yb_l, sv_l, st_f, "_lat_fwd", lat_dtype)
    return y_c, y_l


def _mods(cvec, w_mod, b_mod):
    depth, dm, nm = w_mod.shape
    tn = 1024
    return pl.pallas_call(
        _mod_kernel,
        grid=(depth, nm // tn),
        in_specs=[pl.BlockSpec((8, dm), lambda l, j: (0, 0)),
                  pl.BlockSpec((None, dm, tn), lambda l, j: (l, 0, j)),
                  pl.BlockSpec((None, 1, tn), lambda l, j: (l, 0, j))],
        out_specs=pl.BlockSpec((None, 8, tn), lambda l, j: (l, 0, j)),
        out_shape=jax.ShapeDtypeStruct((depth, 8, nm), F32),
        compiler_params=_cparams(("parallel", "parallel")),
        name="mods",
    )(cvec, w_mod, b_mod.reshape(depth, 1, nm))


def _inproj(x, g, shift, scale, w, layer, rows_per_group):
    m, dm = x.shape
    n = w.shape[2]
    tm = TM_PROJ
    grp = lambda i: (i * tm // rows_per_group, 0, 0)
    return pl.pallas_call(
        _inproj_kernel,
        grid=(m // tm,),
        in_specs=[pl.BlockSpec((tm, dm), lambda i: (i, 0)),
                  _resident((1, dm)),
                  pl.BlockSpec((1, 1, dm), grp),
                  pl.BlockSpec((1, 1, dm), grp),
                  _resident((dm, n), layer)],
        out_specs=pl.BlockSpec((tm, n), lambda i: (i, 0)),
        out_shape=jax.ShapeDtypeStruct((m, n), F32),
        compiler_params=_cparams(("parallel",)),
        name="inproj",
    )(x, g, shift, scale, w)


def _outproj(ys, w, layer, x, gate, g, shift, scale, rows_per_group, colmajor):
    m, dm = x.shape
    tm = min(TM_OUT, rows_per_group)
    grp = lambda i: (i * tm // rows_per_group, 0, 0)
    yspec = pl.BlockSpec((tm, GROUP_W), lambda i: (i, 0))
    if colmajor:
        assert tm == ROWS_PER_TILE * GRID_W
        tiles = rows_per_group // tm
        cspec = pl.BlockSpec((1, GRID_W, ROWS_PER_TILE, GROUP_W), lambda i: (i // tiles, 0, i % tiles, 0))
    else:
        cspec = yspec
    return pl.pallas_call(
        functools.partial(_outproj_kernel, colmajor),
        grid=(m // tm,),
        in_specs=[yspec, cspec, yspec, cspec,
                  _resident((4, GROUP_W, dm), layer),
                  pl.BlockSpec((tm, dm), lambda i: (i, 0)),
                  pl.BlockSpec((1, 1, dm), grp),
                  _resident((1, dm)),
                  pl.BlockSpec((1, 1, dm), grp),
                  pl.BlockSpec((1, 1, dm), grp)],
        out_specs=[pl.BlockSpec((tm, dm), lambda i: (i, 0)), pl.BlockSpec((tm, dm), lambda i: (i, 0))],
        out_shape=[jax.ShapeDtypeStruct((m, dm), F32), jax.ShapeDtypeStruct((m, dm), BF16)],
        compiler_params=_cparams(("parallel",)),
        name="outproj",
    )(*ys, w, x, gate, g, shift, scale)


def _inproj_colmajor(x, g, shift, scale, w, layer, bsz, rows):
    dm = x.shape[1]
    n = w.shape[2]
    tiles = GRID_W // COLS_PER_TILE
    tm = COLS_PER_TILE * rows
    grp = lambda i: (i // tiles, 0, 0)
    return pl.pallas_call(
        _inproj_colmajor_kernel,
        grid=(bsz * tiles,),
        in_specs=[pl.BlockSpec((1, rows, COLS_PER_TILE, dm), lambda i: (i // tiles, 0, i % tiles, 0)),
                  _resident((1, dm)),
                  pl.BlockSpec((1, 1, dm), grp),
                  pl.BlockSpec((1, 1, dm), grp),
                  _resident((dm, n), layer)],
        out_specs=pl.BlockSpec((tm, n), lambda i: (i, 0)),
        out_shape=jax.ShapeDtypeStruct((bsz * rows * GRID_W, n), F32),
        scratch_shapes=[pltpu.VMEM((tm, dm), F32)],
        compiler_params=_cparams(("parallel",)),
        name="inproj_colmajor",
    )(x.reshape(bsz, rows, GRID_W, dm), g, shift, scale, w)


def _ffn_up(h, wg, wu, layer):
    m, dm = h.shape
    dff = wg.shape[2]
    tm = min(TM_FFN_UP, m)
    tn = TN_FFN_UP
    return pl.pallas_call(
        _ffn_up_kernel,
        grid=(m // tm, dff // tn),
        in_specs=[pl.BlockSpec((tm, dm), lambda i, j: (i, 0)),
                  pl.BlockSpec((None, dm, tn), lambda i, j: (layer, 0, j)),
                  pl.BlockSpec((None, dm, tn), lambda i, j: (layer, 0, j))],
        out_specs=pl.BlockSpec((tm, tn), lambda i, j: (i, j)),
        out_shape=jax.ShapeDtypeStruct((m, dff), BF16),
        compiler_params=_cparams(("parallel", "arbitrary")),
        name="ffn_up",
    )(h, wg, wu)


def _ffn_down(a, w, layer, x, gate, final_g, final, rows_per_group):
    m, dm = x.shape
    dff = a.shape[1]
    tm = TM_FFN_DOWN
    grp = lambda i: (i * tm // rows_per_group, 0, 0)
    return pl.pallas_call(
        functools.partial(_ffn_down_kernel, final),
        grid=(m // tm,),
        in_specs=[pl.BlockSpec((tm, dff), lambda i: (i, 0)),
                  _resident((dff, dm), layer),
                  pl.BlockSpec((tm, dm), lambda i: (i, 0)),
                  pl.BlockSpec((1, 1, dm), grp),
                  _resident((1, dm))],
        out_specs=pl.BlockSpec((tm, dm), lambda i: (i, 0)),
        out_shape=jax.ShapeDtypeStruct((m, dm), F32),
        compiler_params=_cparams(("parallel",)),
        name="ffn_down",
    )(a, w, x, gate, final_g)


def _pad_cols(a, width):
    return jnp.pad(a, ((0, 0), (0, width - a.shape[1])))


def _split_w_in(w):
    ssd, ml, lru, gla = jnp.split(w, [1552, 1552 + 2064, 1552 + 2064 + 1024], axis=2)
    ssd_z, ssd_xbc, ssd_dt = ssd[..., :512], ssd[..., 512:1536], ssd[..., 1536:]
    ml_qkvo, ml_g = ml[..., :2048], ml[..., 2048:]
    lru_gate, lru_x = lru[..., :512], lru[..., 512:]
    gla_qkvr, gla_g1 = gla[..., :1536], gla[..., 1536:]
    pad = lambda a: jnp.pad(a, ((0, 0), (0, 0), (0, 128 - a.shape[2])))
    w_row = jnp.concatenate([ssd_xbc, lru_x, lru_gate, ssd_z, pad(ssd_dt)], axis=2)
    w_col = jnp.concatenate([ml_qkvo, gla_qkvr, pad(ml_g), pad(gla_g1)], axis=2)
    return w_row.astype(BF16), w_col.astype(BF16)


def _row128(a):
    return _pad_cols(a.reshape(1, -1).astype(F32), 128)


def _block_diag(w):
    nb, bi, bj = w.shape
    eye = jnp.eye(nb, dtype=w.dtype)
    return (eye[:, None, :, None] * w[:, :, None, :]).reshape(nb * bi, nb * bj)


def kernel(x, c, ctx, c_ctx, norm1_g, norm2_g, w_mod, b_mod, w_in, w_out, ssd_conv_w, ssd_conv_b, ssd_dt_bias, ssd_a_log, ssd_d, ssd_norm_g, ml_conv_w, ml_conv_b, ml_igate_b, ml_fgate_b, ml_norm_g, lru_conv_w, lru_conv_b, lru_wa, lru_ba, lru_wx, lru_bx, lru_lambda, gla_wg2, gla_bg, gla_norm_g, w_gate, w_up, w_down, final_g):
    bsz, length, dm = x.shape
    lctx = ctx.shape[1]
    depth = w_in.shape[0]
    rows = length // GRID_W
    assert dm == D_MODEL and length % SSD_CHUNK == 0 and lctx % SSD_CHUNK == 0
    assert length % LRU_CHUNK == 0 and lctx % LRU_CHUNK == 0 and lctx % ML_CHUNK == 0
    assert rows == ML_CHUNK == GLA_CHUNK

    cvec = jnp.concatenate([c, c_ctx[None, :], jnp.zeros((8 - bsz - 1, dm), F32)], axis=0)
    mods = _mods(cvec, w_mod, b_mod)

    xl = x.reshape(bsz * length, dm)
    xt = ctx.reshape(bsz * lctx, dm)
    row = lambda a: a.reshape(1, -1).astype(F32)
    w_row, w_col = _split_w_in(w_in)
    w_out_b = w_out.astype(BF16).reshape(depth, 4, GROUP_W, dm)
    w_down_b = w_down.astype(BF16)

    for i in range(depth):
        need_ctx = i < depth - 1
        m_l = [mods[i, :bsz, k * dm:(k + 1) * dm].reshape(bsz, 1, dm) for k in range(N_MOD)]
        m_t = [mods[i, bsz:bsz + 1, k * dm:(k + 1) * dm].reshape(1, 1, dm) for k in range(N_MOD)]
        g1 = row(norm1_g[i])
        ul_row = _inproj(xl, g1, m_l[0], m_l[1], w_row, i, length).reshape(bsz, length, N_ROW)
        ul_col = _inproj_colmajor(xl, g1, m_l[0], m_l[1], w_col, i, bsz, rows).reshape(bsz, length, N_COL)
        ut_row = _inproj(xt, g1, m_t[0], m_t[1], w_row, i, bsz * lctx).reshape(bsz, lctx, N_ROW)
        ut_col = _inproj(xt, g1, m_t[0], m_t[1], w_col, i, bsz * lctx).reshape(bsz, lctx, N_COL)

        ssd_params = [ssd_conv_w[i], row(ssd_conv_b[i]), _row128(ssd_dt_bias[i]), _row128(ssd_a_log[i]),
                      row(jnp.repeat(ssd_d[i], GROUP_W // SSD_HEADS)), row(ssd_norm_g[i])]
        ssd_bwd = [("u", 1024, OFF_SSD_XBC, True), ("u", 128, OFF_SSD_DT, False)]
        ssd_fwd = [("saved", 1024, 0, False), ("u", 128, OFF_SSD_DT, False), ("u", 512, OFF_SSD_Z, False)]
        ssd_zero = [jnp.zeros((bsz, SSD_HEADS // 2, 128, 128), F32)]
        ya_t, ya_l = _bidir_mixer(_ssd_kernel, ssd_bwd, ssd_fwd, 1024, SSD_CHUNK, ut_row, ul_row, ssd_zero,
                                  ssd_params, "ssd")

        ml_gb = _row128(jnp.concatenate([ml_igate_b[i].reshape(-1), ml_fgate_b[i].reshape(-1)]))
        ml_params = [ml_conv_w[i], row(ml_conv_b[i]), ml_gb, row(ml_norm_g[i])]
        ml_bwd = [("u", 512, OFF_ML_Q, True), ("u", 512, OFF_ML_K, True), ("u", 512, OFF_ML_V, False),
                  ("u", 128, OFF_ML_G, False)]
        ml_fwd = [("saved", 1024, 0, False), ("u", 512, OFF_ML_V, False), ("u", 128, OFF_ML_G, False),
                  ("u", 512, OFF_ML_O, False)]
        ml_zero = [jnp.zeros((bsz, ML_HEADS, 128, 128), F32), jnp.zeros((bsz, 2 * ML_HEADS, 128), F32)]
        yb_t, yb_l = _bidir_mixer(_mlstm_kernel, ml_bwd, ml_fwd, 1024, ML_CHUNK, ut_col, ul_col, ml_zero,
                                  ml_params, "mlstm", F32)

        lru_w = jnp.stack([jnp.concatenate([_block_diag(lru_wa[i, dd]), _block_diag(lru_wx[i, dd])], axis=1)
                           for dd in range(2)]).astype(BF16)
        lru_bias = jnp.concatenate([lru_ba[i], lru_bx[i]], axis=1).reshape(2, 1, 2 * GROUP_W)
        lru_params = [lru_conv_w[i], row(lru_conv_b[i]), lru_w, lru_bias, lru_lambda[i].reshape(2, 1, GROUP_W)]
        lru_bwd = [("u", 512, OFF_LRU_X, True)]
        lru_fwd = [("saved", 512, 0, False), ("u", 512, OFF_LRU_GATE, False)]
        lru_zero = [jnp.zeros((bsz, 1, GROUP_W), F32)]
        lru_scratch = [pltpu.VMEM((bsz, GROUP_W // 128, LRU_CHUNK, 128), F32)] * 2
        yc_t, yc_l = _bidir_mixer(_lru_kernel, lru_bwd, lru_fwd, 512, LRU_CHUNK, ut_row, ul_row, lru_zero,
                                  lru_params, "lru", scratch=lru_scratch)

        wg = jnp.zeros((2, 128, GLA_HEADS * GLA_DK), F32)
        wg = wg.at[0, :GLA_RANK].set(gla_wg2[i, 0]).at[1, GLA_RANK:2 * GLA_RANK].set(gla_wg2[i, 1])
        gla_params = [wg.astype(BF16), gla_bg[i].reshape(2, 1, GLA_HEADS * GLA_DK), row(gla_norm_g[i])]
        gla_bwd = [("u", 512, OFF_GLA_QK, False), ("u", 512, OFF_GLA_V, False), ("u", 128, OFF_GLA_G1, False)]
        gla_fwd = gla_bwd + [("u", 512, OFF_GLA_R, False)]
        gla_zero = [jnp.zeros((bsz, GROUP_W, GLA_HEADS * GLA_DK), F32)]
        yd_t, yd_l = _bidir_mixer(_gla_kernel, gla_bwd, gla_fwd, 0, GLA_CHUNK, ut_col, ul_col, gla_zero,
                                  gla_params, "gla", F32)

        fg = row(final_g)

        def tail(xs, ys, m, rows_per_group, final, colmajor):
            cm = lambda y: y.reshape(bsz, GRID_W, rows, GROUP_W) if colmajor else y.reshape(-1, GROUP_W)
            ys = [ys[0].reshape(-1, GROUP_W), cm(ys[1]), ys[2].reshape(-1, GROUP_W), cm(ys[3])]
            xs, h2 = _outproj(ys, w_out_b, i, xs, m[2], row(norm2_g[i]), m[3], m[4], rows_per_group, colmajor)
            act = _ffn_up(h2, w_gate, w_up, i)
            return _ffn_down(act, w_down_b, i, xs, m[5], fg, final, rows_per_group)

        xl = tail(xl, [ya_l, yb_l, yc_l, yd_l], m_l, length, i == depth - 1, True)
        if need_ctx:
            xt = tail(xt, [ya_t, yb_t, yc_t, yd_t], m_t, bsz * lctx, False, False)
    return xl.reshape(bsz, length, dm)
```

```python
import functools

import jax
import jax.numpy as jnp
from jax import lax
from jax.experimental import pallas as pl
from jax.experimental.pallas import tpu as pltpu

F32 = jnp.float32
BF16 = jnp.bfloat16
EPS = 1e-6
NEG = -1e30

D_MODEL = 2048
GRID_W = 64
GROUP_W = 512
N_MOD = 6
CONV_W = 5
HALO = 8

SSD_HEADS = 8
SSD_CHUNK = 128
ML_HEADS = 4
ML_CHUNK = 64
ML_HEAD_GROUP = 4
LRU_C = 8.0
LRU_CHUNK = 256
LRU_GROUP = 64
GLA_HEADS = 4
GLA_DK = 64
GLA_RANK = 16
GLA_TAU = 16.0
GLA_CHUNK = 64
GLA_SUB = 16

OFF_SSD_XBC = 0
OFF_LRU_X = 1024
OFF_LRU_GATE = 1536
OFF_SSD_Z = 2048
OFF_SSD_DT = 2560
N_ROW = 2688
OFF_ML_Q = 0
OFF_ML_K = 512
OFF_ML_V = 1024
OFF_ML_O = 1536
OFF_GLA_QK = 2048
OFF_GLA_V = 2560
OFF_GLA_R = 3072
OFF_ML_G = 3584
OFF_GLA_G1 = 3712
N_COL = 3840

VMEM_LIMIT = 56 * 1024 * 1024
TM_PROJ = 512
COLS_PER_TILE = 8
ROWS_PER_TILE = 8
TM_OUT = 512
TM_FFN_UP = 2048
TN_FFN_UP = 512
TM_FFN_DOWN = 512


def _cparams(sem):
    return pltpu.CompilerParams(dimension_semantics=sem, vmem_limit_bytes=VMEM_LIMIT)


def _resident(shape, layer=None):
    if layer is None:
        nd = len(shape)
        return pl.BlockSpec(shape, lambda *_: (0,) * nd, pipeline_mode=pl.Buffered(1))
    nd = len(shape)
    return pl.BlockSpec((None,) + tuple(shape), lambda *_: (layer,) + (0,) * nd, pipeline_mode=pl.Buffered(1))


def _sigmoid(x):
    return 0.5 + 0.5 * jnp.tanh(0.5 * x)


def _silu(x):
    return x * _sigmoid(x)


def _softplus(x):
    return jnp.maximum(x, 0.0) + jnp.log(1.0 + jnp.exp(-jnp.abs(x)))


def _log_sigmoid(x):
    return jnp.minimum(x, 0.0) - jnp.log(1.0 + jnp.exp(-jnp.abs(x)))


def _gelu_tanh(x):
    return 0.5 * x * (1.0 + jnp.tanh(0.7978845608028654 * (x + 0.044715 * (x * x * x))))


def _rms(x):
    return x * lax.rsqrt(jnp.mean(x * x, axis=-1, keepdims=True) + EPS)


def _dot(a, b):
    return jnp.dot(a, b, preferred_element_type=F32)


def _dot_nt(a, b):
    return lax.dot_general(a, b, (((1,), (1,)), ((), ())), preferred_element_type=F32)


def _dot_exact(a, b):
    return jnp.dot(a, b, precision=lax.Precision.HIGHEST, preferred_element_type=F32)


def _scan_mask(t, rev):
    ri = lax.broadcasted_iota(jnp.int32, (t, t), 0)
    ci = lax.broadcasted_iota(jnp.int32, (t, t), 1)
    return (ci >= ri) if rev else (ci <= ri)


def _conv5(prev, main, nxt, w, bias, first, last):
    t = main.shape[0]
    prev = jnp.where(first, 0.0, prev)
    nxt = jnp.where(last, 0.0, nxt)
    ext = jnp.concatenate([prev, main, nxt], axis=0)
    n = t + 2 * HALO
    acc = None
    for k in range(CONV_W):
        sh = (CONV_W // 2 - k) % n
        r = ext if sh == 0 else pltpu.roll(ext, sh, axis=0)
        term = r[HALO:HALO + t] * w[k:k + 1, :]
        acc = term if acc is None else acc + term
    return acc + bias


def _mod_kernel(c_ref, w_ref, b_ref, o_ref):
    s = _silu(c_ref[...]).astype(BF16)
    o_ref[...] = _dot(s, w_ref[...].astype(BF16)) + b_ref[...]


def _norm_mod(x, g_ref, sh_ref, sc_ref):
    return ((_rms(x) * g_ref[...]) * (1.0 + sc_ref[0]) + sh_ref[0]).astype(BF16)


def _inproj_kernel(x_ref, g_ref, sh_ref, sc_ref, w_ref, o_ref):
    o_ref[...] = _dot(_norm_mod(x_ref[...], g_ref, sh_ref, sc_ref), w_ref[...])


def _inproj_colmajor_kernel(x_ref, g_ref, sh_ref, sc_ref, w_ref, o_ref, xs_ref):
    rows = x_ref.shape[1]
    for wl in range(COLS_PER_TILE):
        xs_ref[wl * rows:(wl + 1) * rows, :] = x_ref[0, :, wl, :]
    o_ref[...] = _dot(_norm_mod(xs_ref[...], g_ref, sh_ref, sc_ref), w_ref[...])


def _outproj_kernel(colmajor, ya_ref, yb_ref, yc_ref, yd_ref, w_ref, x_ref, gate_ref, g_ref, sh_ref, sc_ref,
                    xo_ref, h_ref):
    if colmajor:
        def rowmajor(ref):
            return jnp.concatenate([ref[0, :, rl, :] for rl in range(ROWS_PER_TILE)], axis=0).astype(BF16)
        yb, yd = rowmajor(yb_ref), rowmajor(yd_ref)
    else:
        yb, yd = yb_ref[...], yd_ref[...]
    acc = _dot(ya_ref[...], w_ref[0])
    acc = acc + _dot(yc_ref[...], w_ref[2])
    acc = acc + _dot(yb, w_ref[1])
    acc = acc + _dot(yd, w_ref[3])
    xn = x_ref[...] + gate_ref[0] * acc
    xo_ref[...] = xn
    y = _rms(xn) * g_ref[...]
    h_ref[...] = (y * (1.0 + sc_ref[0]) + sh_ref[0]).astype(BF16)


def _ffn_up_kernel(h_ref, wg_ref, wu_ref, o_ref):
    h = h_ref[...]
    g = _dot(h, wg_ref[...].astype(BF16))
    u = _dot(h, wu_ref[...].astype(BF16))
    o_ref[...] = (_silu(g) * u).astype(BF16)


def _ffn_down_kernel(final, a_ref, w_ref, x_ref, gate_ref, fg_ref, o_ref):
    xn = x_ref[...] + gate_ref[0] * _dot(a_ref[...], w_ref[...])
    if final:
        xn = _rms(xn) * fg_ref[...]
    o_ref[...] = xn


def _ssd_kernel(rev, finish, nchunks, nb, *refs):
    if finish:
        (xc_ref, dt_ref, z_ref, yp_ref, s0_ref, cw_ref, cb_ref, dtb_ref, alog_ref,
         dsk_ref, ng_ref, y_ref, s_ref) = refs
    else:
        (xbc_ref, xp_ref, xn_ref, dt_ref, s0_ref, cw_ref, cb_ref, dtb_ref, alog_ref,
         dsk_ref, ng_ref, y_ref, xc_ref, s_ref) = refs
    t = SSD_CHUNK
    d = 1 if rev else 0
    j = pl.program_id(0)
    c = (nchunks - 1 - j) if rev else j

    @pl.when(j == 0)
    def _():
        s_ref[...] = s0_ref[...]

    mask = _scan_mask(t, rev)
    maskf = mask.astype(F32)
    er = 0 if rev else t - 1
    lo = lax.broadcasted_iota(jnp.int32, (t, 128), 1) < 64
    lo_state = lax.broadcasted_iota(jnp.int32, s_ref.shape[2:], 1) < 64
    cw, cbias = cw_ref[...], cb_ref[...]
    nega = -jnp.exp(alog_ref[...])

    bs = range(nb)
    if finish:
        xbc = [xc_ref[bi] for bi in bs]
    else:
        xbc = [_silu(_conv5(xp_ref[bi], xbc_ref[bi], xn_ref[bi], cw, cbias, c == 0, c == nchunks - 1))
               for bi in bs]
        for bi in bs:
            xc_ref[bi] = xbc[bi]
    dt_all = [_softplus(dt_ref[bi] + dtb_ref[...]) for bi in bs]
    b_all = [_dot_exact(maskf, dt_all[bi] * nega) for bi in bs]
    nl = 2 * SSD_HEADS
    lane = lax.broadcasted_iota(jnp.int32, (t, 128), 1)
    bd_t = [jnp.where(lane < nl, b_all[bi], pltpu.roll(dt_all[bi], nl, axis=1)).T for bi in bs]
    ys = [[None] * (SSD_HEADS // 2) for _ in bs]
    bm_t = [[xbc[bi][:, 512 + 128 * g:640 + 128 * g].T.astype(BF16) for bi in bs] for g in range(2)]
    for p in range(SSD_HEADS // 2):
        g = p // 2
        bm = [xbc[bi][:, 512 + 128 * g:640 + 128 * g].astype(BF16) for bi in bs]
        cm = [xbc[bi][:, 768 + 128 * g:896 + 128 * g].astype(BF16) for bi in bs]
        cb = [_dot_nt(cm[bi], bm[bi]) for bi in bs]
        xpair = [xbc[bi][:, 128 * p:128 * p + 128] for bi in bs]
        xpair_b = [xpair[bi].astype(BF16) for bi in bs]
        s_pair = [s_ref[bi, p] for bi in bs]
        y_inter = [_dot(cm[bi], s_pair[bi].astype(BF16)) for bi in bs]
        yh, bh, dh, dec = [], [], [], []
        for hh in range(2):
            l = d * SSD_HEADS + 2 * p + hh
            bh.append([jnp.broadcast_to(b_all[bi][:, l:l + 1], (t, 128)) for bi in bs])
            dh.append([jnp.broadcast_to(dt_all[bi][:, l:l + 1], (t, 128)) for bi in bs])
            seg = [jnp.exp(jnp.where(mask, bh[hh][bi] - bd_t[bi][l:l + 1, :], NEG)) for bi in bs]
            w = [(cb[bi] * seg[bi] * bd_t[bi][nl + l:nl + l + 1, :]).astype(BF16) for bi in bs]
            yh.append([_dot(w[bi], xpair_b[bi]) for bi in bs])
            dec.append([jnp.exp(b_all[bi][er:er + 1, l:l + 1]) for bi in bs])
        b_pair = [jnp.where(lo, bh[0][bi], bh[1][bi]) for bi in bs]
        dt_pair = [jnp.where(lo, dh[0][bi], dh[1][bi]) for bi in bs]
        coef = [jnp.exp(b_pair[bi][er:er + 1, :] - b_pair[bi]) * dt_pair[bi] for bi in bs]
        xw = [(xpair[bi] * coef[bi]).astype(BF16) for bi in bs]
        upd = [_dot(bm_t[g][bi], xw[bi]) for bi in bs]
        for bi in bs:
            ys[bi][p] = jnp.where(lo, yh[0][bi], yh[1][bi]) + y_inter[bi] * jnp.exp(b_pair[bi])
            s_ref[bi, p] = jnp.where(lo_state, dec[0][bi], dec[1][bi]) * s_pair[bi] + upd[bi]
    for bi in bs:
        y = jnp.concatenate(ys[bi], axis=1)
        if finish:
            y = y + yp_ref[bi] + dsk_ref[...] * xbc[bi][:, :GROUP_W]
            y = _rms(y * _silu(z_ref[bi])) * ng_ref[...]
            y_ref[bi] = y.astype(y_ref.dtype)
        else:
            y_ref[bi] = y


def _mlstm_kernel(rev, finish, nchunks, nb, *refs):
    if finish:
        (qkc_ref, v_ref, g_ref, o_ref, yp_ref, cs0_ref, sm0_ref,
         cw_ref, cb_ref, gb_ref, ng_ref, y_ref, cs_ref, sm_ref) = refs
    else:
        (q_ref, qp_ref, qn_ref, k_ref, kp_ref, kn_ref, v_ref, g_ref, cs0_ref, sm0_ref,
         cw_ref, cb_ref, gb_ref, ng_ref, y_ref, qkc_ref, cs_ref, sm_ref) = refs
    t = ML_CHUNK
    d = 1 if rev else 0
    j = pl.program_id(0)
    c = (nchunks - 1 - j) if rev else j
    first, last = c == 0, c == nchunks - 1

    @pl.when(j == 0)
    def _():
        cs_ref[...] = cs0_ref[...]
        sm_ref[...] = sm0_ref[...]

    cw = cw_ref[...]
    cbias = cb_ref[...]
    maskf = _scan_mask(t, rev).astype(F32)
    mask_t = _scan_mask(t, not rev)
    er = 0 if rev else t - 1

    bs = range(nb)
    if finish:
        q = [qkc_ref[bi][:, :GROUP_W] for bi in bs]
        k = [qkc_ref[bi][:, GROUP_W:] for bi in bs]
    else:
        q = [_silu(_conv5(qp_ref[bi], q_ref[bi], qn_ref[bi], cw[:, :GROUP_W], cbias[:, :GROUP_W], first, last))
             for bi in bs]
        k = [_silu(_conv5(kp_ref[bi], k_ref[bi], kn_ref[bi], cw[:, GROUP_W:], cbias[:, GROUP_W:], first, last))
             * (128.0 ** -0.5) for bi in bs]
        for bi in bs:
            qkc_ref[bi, :, :GROUP_W] = q[bi]
            qkc_ref[bi, :, GROUP_W:] = k[bi]
    v = [v_ref[bi] for bi in bs]
    gts = [g_ref[bi] + gb_ref[...] for bi in bs]
    b_all = [_dot_exact(maskf, _log_sigmoid(gts[bi])) for bi in bs]
    lane = lax.broadcasted_iota(jnp.int32, (t, 128), 1)
    gb_t = [jnp.where(lane < 2 * ML_HEADS, gts[bi], b_all[bi]).T for bi in bs]
    ys = [[None] * ML_HEADS for _ in bs]
    for h0 in range(0, ML_HEADS, ML_HEAD_GROUP):
        ch = [(bi, h) for h in range(h0, h0 + ML_HEAD_GROUP) for bi in bs]
        cs_ = range(len(ch))
        li = [d * ML_HEADS + h for _, h in ch]
        lf = [2 * ML_HEADS + l for l in li]
        sl = [slice(128 * h, 128 * h + 128) for _, h in ch]
        brow = [gb_t[bi][lf[c]:lf[c] + 1, :] for c, (bi, h) in enumerate(ch)]
        igrow = [gb_t[bi][li[c]:li[c] + 1, :] for c, (bi, h) in enumerate(ch)]
        ccol = [gts[bi][:, li[c]:li[c] + 1] - b_all[bi][:, lf[c]:lf[c] + 1]
                for c, (bi, h) in enumerate(ch)]
        m_prev = [sm_ref[bi, ML_HEADS + h:ML_HEADS + h + 1, 0:1] for bi, h in ch]
        ns = [sm_ref[bi, h:h + 1, :] for bi, h in ch]
        cs = [cs_ref[bi, h] for bi, h in ch]
        dmat = [jnp.where(mask_t, brow[c] + ccol[c], NEG) for c in cs_]
        inter = [brow[c] + m_prev[c] for c in cs_]
        mt = [jnp.maximum(inter[c], jnp.max(dmat[c], axis=0, keepdims=True)) for c in cs_]
        w = [jnp.exp(dmat[c] - mt[c]) for c in cs_]
        sc = [jnp.exp(inter[c] - mt[c]) for c in cs_]
        qb = [q[bi][:, sl[c]].astype(BF16) for c, (bi, h) in enumerate(ch)]
        kb = [k[bi][:, sl[c]].astype(BF16) for c, (bi, h) in enumerate(ch)]
        sw = [_dot_nt(kb[c], qb[c]) * w[c] for c in cs_]
        v_t = [v[bi][:, sl[c]].T for c, (bi, h) in enumerate(ch)]
        num = [_dot(v_t[c].astype(BF16), sw[c].astype(BF16)) for c in cs_]
        qc = [_dot_nt(cs[c].astype(BF16), qb[c]) for c in cs_]
        qn = [_dot_nt(jnp.broadcast_to(ns[c], (8, 128)).astype(BF16), qb[c])[0:1] for c in cs_]
        den = [jnp.sum(sw[c], axis=0, keepdims=True) + sc[c] * qn[c] for c in cs_]
        for c, (bi, h) in enumerate(ch):
            inv = 1.0 / jnp.maximum(jnp.abs(den[c]), jnp.exp(-mt[c]))
            ys[bi][h] = ((num[c] + sc[c] * qc[c]) * inv).T
        bl = [b_all[bi][er:er + 1, lf[c]:lf[c] + 1] for c, (bi, h) in enumerate(ch)]
        tail = [bl[c] - brow[c] + igrow[c] for c in cs_]
        m_new = [jnp.maximum(bl[c] + m_prev[c], jnp.max(tail[c], axis=1, keepdims=True)) for c in cs_]
        ws = [jnp.exp(tail[c] - m_new[c]) for c in cs_]
        sc_end = [jnp.exp(bl[c] + m_prev[c] - m_new[c]) for c in cs_]
        upd = [_dot((v_t[c] * ws[c]).astype(BF16), kb[c]) for c in cs_]
        nup = [_dot(jnp.broadcast_to(ws[c], (8, t)).astype(BF16), kb[c])[0:1] for c in cs_]
        for c, (bi, h) in enumerate(ch):
            cs_ref[bi, h] = sc_end[c] * cs[c] + upd[c]
            sm_ref[bi, h:h + 1, :] = sc_end[c] * ns[c] + nup[c]
            sm_ref[bi, ML_HEADS + h:ML_HEADS + h + 1, :] = jnp.broadcast_to(m_new[c], (1, 128))
    for bi in bs:
        if finish:
            yp = yp_ref[bi]
            outs = [_rms(ys[bi][h] + yp[:, 128 * h:128 * h + 128]) for h in range(ML_HEADS)]
            y = jnp.concatenate(outs, axis=1) * ng_ref[...]
            y_ref[bi] = (_sigmoid(o_ref[bi]) * y).astype(y_ref.dtype)
        else:
            y_ref[bi] = jnp.concatenate(ys[bi], axis=1)


def _lru_kernel(rev, finish, nchunks, nb, *refs):
    if finish:
        (xf_ref, gate_ref, yp_ref, h0_ref, cw_ref, cb_ref, w_ref, bias_ref, lam_ref,
         y_ref, h_ref, sa_ref, sh_ref) = refs
    else:
        (x_ref, xp_ref, xn_ref, h0_ref, cw_ref, cb_ref, w_ref, bias_ref, lam_ref,
         y_ref, xf_ref, h_ref, sa_ref, sh_ref) = refs
    t = LRU_CHUNK
    d = 1 if rev else 0
    j = pl.program_id(0)
    c = (nchunks - 1 - j) if rev else j

    @pl.when(j == 0)
    def _():
        h_ref[...] = h0_ref[...]

    cw, cbias = cw_ref[...], cb_ref[...]
    lsl = LRU_C * _log_sigmoid(lam_ref[d])
    sub = lax.broadcasted_iota(jnp.int32, (8, 128), 0)
    ng = GROUP_W // 128

    for bi in range(nb):
        if finish:
            xf = xf_ref[bi]
        else:
            xf = _conv5(xp_ref[bi], x_ref[bi], xn_ref[bi], cw, cbias, c == 0, c == nchunks - 1)
            xf_ref[bi] = xf
        pre = _dot(xf.astype(BF16), w_ref[d]) + bias_ref[d]
        r = _sigmoid(pre[:, :GROUP_W])
        ig = _sigmoid(pre[:, GROUP_W:])
        loga = r * lsl
        a = jnp.exp(loga)
        bx = jnp.sqrt(-jnp.tanh(loga) * (1.0 + a * a)) * ig * xf
        for g in range(ng):
            sa_ref[bi, g] = a[:, 128 * g:128 * g + 128]
            sh_ref[bi, g] = bx[:, 128 * g:128 * g + 128]

    chains = [(bi, g) for bi in range(nb) for g in range(ng)]
    carry = [h_ref[bi, :, 128 * g:128 * g + 128] for bi, g in chains]
    for grp in (range(t // LRU_GROUP - 1, -1, -1) if rev else range(t // LRU_GROUP)):
        base = LRU_GROUP * grp
        a_run, h_run = [None] * len(chains), [None] * len(chains)
        a_cum = [[None] * 8 for _ in chains]
        h_loc = [[None] * 8 for _ in chains]
        for v in (range(7, -1, -1) if rev else range(8)):
            for ci, (bi, g) in enumerate(chains):
                a_v = sa_ref[bi, g, pl.ds(base + v, 8, stride=8), :]
                b_v = sh_ref[bi, g, pl.ds(base + v, 8, stride=8), :]
                if a_run[ci] is None:
                    a_run[ci], h_run[ci] = a_v, b_v
                else:
                    h_run[ci] = a_v * h_run[ci] + b_v
                    a_run[ci] = a_v * a_run[ci]
                a_cum[ci][v], h_loc[ci][v] = a_run[ci], h_run[ci]
        for ci, (bi, g) in enumerate(chains):
            pa, ph = a_run[ci], h_run[ci]
            for sh in (1, 2, 4):
                if rev:
                    valid = sub < 8 - sh
                    a_s = pltpu.roll(pa, 8 - sh, axis=0)
                    h_s = pltpu.roll(ph, 8 - sh, axis=0)
                else:
                    valid = sub >= sh
                    a_s = pltpu.roll(pa, sh, axis=0)
                    h_s = pltpu.roll(ph, sh, axis=0)
                ph = jnp.where(valid, pa * h_s + ph, ph)
                pa = jnp.where(valid, pa * a_s, pa)
            after = pa * carry[ci] + ph
            if rev:
                cin = jnp.where(sub == 7, carry[ci], pltpu.roll(after, 7, axis=0))
                carry[ci] = after[0:1]
            else:
                cin = jnp.where(sub == 0, carry[ci], pltpu.roll(after, 1, axis=0))
                carry[ci] = after[7:8]
            for v in range(8):
                sh_ref[bi, g, pl.ds(base + v, 8, stride=8), :] = h_loc[ci][v] + a_cum[ci][v] * cin
    for ci, (bi, g) in enumerate(chains):
        h_ref[bi, :, 128 * g:128 * g + 128] = carry[ci]
    for bi in range(nb):
        hs = jnp.concatenate([sh_ref[bi, g] for g in range(ng)], axis=1)
        if finish:
            y_ref[bi] = ((hs + yp_ref[bi]) * _gelu_tanh(gate_ref[bi])).astype(y_ref.dtype)
        else:
            y_ref[bi] = hs


def _gla_kernel(rev, finish, nchunks, nb, *refs):
    if finish:
        (qk_ref, v_ref, g1_ref, r_ref, yp_ref, s0_ref, wg_ref, bg_ref, ng_ref, y_ref, s_ref) = refs
    else:
        (qk_ref, v_ref, g1_ref, s0_ref, wg_ref, bg_ref, ng_ref, y_ref, s_ref) = refs
    t = GLA_CHUNK
    d = 1 if rev else 0
    dkk = GLA_HEADS * GLA_DK
    j = pl.program_id(0)

    @pl.when(j == 0)
    def _():
        s_ref[...] = s0_ref[...]

    mask = _scan_mask(t, rev)
    maskf = mask.astype(F32)
    rows = lax.broadcasted_iota(jnp.int32, (t, 1), 0)
    lane_head = lax.broadcasted_iota(jnp.int32, (GLA_SUB, dkk), 1) // GLA_DK
    blk = (lax.broadcasted_iota(jnp.int32, (GROUP_W, dkk), 0) // 128
           == lax.broadcasted_iota(jnp.int32, (GROUP_W, dkk), 1) // GLA_DK)
    er = 0 if rev else t - 1

    bs = range(nb)
    q = [qk_ref[bi][:, :dkk] * (GLA_DK ** -0.5) for bi in bs]
    k = [qk_ref[bi][:, dkk:] for bi in bs]
    v = [v_ref[bi] for bi in bs]
    vb = [v[bi].astype(BF16) for bi in bs]
    glog = [_dot(g1_ref[bi].astype(BF16), wg_ref[d]) + bg_ref[d] for bi in bs]
    la = [_log_sigmoid(glog[bi]) * (1.0 / GLA_TAU) for bi in bs]
    b = [_dot_exact(maskf, la[bi]) for bi in bs]
    excl = [b[bi] - la[bi] for bi in bs]
    s_t = [s_ref[bi] for bi in bs]
    o_inter = [_dot_nt((q[bi] * jnp.exp(b[bi])).astype(BF16), s_t[bi].astype(BF16)) for bi in bs]

    att_blocks = [[None] * (t // GLA_SUB) for _ in bs]
    for i in range(t // GLA_SUB):
        r0 = GLA_SUB * i
        if rev:
            ref_row = [excl[bi][r0 + GLA_SUB - 1:r0 + GLA_SUB] for bi in bs]
            kvalid = rows >= r0
        else:
            ref_row = [excl[bi][r0:r0 + 1] for bi in bs]
            kvalid = rows < r0 + GLA_SUB
        qi = [q[bi][r0:r0 + GLA_SUB] * jnp.exp(b[bi][r0:r0 + GLA_SUB] - ref_row[bi]) for bi in bs]
        ki = [(k[bi] * jnp.exp(jnp.where(kvalid, ref_row[bi] - b[bi], NEG))).astype(BF16) for bi in bs]
        qs = [jnp.concatenate([jnp.where(lane_head == h, qi[bi], 0.0) for h in range(GLA_HEADS)],
                              axis=0).astype(BF16) for bi in bs]
        for bi in bs:
            att_blocks[bi][i] = _dot_nt(qs[bi], ki[bi])
    outs = [[None] * GLA_HEADS for _ in bs]
    for h in range(GLA_HEADS):
        att = [jnp.where(mask, jnp.concatenate([ab[GLA_SUB * h:GLA_SUB * h + GLA_SUB] for ab in att_blocks[bi]],
                                               axis=0), 0.0).astype(BF16) for bi in bs]
        for bi in bs:
            outs[bi][h] = _dot(att[bi], vb[bi][:, 128 * h:128 * h + 128])

    bl = [b[bi][er:er + 1] for bi in bs]
    kd = [(k[bi] * jnp.exp(bl[bi] - b[bi])).astype(BF16) for bi in bs]
    v_t = [jnp.concatenate([v[bi][:, 128 * h:128 * h + 128].T for h in range(GLA_HEADS)], axis=0).astype(BF16)
           for bi in bs]
    upd = [_dot(v_t[bi], kd[bi]) for bi in bs]
    for bi in bs:
        s_ref[bi] = s_t[bi] * jnp.exp(bl[bi]) + jnp.where(blk, upd[bi], 0.0)
    for bi in bs:
        o = jnp.concatenate(outs[bi], axis=1) + o_inter[bi]
        if finish:
            o = o + yp_ref[bi]
            o = jnp.concatenate([_rms(o[:, 128 * h:128 * h + 128]) for h in range(GLA_HEADS)], axis=1)
            y_ref[bi] = (o * ng_ref[...] * _silu(r_ref[bi])).astype(y_ref.dtype)
        else:
            y_ref[bi] = o


def _chunk_specs(nb, width, off, t, length, rev, nchunks, halo):
    ob = off // width
    tb = t // HALO

    def cj(j):
        return (nchunks - 1 - j) if rev else j

    main = pl.BlockSpec((nb, t, width), lambda j: (0, cj(j), ob))
    if not halo:
        return [main]
    prev = pl.BlockSpec((nb, HALO, width), lambda j: (0, jnp.maximum(cj(j) * tb - 1, 0), ob))
    nxt = pl.BlockSpec((nb, HALO, width), lambda j: (0, jnp.minimum((cj(j) + 1) * tb, length // HALO - 1), ob))
    return [main, prev, nxt]


def _run_mixer(body, pieces, t, rev, finish, ypart, states, params, name, save_width=0, out_dtype=BF16,
               scratch=()):
    nb, length = pieces[0][0].shape[0], pieces[0][0].shape[1]
    nchunks = length // t
    cj = (lambda j: nchunks - 1 - j) if rev else (lambda j: j)
    row_spec = lambda width: pl.BlockSpec((nb, t, width), lambda j: (0, cj(j), 0))
    in_specs, args = [], []
    for arr, width, off, halo in pieces:
        sp = _chunk_specs(nb, width, off, t, length, rev, nchunks, halo)
        in_specs += sp
        args += [arr] * len(sp)
    if finish:
        in_specs.append(row_spec(GROUP_W))
        args.append(ypart)
    for a in list(states) + list(params):
        in_specs.append(pl.BlockSpec(a.shape, lambda j, _nd=a.ndim: (0,) * _nd))
        args.append(a)
    out_shape = [jax.ShapeDtypeStruct((nb, length, GROUP_W), out_dtype if finish else F32)]
    out_specs = [row_spec(GROUP_W)]
    if save_width:
        out_shape.append(jax.ShapeDtypeStruct((nb, length, save_width), F32))
        out_specs.append(row_spec(save_width))
    for s in states:
        out_shape.append(jax.ShapeDtypeStruct(s.shape, s.dtype))
        out_specs.append(pl.BlockSpec(s.shape, lambda j, _nd=s.ndim: (0,) * _nd))
    res = pl.pallas_call(
        functools.partial(body, rev, finish, nchunks, nb),
        grid=(nchunks,),
        in_specs=in_specs,
        out_specs=out_specs,
        out_shape=out_shape,
        scratch_shapes=list(scratch),
        compiler_params=_cparams(("arbitrary",)),
        name=name,
    )(*args)
    nfix = 2 if save_width else 1
    return res[0], (res[1] if save_width else None), list(res[nfix:])


def _bidir_mixer(body, bwd_pieces, fwd_pieces, save_width, t, u_ctx, u_lat, zero_states, params, name,
                 lat_dtype=BF16, scratch=()):
    def run(u, rev, ypart, saved, states, tag, out_dtype=BF16):
        src = {"u": u, "saved": saved}
        pieces = [(src[s], w, o, h) for s, w, o, h in (bwd_pieces if rev else fwd_pieces)]
        return _run_mixer(body, pieces, t, rev, not rev, ypart, states, params, name + tag,
                          save_width if rev else 0, out_dtype, scratch)

    yb_c, sv_c, st_b = run(u_ctx, True, None, None, zero_states, "_ctx_bwd")
    y_c, _, st_f = run(u_ctx, False, yb_c, sv_c, zero_states, "_ctx_fwd")
    yb_l, sv_l, _ = run(u_lat, True, None, None, st_b, "_lat_bwd")
    y_l, _, _ = run(u_lat, False, yb_l, sv_l, st_f, "_lat_fwd", lat_dtype)
    return y_c, y_l


def _mods(cvec, w_mod, b_mod):
    depth, dm, nm = w_mod.shape
    tn = 1024
    return pl.pallas_call(
        _mod_kernel,
        grid=(depth, nm // tn),
        in_specs=[pl.BlockSpec((8, dm), lambda l, j: (0, 0)),
                  pl.BlockSpec((None, dm, tn), lambda l, j: (l, 0, j)),
                  pl.BlockSpec((None, 1, tn), lambda l, j: (l, 0, j))],
        out_specs=pl.BlockSpec((None, 8, tn), lambda l, j: (l, 0, j)),
        out_shape=jax.ShapeDtypeStruct((depth, 8, nm), F32),
        compiler_params=_cparams(("parallel", "parallel")),
        name="mods",
    )(cvec, w_mod, b_mod.reshape(depth, 1, nm))


def _inproj(x, g, shift, scale, w, layer, rows_per_group):
    m, dm = x.shape
    n = w.shape[2]
    tm = TM_PROJ
    grp = lambda i: (i * tm // rows_per_group, 0, 0)
    return pl.pallas_call(
        _inproj_kernel,
        grid=(m // tm,),
        in_specs=[pl.BlockSpec((tm, dm), lambda i: (i, 0)),
                  _resident((1, dm)),
                  pl.BlockSpec((1, 1, dm), grp),
                  pl.BlockSpec((1, 1, dm), grp),
                  _resident((dm, n), layer)],
        out_specs=pl.BlockSpec((tm, n), lambda i: (i, 0)),
        out_shape=jax.ShapeDtypeStruct((m, n), F32),
        compiler_params=_cparams(("parallel",)),
        name="inproj",
    )(x, g, shift, scale, w)


def _outproj(ys, w, layer, x, gate, g, shift, scale, rows_per_group, colmajor):
    m, dm = x.shape
    tm = min(TM_OUT, rows_per_group)
    grp = lambda i: (i * tm // rows_per_group, 0, 0)
    yspec = pl.BlockSpec((tm, GROUP_W), lambda i: (i, 0))
    if colmajor:
        assert tm == ROWS_PER_TILE * GRID_W
        tiles = rows_per_group // tm
        cspec = pl.BlockSpec((1, GRID_W, ROWS_PER_TILE, GROUP_W), lambda i: (i // tiles, 0, i % tiles, 0))
    else:
        cspec = yspec
    return pl.pallas_call(
        functools.partial(_outproj_kernel, colmajor),
        grid=(m // tm,),
        in_specs=[yspec, cspec, yspec, cspec,
                  _resident((4, GROUP_W, dm), layer),
                  pl.BlockSpec((tm, dm), lambda i: (i, 0)),
                  pl.BlockSpec((1, 1, dm), grp),
                  _resident((1, dm)),
                  pl.BlockSpec((1, 1, dm), grp),
                  pl.BlockSpec((1, 1, dm), grp)],
        out_specs=[pl.BlockSpec((tm, dm), lambda i: (i, 0)), pl.BlockSpec((tm, dm), lambda i: (i, 0))],
        out_shape=[jax.ShapeDtypeStruct((m, dm), F32), jax.ShapeDtypeStruct((m, dm), BF16)],
        compiler_params=_cparams(("parallel",)),
        name="outproj",
    )(*ys, w, x, gate, g, shift, scale)


def _inproj_colmajor(x, g, shift, scale, w, layer, bsz, rows):
    dm = x.shape[1]
    n = w.shape[2]
    tiles = GRID_W // COLS_PER_TILE
    tm = COLS_PER_TILE * rows
    grp = lambda i: (i // tiles, 0, 0)
    return pl.pallas_call(
        _inproj_colmajor_kernel,
        grid=(bsz * tiles,),
        in_specs=[pl.BlockSpec((1, rows, COLS_PER_TILE, dm), lambda i: (i // tiles, 0, i % tiles, 0)),
                  _resident((1, dm)),
                  pl.BlockSpec((1, 1, dm), grp),
                  pl.BlockSpec((1, 1, dm), grp),
                  _resident((dm, n), layer)],
        out_specs=pl.BlockSpec((tm, n), lambda i: (i, 0)),
        out_shape=jax.ShapeDtypeStruct((bsz * rows * GRID_W, n), F32),
        scratch_shapes=[pltpu.VMEM((tm, dm), F32)],
        compiler_params=_cparams(("parallel",)),
        name="inproj_colmajor",
    )(x.reshape(bsz, rows, GRID_W, dm), g, shift, scale, w)


def _ffn_up(h, wg, wu, layer):
    m, dm = h.shape
    dff = wg.shape[2]
    tm = min(TM_FFN_UP, m)
    tn = TN_FFN_UP
    return pl.pallas_call(
        _ffn_up_kernel,
        grid=(m // tm, dff // tn),
        in_specs=[pl.BlockSpec((tm, dm), lambda i, j: (i, 0)),
                  pl.BlockSpec((None, dm, tn), lambda i, j: (layer, 0, j)),
                  pl.BlockSpec((None, dm, tn), lambda i, j: (layer, 0, j))],
        out_specs=pl.BlockSpec((tm, tn), lambda i, j: (i, j)),
        out_shape=jax.ShapeDtypeStruct((m, dff), BF16),
        compiler_params=_cparams(("parallel", "arbitrary")),
        name="ffn_up",
    )(h, wg, wu)


def _ffn_down(a, w, layer, x, gate, final_g, final, rows_per_group):
    m, dm = x.shape
    dff = a.shape[1]
    tm = TM_FFN_DOWN
    grp = lambda i: (i * tm // rows_per_group, 0, 0)
    return pl.pallas_call(
        functools.partial(_ffn_down_kernel, final),
        grid=(m // tm,),
        in_specs=[pl.BlockSpec((tm, dff), lambda i: (i, 0)),
                  _resident((dff, dm), layer),
                  pl.BlockSpec((tm, dm), lambda i: (i, 0)),
                  pl.BlockSpec((1, 1, dm), grp),
                  _resident((1, dm))],
        out_specs=pl.BlockSpec((tm, dm), lambda i: (i, 0)),
        out_shape=jax.ShapeDtypeStruct((m, dm), F32),
        compiler_params=_cparams(("parallel",)),
        name="ffn_down",
    )(a, w, x, gate, final_g)


def _pad_cols(a, width):
    return jnp.pad(a, ((0, 0), (0, width - a.shape[1])))


def _split_w_in(w):
    ssd, ml, lru, gla = jnp.split(w, [1552, 1552 + 2064, 1552 + 2064 + 1024], axis=2)
    ssd_z, ssd_xbc, ssd_dt = ssd[..., :512], ssd[..., 512:1536], ssd[..., 1536:]
    ml_qkvo, ml_g = ml[..., :2048], ml[..., 2048:]
    lru_gate, lru_x = lru[..., :512], lru[..., 512:]
    gla_qkvr, gla_g1 = gla[..., :1536], gla[..., 1536:]
    pad = lambda a: jnp.pad(a, ((0, 0), (0, 0), (0, 128 - a.shape[2])))
    w_row = jnp.concatenate([ssd_xbc, lru_x, lru_gate, ssd_z, pad(ssd_dt)], axis=2)
    w_col = jnp.concatenate([ml_qkvo, gla_qkvr, pad(ml_g), pad(gla_g1)], axis=2)
    return w_row.astype(BF16), w_col.astype(BF16)


def _row128(a):
    return _pad_cols(a.reshape(1, -1).astype(F32), 128)


def _block_diag(w):
    nb, bi, bj = w.shape
    eye = jnp.eye(nb, dtype=w.dtype)
    return (eye[:, None, :, None] * w[:, :, None, :]).reshape(nb * bi, nb * bj)


def kernel(x, c, ctx, c_ctx, norm1_g, norm2_g, w_mod, b_mod, w_in, w_out, ssd_conv_w, ssd_conv_b, ssd_dt_bias, ssd_a_log, ssd_d, ssd_norm_g, ml_conv_w, ml_conv_b, ml_igate_b, ml_fgate_b, ml_norm_g, lru_conv_w, lru_conv_b, lru_wa, lru_ba, lru_wx, lru_bx, lru_lambda, gla_wg2, gla_bg, gla_norm_g, w_gate, w_up, w_down, final_g):
    bsz, length, dm = x.shape
    lctx = ctx.shape[1]
    depth = w_in.shape[0]
    rows = length // GRID_W
    assert dm == D_MODEL and length % SSD_CHUNK == 0 and lctx % SSD_CHUNK == 0
    assert length % LRU_CHUNK == 0 and lctx % LRU_CHUNK == 0 and lctx % ML_CHUNK == 0
    assert rows == ML_CHUNK == GLA_CHUNK

    cvec = jnp.concatenate([c, c_ctx[None, :], jnp.zeros((8 - bsz - 1, dm), F32)], axis=0)
    mods = _mods(cvec, w_mod, b_mod)

    xl = x.reshape(bsz * length, dm)
    xt = ctx.reshape(bsz * lctx, dm)
    row = lambda a: a.reshape(1, -1).astype(F32)
    w_row, w_col = _split_w_in(w_in)
    w_out_b = w_out.astype(BF16).reshape(depth, 4, GROUP_W, dm)
    w_down_b = w_down.astype(BF16)

    for i in range(depth):
        need_ctx = i < depth - 1
        m_l = [mods[i, :bsz, k * dm:(k + 1) * dm].reshape(bsz, 1, dm) for k in range(N_MOD)]
        m_t = [mods[i, bsz:bsz + 1, k * dm:(k + 1) * dm].reshape(1, 1, dm) for k in range(N_MOD)]
        g1 = row(norm1_g[i])
        ul_row = _inproj(xl, g1, m_l[0], m_l[1], w_row, i, length).reshape(bsz, length, N_ROW)
        ul_col = _inproj_colmajor(xl, g1, m_l[0], m_l[1], w_col, i, bsz, rows).reshape(bsz, length, N_COL)
        ut_row = _inproj(xt, g1, m_t[0], m_t[1], w_row, i, bsz * lctx).reshape(bsz, lctx, N_ROW)
        ut_col = _inproj(xt, g1, m_t[0], m_t[1], w_col, i, bsz * lctx).reshape(bsz, lctx, N_COL)

        ssd_params = [ssd_conv_w[i], row(ssd_conv_b[i]), _row128(ssd_dt_bias[i]), _row128(ssd_a_log[i]),
                      row(jnp.repeat(ssd_d[i], GROUP_W // SSD_HEADS)), row(ssd_norm_g[i])]
        ssd_bwd = [("u", 1024, OFF_SSD_XBC, True), ("u", 128, OFF_SSD_DT, False)]
        ssd_fwd = [("saved", 1024, 0, False), ("u", 128, OFF_SSD_DT, False), ("u", 512, OFF_SSD_Z, False)]
        ssd_zero = [jnp.zeros((bsz, SSD_HEADS // 2, 128, 128), F32)]
        ya_t, ya_l = _bidir_mixer(_ssd_kernel, ssd_bwd, ssd_fwd, 1024, SSD_CHUNK, ut_row, ul_row, ssd_zero,
                                  ssd_params, "ssd")

        ml_gb = _row128(jnp.concatenate([ml_igate_b[i].reshape(-1), ml_fgate_b[i].reshape(-1)]))
        ml_params = [ml_conv_w[i], row(ml_conv_b[i]), ml_gb, row(ml_norm_g[i])]
        ml_bwd = [("u", 512, OFF_ML_Q, True), ("u", 512, OFF_ML_K, True), ("u", 512, OFF_ML_V, False),
                  ("u", 128, OFF_ML_G, False)]
        ml_fwd = [("saved", 1024, 0, False), ("u", 512, OFF_ML_V, False), ("u", 128, OFF_ML_G, False),
                  ("u", 512, OFF_ML_O, False)]
        ml_zero = [jnp.zeros((bsz, ML_HEADS, 128, 128), F32), jnp.zeros((bsz, 2 * ML_HEADS, 128), F32)]
        yb_t, yb_l = _bidir_mixer(_mlstm_kernel, ml_bwd, ml_fwd, 1024, ML_CHUNK, ut_col, ul_col, ml_zero,
                                  ml_params, "mlstm", F32)

        lru_w = jnp.stack([jnp.concatenate([_block_diag(lru_wa[i, dd]), _block_diag(lru_wx[i, dd])], axis=1)
                           for dd in range(2)]).astype(BF16)
        lru_bias = jnp.concatenate([lru_ba[i], lru_bx[i]], axis=1).reshape(2, 1, 2 * GROUP_W)
        lru_params = [lru_conv_w[i], row(lru_conv_b[i]), lru_w, lru_bias, lru_lambda[i].reshape(2, 1, GROUP_W)]
        lru_bwd = [("u", 512, OFF_LRU_X, True)]
        lru_fwd = [("saved", 512, 0, False), ("u", 512, OFF_LRU_GATE, False)]
        lru_zero = [jnp.zeros((bsz, 1, GROUP_W), F32)]
        lru_scratch = [pltpu.VMEM((bsz, GROUP_W // 128, LRU_CHUNK, 128), F32)] * 2
        yc_t, yc_l = _bidir_mixer(_lru_kernel, lru_bwd, lru_fwd, 512, LRU_CHUNK, ut_row, ul_row, lru_zero,
                                  lru_params, "lru", scratch=lru_scratch)

        wg = jnp.zeros((2, 128, GLA_HEADS * GLA_DK), F32)
        wg = wg.at[0, :GLA_RANK].set(gla_wg2[i, 0]).at[1, GLA_RANK:2 * GLA_RANK].set(gla_wg2[i, 1])
        gla_params = [wg.astype(BF16), gla_bg[i].reshape(2, 1, GLA_HEADS * GLA_DK), row(gla_norm_g[i])]
        gla_bwd = [("u", 512, OFF_GLA_QK, False), ("u", 512, OFF_GLA_V, False), ("u", 128, OFF_GLA_G1, False)]
        gla_fwd = gla_bwd + [("u", 512, OFF_GLA_R, False)]
        gla_zero = [jnp.zeros((bsz, GROUP_W, GLA_HEADS * GLA_DK), F32)]
        yd_t, yd_l = _bidir_mixer(_gla_kernel, gla_bwd, gla_fwd, 0, GLA_CHUNK, ut_col, ul_col, gla_zero,
                                  gla_params, "gla", F32)

        fg = row(final_g)

        def tail(xs, ys, m, rows_per_group, final, colmajor):
            cm = lambda y: y.reshape(bsz, GRID_W, rows, GROUP_W) if colmajor else y.reshape(-1, GROUP_W)
            ys = [ys[0].reshape(-1, GROUP_W), cm(ys[1]), ys[2].reshape(-1, GROUP_W), cm(ys[3])]
            xs, h2 = _outproj(ys, w_out_b, i, xs, m[2], row(norm2_g[i]), m[3], m[4], rows_per_group, colmajor)
            act = _ffn_up(h2, w_gate, w_up, i)
            return _ffn_down(act, w_down_b, i, xs, m[5], fg, final, rows_per_group)

        xl = tail(xl, [ya_l, yb_l, yc_l, yd_l], m_l, length, i == depth - 1, True)
        if need_ctx:
            xt = tail(xt, [ya_t, yb_t, yc_t, yd_t], m_t, bsz * lctx, False, False)
    return xl.reshape(bsz, length, dm)
```

```python
import functools

import jax
import jax.numpy as jnp
from jax import lax
from jax.experimental import pallas as pl
from jax.experimental.pallas import tpu as pltpu

F32 = jnp.float32
BF16 = jnp.bfloat16
EPS = 1e-6
NEG = -1e30

D_MODEL = 2048
GRID_W = 64
GROUP_W = 512
N_MOD = 6
CONV_W = 5
HALO = 8

SSD_HEADS = 8
SSD_CHUNK = 128
ML_HEADS = 4
ML_CHUNK = 64
ML_HEAD_GROUP = 4
LRU_C = 8.0
LRU_CHUNK = 256
LRU_GROUP = 64
GLA_HEADS = 4
GLA_DK = 64
GLA_RANK = 16
GLA_TAU = 16.0
GLA_CHUNK = 64
GLA_SUB = 16

OFF_SSD_XBC = 0
OFF_LRU_X = 1024
OFF_LRU_GATE = 1536
OFF_SSD_Z = 2048
OFF_SSD_DT = 2560
N_ROW = 2688
OFF_ML_Q = 0
OFF_ML_K = 512
OFF_ML_V = 1024
OFF_ML_O = 1536
OFF_GLA_QK = 2048
OFF_GLA_V = 2560
OFF_GLA_R = 3072
OFF_ML_G = 3584
OFF_GLA_G1 = 3712
N_COL = 3840

VMEM_LIMIT = 56 * 1024 * 1024
TM_PROJ = 512
COLS_PER_TILE = 8
ROWS_PER_TILE = 8
TM_OUT = 512
TM_FFN_UP = 2048
TN_FFN_UP = 512
TM_FFN_DOWN = 512


def _cparams(sem):
    return pltpu.CompilerParams(dimension_semantics=sem, vmem_limit_bytes=VMEM_LIMIT)


def _resident(shape, layer=None):
    if layer is None:
        nd = len(shape)
        return pl.BlockSpec(shape, lambda *_: (0,) * nd, pipeline_mode=pl.Buffered(1))
    nd = len(shape)
    return pl.BlockSpec((None,) + tuple(shape), lambda *_: (layer,) + (0,) * nd, pipeline_mode=pl.Buffered(1))


def _sigmoid(x):
    return 0.5 + 0.5 * jnp.tanh(0.5 * x)


def _silu(x):
    return x * _sigmoid(x)


def _softplus(x):
    return jnp.maximum(x, 0.0) + jnp.log(1.0 + jnp.exp(-jnp.abs(x)))


def _log_sigmoid(x):
    return jnp.minimum(x, 0.0) - jnp.log(1.0 + jnp.exp(-jnp.abs(x)))


def _gelu_tanh(x):
    return 0.5 * x * (1.0 + jnp.tanh(0.7978845608028654 * (x + 0.044715 * (x * x * x))))


def _rms(x):
    return x * lax.rsqrt(jnp.mean(x * x, axis=-1, keepdims=True) + EPS)


def _dot(a, b):
    return jnp.dot(a, b, preferred_element_type=F32)


def _dot_nt(a, b):
    return lax.dot_general(a, b, (((1,), (1,)), ((), ())), preferred_element_type=F32)


def _dot_exact(a, b):
    return jnp.dot(a, b, precision=lax.Precision.HIGHEST, preferred_element_type=F32)


def _scan_mask(t, rev):
    ri = lax.broadcasted_iota(jnp.int32, (t, t), 0)
    ci = lax.broadcasted_iota(jnp.int32, (t, t), 1)
    return (ci >= ri) if rev else (ci <= ri)


def _conv5(prev, main, nxt, w, bias, first, last):
    t = main.shape[0]
    prev = jnp.where(first, 0.0, prev)
    nxt = jnp.where(last, 0.0, nxt)
    ext = jnp.concatenate([prev, main, nxt], axis=0)
    n = t + 2 * HALO
    acc = None
    for k in range(CONV_W):
        sh = (CONV_W // 2 - k) % n
        r = ext if sh == 0 else pltpu.roll(ext, sh, axis=0)
        term = r[HALO:HALO + t] * w[k:k + 1, :]
        acc = term if acc is None else acc + term
    return acc + bias


def _mod_kernel(c_ref, w_ref, b_ref, o_ref):
    s = _silu(c_ref[...]).astype(BF16)
    o_ref[...] = _dot(s, w_ref[...].astype(BF16)) + b_ref[...]


def _norm_mod(x, g_ref, sh_ref, sc_ref):
    return ((_rms(x) * g_ref[...]) * (1.0 + sc_ref[0]) + sh_ref[0]).astype(BF16)


def _inproj_kernel(x_ref, g_ref, sh_ref, sc_ref, w_ref, o_ref):
    o_ref[...] = _dot(_norm_mod(x_ref[...], g_ref, sh_ref, sc_ref), w_ref[...])


def _inproj_colmajor_kernel(x_ref, g_ref, sh_ref, sc_ref, w_ref, o_ref, xs_ref):
    rows = x_ref.shape[1]
    for wl in range(COLS_PER_TILE):
        xs_ref[wl * rows:(wl + 1) * rows, :] = x_ref[0, :, wl, :]
    o_ref[...] = _dot(_norm_mod(xs_ref[...], g_ref, sh_ref, sc_ref), w_ref[...])


def _outproj_kernel(colmajor, ya_ref, yb_ref, yc_ref, yd_ref, w_ref, x_ref, gate_ref, g_ref, sh_ref, sc_ref,
                    xo_ref, h_ref):
    if colmajor:
        def rowmajor(ref):
            return jnp.concatenate([ref[0, :, rl, :] for rl in range(ROWS_PER_TILE)], axis=0).astype(BF16)
        yb, yd = rowmajor(yb_ref), rowmajor(yd_ref)
    else:
        yb, yd = yb_ref[...], yd_ref[...]
    acc = _dot(jnp.concatenate([ya_ref[...], yb, yc_ref[...], yd], axis=1), w_ref[...])
    xn = x_ref[...] + gate_ref[0] * acc
    xo_ref[...] = xn
    y = _rms(xn) * g_ref[...]
    h_ref[...] = (y * (1.0 + sc_ref[0]) + sh_ref[0]).astype(BF16)


def _ffn_up_kernel(h_ref, wg_ref, wu_ref, o_ref):
    h = h_ref[...]
    g = _dot(h, wg_ref[...].astype(BF16))
    u = _dot(h, wu_ref[...].astype(BF16))
    o_ref[...] = (_silu(g) * u).astype(BF16)


def _ffn_down_kernel(final, a_ref, w_ref, x_ref, gate_ref, fg_ref, o_ref):
    xn = x_ref[...] + gate_ref[0] * _dot(a_ref[...], w_ref[...])
    if final:
        xn = _rms(xn) * fg_ref[...]
    o_ref[...] = xn


def _ssd_kernel(rev, finish, nchunks, nb, *refs):
    if finish:
        (xc_ref, dt_ref, z_ref, yp_ref, s0_ref, cw_ref, cb_ref, dtb_ref, alog_ref,
         dsk_ref, ng_ref, y_ref, s_ref) = refs
    else:
        (xbc_ref, xp_ref, xn_ref, dt_ref, s0_ref, cw_ref, cb_ref, dtb_ref, alog_ref,
         dsk_ref, ng_ref, y_ref, xc_ref, s_ref) = refs
    t = SSD_CHUNK
    d = 1 if rev else 0
    j = pl.program_id(0)
    c = (nchunks - 1 - j) if rev else j

    @pl.when(j == 0)
    def _():
        s_ref[...] = s0_ref[...]

    mask = _scan_mask(t, rev)
    maskf = mask.astype(F32)
    er = 0 if rev else t - 1
    lo = lax.broadcasted_iota(jnp.int32, (t, 128), 1) < 64
    lo_state = lax.broadcasted_iota(jnp.int32, s_ref.shape[2:], 1) < 64
    cw, cbias = cw_ref[...], cb_ref[...]
    nega = -jnp.exp(alog_ref[...])

    bs = range(nb)
    if finish:
        xbc = [xc_ref[bi] for bi in bs]
    else:
        xbc = [_silu(_conv5(xp_ref[bi], xbc_ref[bi], xn_ref[bi], cw, cbias, c == 0, c == nchunks - 1))
               for bi in bs]
        for bi in bs:
            xc_ref[bi] = xbc[bi]
    dt_all = [_softplus(dt_ref[bi] + dtb_ref[...]) for bi in bs]
    b_all = [_dot_exact(maskf, dt_all[bi] * nega) for bi in bs]
    nl = 2 * SSD_HEADS
    lane = lax.broadcasted_iota(jnp.int32, (t, 128), 1)
    bd_t = [jnp.where(lane < nl, b_all[bi], pltpu.roll(dt_all[bi], nl, axis=1)).T for bi in bs]
    ys = [[None] * (SSD_HEADS // 2) for _ in bs]
    bm_t = [[xbc[bi][:, 512 + 128 * g:640 + 128 * g].T.astype(BF16) for bi in bs] for g in range(2)]
    bm_g = [[xbc[bi][:, 512 + 128 * g:640 + 128 * g].astype(BF16) for bi in bs] for g in range(2)]
    cm_g = [[xbc[bi][:, 768 + 128 * g:896 + 128 * g].astype(BF16) for bi in bs] for g in range(2)]
    cb_g = [[_dot_nt(cm_g[g][bi], bm_g[g][bi]) for bi in bs] for g in range(2)]
    for p in range(SSD_HEADS // 2):
        g = p // 2
        cm, cb = cm_g[g], cb_g[g]
        xpair = [xbc[bi][:, 128 * p:128 * p + 128] for bi in bs]
        xpair_b = [xpair[bi].astype(BF16) for bi in bs]
        s_pair = [s_ref[bi, p] for bi in bs]
        y_inter = [_dot(cm[bi], s_pair[bi].astype(BF16)) for bi in bs]
        yh, bh, dh, dec = [], [], [], []
        for hh in range(2):
            l = d * SSD_HEADS + 2 * p + hh
            bh.append([jnp.broadcast_to(b_all[bi][:, l:l + 1], (t, 128)) for bi in bs])
            dh.append([jnp.broadcast_to(dt_all[bi][:, l:l + 1], (t, 128)) for bi in bs])
            seg = [jnp.exp(jnp.where(mask, bh[hh][bi] - bd_t[bi][l:l + 1, :], NEG)) for bi in bs]
            w = [(cb[bi] * seg[bi] * bd_t[bi][nl + l:nl + l + 1, :]).astype(BF16) for bi in bs]
            yh.append([_dot(w[bi], xpair_b[bi]) for bi in bs])
            dec.append([jnp.exp(b_all[bi][er:er + 1, l:l + 1]) for bi in bs])
        b_pair = [jnp.where(lo, bh[0][bi], bh[1][bi]) for bi in bs]
        dt_pair = [jnp.where(lo, dh[0][bi], dh[1][bi]) for bi in bs]
        coef = [jnp.exp(b_pair[bi][er:er + 1, :] - b_pair[bi]) * dt_pair[bi] for bi in bs]
        xw = [(xpair[bi] * coef[bi]).astype(BF16) for bi in bs]
        upd = [_dot(bm_t[g][bi], xw[bi]) for bi in bs]
        for bi in bs:
            ys[bi][p] = jnp.where(lo, yh[0][bi], yh[1][bi]) + y_inter[bi] * jnp.exp(b_pair[bi])
            s_ref[bi, p] = jnp.where(lo_state, dec[0][bi], dec[1][bi]) * s_pair[bi] + upd[bi]
    for bi in bs:
        y = jnp.concatenate(ys[bi], axis=1)
        if finish:
            y = y + yp_ref[bi] + dsk_ref[...] * xbc[bi][:, :GROUP_W]
            y = _rms(y * _silu(z_ref[bi])) * ng_ref[...]
            y_ref[bi] = y.astype(y_ref.dtype)
        else:
            y_ref[bi] = y


def _mlstm_kernel(rev, finish, nchunks, nb, *refs):
    if finish:
        (qkc_ref, v_ref, g_ref, o_ref, yp_ref, cs0_ref, sm0_ref,
         cw_ref, cb_ref, gb_ref, ng_ref, y_ref, cs_ref, sm_ref) = refs
    else:
        (q_ref, qp_ref, qn_ref, k_ref, kp_ref, kn_ref, v_ref, g_ref, cs0_ref, sm0_ref,
         cw_ref, cb_ref, gb_ref, ng_ref, y_ref, qkc_ref, cs_ref, sm_ref) = refs
    t = ML_CHUNK
    d = 1 if rev else 0
    j = pl.program_id(0)
    c = (nchunks - 1 - j) if rev else j
    first, last = c == 0, c == nchunks - 1

    @pl.when(j == 0)
    def _():
        cs_ref[...] = cs0_ref[...]
        sm_ref[...] = sm0_ref[...]

    cw = cw_ref[...]
    cbias = cb_ref[...]
    maskf = _scan_mask(t, rev).astype(F32)
    mask_t = _scan_mask(t, not rev)
    er = 0 if rev else t - 1

    bs = range(nb)
    if finish:
        q = [qkc_ref[bi][:, :GROUP_W] for bi in bs]
        k = [qkc_ref[bi][:, GROUP_W:] for bi in bs]
    else:
        q = [_silu(_conv5(qp_ref[bi], q_ref[bi], qn_ref[bi], cw[:, :GROUP_W], cbias[:, :GROUP_W], first, last))
             for bi in bs]
        k = [_silu(_conv5(kp_ref[bi], k_ref[bi], kn_ref[bi], cw[:, GROUP_W:], cbias[:, GROUP_W:], first, last))
             * (128.0 ** -0.5) for bi in bs]
        for bi in bs:
            qkc_ref[bi, :, :GROUP_W] = q[bi]
            qkc_ref[bi, :, GROUP_W:] = k[bi]
    v = [v_ref[bi] for bi in bs]
    gts = [g_ref[bi] + gb_ref[...] for bi in bs]
    b_all = [_dot_exact(maskf, _log_sigmoid(gts[bi])) for bi in bs]
    lane = lax.broadcasted_iota(jnp.int32, (t, 128), 1)
    gb_t = [jnp.where(lane < 2 * ML_HEADS, gts[bi], b_all[bi]).T for bi in bs]
    ys = [[None] * ML_HEADS for _ in bs]
    for h0 in range(0, ML_HEADS, ML_HEAD_GROUP):
        ch = [(bi, h) for h in range(h0, h0 + ML_HEAD_GROUP) for bi in bs]
        cs_ = range(len(ch))
        li = [d * ML_HEADS + h for _, h in ch]
        lf = [2 * ML_HEADS + l for l in li]
        sl = [slice(128 * h, 128 * h + 128) for _, h in ch]
        brow = [gb_t[bi][lf[c]:lf[c] + 1, :] for c, (bi, h) in enumerate(ch)]
        igrow = [gb_t[bi][li[c]:li[c] + 1, :] for c, (bi, h) in enumerate(ch)]
        ccol = [gts[bi][:, li[c]:li[c] + 1] - b_all[bi][:, lf[c]:lf[c] + 1]
                for c, (bi, h) in enumerate(ch)]
        m_prev = [sm_ref[bi, ML_HEADS + h:ML_HEADS + h + 1, 0:1] for bi, h in ch]
        ns = [sm_ref[bi, h:h + 1, :] for bi, h in ch]
        cs = [cs_ref[bi, h] for bi, h in ch]
        dmat = [jnp.where(mask_t, brow[c] + ccol[c], NEG) for c in cs_]
        inter = [brow[c] + m_prev[c] for c in cs_]
        mt = [jnp.maximum(inter[c], jnp.max(dmat[c], axis=0, keepdims=True)) for c in cs_]
        w = [jnp.exp(dmat[c] - mt[c]) for c in cs_]
        sc = [jnp.exp(inter[c] - mt[c]) for c in cs_]
        qb = [q[bi][:, sl[c]].astype(BF16) for c, (bi, h) in enumerate(ch)]
        kb = [k[bi][:, sl[c]].astype(BF16) for c, (bi, h) in enumerate(ch)]
        sw = [_dot_nt(kb[c], qb[c]) * w[c] for c in cs_]
        v_t = [v[bi][:, sl[c]].T for c, (bi, h) in enumerate(ch)]
        num = [_dot(v_t[c].astype(BF16), sw[c].astype(BF16)) for c in cs_]
        qc = [_dot_nt(cs[c].astype(BF16), qb[c]) for c in cs_]
        qn = [_dot_nt(jnp.broadcast_to(ns[c], (8, 128)).astype(BF16), qb[c])[0:1] for c in cs_]
        den = [jnp.sum(sw[c], axis=0, keepdims=True) + sc[c] * qn[c] for c in cs_]
        for c, (bi, h) in enumerate(ch):
            inv = 1.0 / jnp.maximum(jnp.abs(den[c]), jnp.exp(-mt[c]))
            ys[bi][h] = ((num[c] + sc[c] * qc[c]) * inv).T
        bl = [b_all[bi][er:er + 1, lf[c]:lf[c] + 1] for c, (bi, h) in enumerate(ch)]
        tail = [bl[c] - brow[c] + igrow[c] for c in cs_]
        m_new = [jnp.maximum(bl[c] + m_prev[c], jnp.max(tail[c], axis=1, keepdims=True)) for c in cs_]
        ws = [jnp.exp(tail[c] - m_new[c]) for c in cs_]
        sc_end = [jnp.exp(bl[c] + m_prev[c] - m_new[c]) for c in cs_]
        upd = [_dot((v_t[c] * ws[c]).astype(BF16), kb[c]) for c in cs_]
        nup = [_dot(jnp.broadcast_to(ws[c], (8, t)).astype(BF16), kb[c])[0:1] for c in cs_]
        for c, (bi, h) in enumerate(ch):
            cs_ref[bi, h] = sc_end[c] * cs[c] + upd[c]
            sm_ref[bi, h:h + 1, :] = sc_end[c] * ns[c] + nup[c]
            sm_ref[bi, ML_HEADS + h:ML_HEADS + h + 1, :] = jnp.broadcast_to(m_new[c], (1, 128))
    for bi in bs:
        if finish:
            yp = yp_ref[bi]
            outs = [_rms(ys[bi][h] + yp[:, 128 * h:128 * h + 128]) for h in range(ML_HEADS)]
            y = jnp.concatenate(outs, axis=1) * ng_ref[...]
            y_ref[bi] = (_sigmoid(o_ref[bi]) * y).astype(y_ref.dtype)
        else:
            y_ref[bi] = jnp.concatenate(ys[bi], axis=1)


def _lru_kernel(rev, finish, nchunks, nb, *refs):
    if finish:
        (xf_ref, gate_ref, yp_ref, h0_ref, cw_ref, cb_ref, w_ref, bias_ref, lam_ref,
         y_ref, h_ref, sa_ref, sh_ref) = refs
    else:
        (x_ref, xp_ref, xn_ref, h0_ref, cw_ref, cb_ref, w_ref, bias_ref, lam_ref,
         y_ref, xf_ref, h_ref, sa_ref, sh_ref) = refs
    t = LRU_CHUNK
    d = 1 if rev else 0
    j = pl.program_id(0)
    c = (nchunks - 1 - j) if rev else j

    @pl.when(j == 0)
    def _():
        h_ref[...] = h0_ref[...]

    cw, cbias = cw_ref[...], cb_ref[...]
    lsl = LRU_C * _log_sigmoid(lam_ref[d])
    sub = lax.broadcasted_iota(jnp.int32, (8, 128), 0)
    ng = GROUP_W // 128

    for bi in range(nb):
        if finish:
            xf = xf_ref[bi]
        else:
            xf = _conv5(xp_ref[bi], x_ref[bi], xn_ref[bi], cw, cbias, c == 0, c == nchunks - 1)
            xf_ref[bi] = xf
        pre = _dot(xf.astype(BF16), w_ref[d]) + bias_ref[d]
        r = _sigmoid(pre[:, :GROUP_W])
        ig = _sigmoid(pre[:, GROUP_W:])
        loga = r * lsl
        a = jnp.exp(loga)
        bx = jnp.sqrt(-jnp.tanh(loga) * (1.0 + a * a)) * ig * xf
        for g in range(ng):
            sa_ref[bi, g] = a[:, 128 * g:128 * g + 128]
            sh_ref[bi, g] = bx[:, 128 * g:128 * g + 128]

    chains = [(bi, g) for bi in range(nb) for g in range(ng)]
    carry = [h_ref[bi, :, 128 * g:128 * g + 128] for bi, g in chains]
    for grp in (range(t // LRU_GROUP - 1, -1, -1) if rev else range(t // LRU_GROUP)):
        base = LRU_GROUP * grp
        a_run, h_run = [None] * len(chains), [None] * len(chains)
        a_cum = [[None] * 8 for _ in chains]
        h_loc = [[None] * 8 for _ in chains]
        for v in (range(7, -1, -1) if rev else range(8)):
            for ci, (bi, g) in enumerate(chains):
                a_v = sa_ref[bi, g, pl.ds(base + v, 8, stride=8), :]
                b_v = sh_ref[bi, g, pl.ds(base + v, 8, stride=8), :]
                if a_run[ci] is None:
                    a_run[ci], h_run[ci] = a_v, b_v
                else:
                    h_run[ci] = a_v * h_run[ci] + b_v
                    a_run[ci] = a_v * a_run[ci]
                a_cum[ci][v], h_loc[ci][v] = a_run[ci], h_run[ci]
        for ci, (bi, g) in enumerate(chains):
            pa, ph = a_run[ci], h_run[ci]
            for sh in (1, 2, 4):
                if rev:
                    valid = sub < 8 - sh
                    a_s = pltpu.roll(pa, 8 - sh, axis=0)
                    h_s = pltpu.roll(ph, 8 - sh, axis=0)
                else:
                    valid = sub >= sh
                    a_s = pltpu.roll(pa, sh, axis=0)
                    h_s = pltpu.roll(ph, sh, axis=0)
                ph = jnp.where(valid, pa * h_s + ph, ph)
                pa = jnp.where(valid, pa * a_s, pa)
            after = pa * carry[ci] + ph
            if rev:
                cin = jnp.where(sub == 7, carry[ci], pltpu.roll(after, 7, axis=0))
                carry[ci] = after[0:1]
            else:
                cin = jnp.where(sub == 0, carry[ci], pltpu.roll(after, 1, axis=0))
                carry[ci] = after[7:8]
            for v in range(8):
                sh_ref[bi, g, pl.ds(base + v, 8, stride=8), :] = h_loc[ci][v] + a_cum[ci][v] * cin
    for ci, (bi, g) in enumerate(chains):
        h_ref[bi, :, 128 * g:128 * g + 128] = carry[ci]
    for bi in range(nb):
        hs = jnp.concatenate([sh_ref[bi, g] for g in range(ng)], axis=1)
        if finish:
            y_ref[bi] = ((hs + yp_ref[bi]) * _gelu_tanh(gate_ref[bi])).astype(y_ref.dtype)
        else:
            y_ref[bi] = hs


def _gla_kernel(rev, finish, nchunks, nb, *refs):
    if finish:
        (qk_ref, v_ref, g1_ref, r_ref, yp_ref, s0_ref, wg_ref, bg_ref, ng_ref, y_ref, s_ref) = refs
    else:
        (qk_ref, v_ref, g1_ref, s0_ref, wg_ref, bg_ref, ng_ref, y_ref, s_ref) = refs
    t = GLA_CHUNK
    d = 1 if rev else 0
    dkk = GLA_HEADS * GLA_DK
    j = pl.program_id(0)

    @pl.when(j == 0)
    def _():
        s_ref[...] = s0_ref[...]

    mask = _scan_mask(t, rev)
    maskf = mask.astype(F32)
    rows = lax.broadcasted_iota(jnp.int32, (t, 1), 0)
    lane_head = lax.broadcasted_iota(jnp.int32, (GLA_SUB, dkk), 1) // GLA_DK
    blk = (lax.broadcasted_iota(jnp.int32, (GROUP_W, dkk), 0) // 128
           == lax.broadcasted_iota(jnp.int32, (GROUP_W, dkk), 1) // GLA_DK)
    er = 0 if rev else t - 1

    bs = range(nb)
    q = [qk_ref[bi][:, :dkk] * (GLA_DK ** -0.5) for bi in bs]
    k = [qk_ref[bi][:, dkk:] for bi in bs]
    v = [v_ref[bi] for bi in bs]
    vb = [v[bi].astype(BF16) for bi in bs]
    glog = [_dot(g1_ref[bi].astype(BF16), wg_ref[d]) + bg_ref[d] for bi in bs]
    la = [_log_sigmoid(glog[bi]) * (1.0 / GLA_TAU) for bi in bs]
    b = [_dot_exact(maskf, la[bi]) for bi in bs]
    excl = [b[bi] - la[bi] for bi in bs]
    s_t = [s_ref[bi] for bi in bs]
    o_inter = [_dot_nt((q[bi] * jnp.exp(b[bi])).astype(BF16), s_t[bi].astype(BF16)) for bi in bs]

    att_blocks = [[None] * (t // GLA_SUB) for _ in bs]
    for i in range(t // GLA_SUB):
        r0 = GLA_SUB * i
        if rev:
            ref_row = [excl[bi][r0 + GLA_SUB - 1:r0 + GLA_SUB] for bi in bs]
            kvalid = rows >= r0
        else:
            ref_row = [excl[bi][r0:r0 + 1] for bi in bs]
            kvalid = rows < r0 + GLA_SUB
        qi = [q[bi][r0:r0 + GLA_SUB] * jnp.exp(b[bi][r0:r0 + GLA_SUB] - ref_row[bi]) for bi in bs]
        ki = [(k[bi] * jnp.exp(jnp.where(kvalid, ref_row[bi] - b[bi], NEG))).astype(BF16) for bi in bs]
        qs = [jnp.concatenate([jnp.where(lane_head == h, qi[bi], 0.0) for h in range(GLA_HEADS)],
                              axis=0).astype(BF16) for bi in bs]
        for bi in bs:
            att_blocks[bi][i] = _dot_nt(qs[bi], ki[bi])
    outs = [[None] * GLA_HEADS for _ in bs]
    for h in range(GLA_HEADS):
        att = [jnp.where(mask, jnp.concatenate([ab[GLA_SUB * h:GLA_SUB * h + GLA_SUB] for ab in att_blocks[bi]],
                                               axis=0), 0.0).astype(BF16) for bi in bs]
        for bi in bs:
            outs[bi][h] = _dot(att[bi], vb[bi][:, 128 * h:128 * h + 128])

    bl = [b[bi][er:er + 1] for bi in bs]
    kd = [(k[bi] * jnp.exp(bl[bi] - b[bi])).astype(BF16) for bi in bs]
    v_t = [jnp.concatenate([v[bi][:, 128 * h:128 * h + 128].T for h in range(GLA_HEADS)], axis=0).astype(BF16)
           for bi in bs]
    upd = [_dot(v_t[bi], kd[bi]) for bi in bs]
    for bi in bs:
        s_ref[bi] = s_t[bi] * jnp.exp(bl[bi]) + jnp.where(blk, upd[bi], 0.0)
    for bi in bs:
        o = jnp.concatenate(outs[bi], axis=1) + o_inter[bi]
        if finish:
            o = o + yp_ref[bi]
            o = jnp.concatenate([_rms(o[:, 128 * h:128 * h + 128]) for h in range(GLA_HEADS)], axis=1)
            y_ref[bi] = (o * ng_ref[...] * _silu(r_ref[bi])).astype(y_ref.dtype)
        else:
            y_ref[bi] = o


def _chunk_specs(nb, width, off, t, length, rev, nchunks, halo):
    ob = off // width
    tb = t // HALO

    def cj(j):
        return (nchunks - 1 - j) if rev else j

    main = pl.BlockSpec((nb, t, width), lambda j: (0, cj(j), ob))
    if not halo:
        return [main]
    prev = pl.BlockSpec((nb, HALO, width), lambda j: (0, jnp.maximum(cj(j) * tb - 1, 0), ob))
    nxt = pl.BlockSpec((nb, HALO, width), lambda j: (0, jnp.minimum((cj(j) + 1) * tb, length // HALO - 1), ob))
    return [main, prev, nxt]


def _run_mixer(body, pieces, t, rev, finish, ypart, states, params, name, save_width=0, out_dtype=BF16,
               scratch=()):
    nb, length = pieces[0][0].shape[0], pieces[0][0].shape[1]
    nchunks = length // t
    cj = (lambda j: nchunks - 1 - j) if rev else (lambda j: j)
    row_spec = lambda width: pl.BlockSpec((nb, t, width), lambda j: (0, cj(j), 0))
    in_specs, args = [], []
    for arr, width, off, halo in pieces:
        sp = _chunk_specs(nb, width, off, t, length, rev, nchunks, halo)
        in_specs += sp
        args += [arr] * len(sp)
    if finish:
        in_specs.append(row_spec(GROUP_W))
        args.append(ypart)
    for a in list(states) + list(params):
        in_specs.append(pl.BlockSpec(a.shape, lambda j, _nd=a.ndim: (0,) * _nd))
        args.append(a)
    out_shape = [jax.ShapeDtypeStruct((nb, length, GROUP_W), out_dtype if finish else F32)]
    out_specs = [row_spec(GROUP_W)]
    if save_width:
        out_shape.append(jax.ShapeDtypeStruct((nb, length, save_width), F32))
        out_specs.append(row_spec(save_width))
    for s in states:
        out_shape.append(jax.ShapeDtypeStruct(s.shape, s.dtype))
        out_specs.append(pl.BlockSpec(s.shape, lambda j, _nd=s.ndim: (0,) * _nd))
    res = pl.pallas_call(
        functools.partial(body, rev, finish, nchunks, nb),
        grid=(nchunks,),
        in_specs=in_specs,
        out_specs=out_specs,
        out_shape=out_shape,
        scratch_shapes=list(scratch),
        compiler_params=_cparams(("arbitrary",)),
        name=name,
    )(*args)
    nfix = 2 if save_width else 1
    return res[0], (res[1] if save_width else None), list(res[nfix:])


def _bidir_mixer(body, bwd_pieces, fwd_pieces, save_width, t, u_ctx, u_lat, zero_states, params, name,
                 lat_dtype=BF16, scratch=()):
    def run(u, rev, ypart, saved, states, tag, out_dtype=BF16):
        src = {"u": u, "saved": saved}
        pieces = [(src[s], w, o, h) for s, w, o, h in (bwd_pieces if rev else fwd_pieces)]
        return _run_mixer(body, pieces, t, rev, not rev, ypart, states, params, name + tag,
                          save_width if rev else 0, out_dtype, scratch)

    yb_c, sv_c, st_b = run(u_ctx, True, None, None, zero_states, "_ctx_bwd")
    y_c, _, st_f = run(u_ctx, False, yb_c, sv_c, zero_states, "_ctx_fwd")
    yb_l, sv_l, _ = run(u_lat, True, None, None, st_b, "_lat_bwd")
    y_l, _, _ = run(u_lat, False, yb_l, sv_l, st_f, "_lat_fwd", lat_dtype)
    return y_c, y_l


def _mods(cvec, w_mod, b_mod):
    depth, dm, nm = w_mod.shape
    tn = 1024
    return pl.pallas_call(
        _mod_kernel,
        grid=(depth, nm // tn),
        in_specs=[pl.BlockSpec((8, dm), lambda l, j: (0, 0)),
                  pl.BlockSpec((None, dm, tn), lambda l, j: (l, 0, j)),
                  pl.BlockSpec((None, 1, tn), lambda l, j: (l, 0, j))],
        out_specs=pl.BlockSpec((None, 8, tn), lambda l, j: (l, 0, j)),
        out_shape=jax.ShapeDtypeStruct((depth, 8, nm), F32),
        compiler_params=_cparams(("parallel", "parallel")),
        name="mods",
    )(cvec, w_mod, b_mod.reshape(depth, 1, nm))


def _inproj(x, g, shift, scale, w, layer, rows_per_group):
    m, dm = x.shape
    n = w.shape[2]
    tm = TM_PROJ
    grp = lambda i: (i * tm // rows_per_group, 0, 0)
    return pl.pallas_call(
        _inproj_kernel,
        grid=(m // tm,),
        in_specs=[pl.BlockSpec((tm, dm), lambda i: (i, 0)),
                  _resident((1, dm)),
                  pl.BlockSpec((1, 1, dm), grp),
                  pl.BlockSpec((1, 1, dm), grp),
                  _resident((dm, n), layer)],
        out_specs=pl.BlockSpec((tm, n), lambda i: (i, 0)),
        out_shape=jax.ShapeDtypeStruct((m, n), F32),
        compiler_params=_cparams(("parallel",)),
        name="inproj",
    )(x, g, shift, scale, w)


def _outproj(ys, w, layer, x, gate, g, shift, scale, rows_per_group, colmajor):
    m, dm = x.shape
    tm = min(TM_OUT, rows_per_group)
    grp = lambda i: (i * tm // rows_per_group, 0, 0)
    yspec = pl.BlockSpec((tm, GROUP_W), lambda i: (i, 0))
    if colmajor:
        assert tm == ROWS_PER_TILE * GRID_W
        tiles = rows_per_group // tm
        cspec = pl.BlockSpec((1, GRID_W, ROWS_PER_TILE, GROUP_W), lambda i: (i // tiles, 0, i % tiles, 0))
    else:
        cspec = yspec
    return pl.pallas_call(
        functools.partial(_outproj_kernel, colmajor),
        grid=(m // tm,),
        in_specs=[yspec, cspec, yspec, cspec,
                  _resident((4 * GROUP_W, dm), layer),
                  pl.BlockSpec((tm, dm), lambda i: (i, 0)),
                  pl.BlockSpec((1, 1, dm), grp),
                  _resident((1, dm)),
                  pl.BlockSpec((1, 1, dm), grp),
                  pl.BlockSpec((1, 1, dm), grp)],
        out_specs=[pl.BlockSpec((tm, dm), lambda i: (i, 0)), pl.BlockSpec((tm, dm), lambda i: (i, 0))],
        out_shape=[jax.ShapeDtypeStruct((m, dm), F32), jax.ShapeDtypeStruct((m, dm), BF16)],
        compiler_params=_cparams(("parallel",)),
        name="outproj",
    )(*ys, w, x, gate, g, shift, scale)


def _inproj_colmajor(x, g, shift, scale, w, layer, bsz, rows):
    dm = x.shape[1]
    n = w.shape[2]
    tiles = GRID_W // COLS_PER_TILE
    tm = COLS_PER_TILE * rows
    grp = lambda i: (i // tiles, 0, 0)
    return pl.pallas_call(
        _inproj_colmajor_kernel,
        grid=(bsz * tiles,),
        in_specs=[pl.BlockSpec((1, rows, COLS_PER_TILE, dm), lambda i: (i // tiles, 0, i % tiles, 0)),
                  _resident((1, dm)),
                  pl.BlockSpec((1, 1, dm), grp),
                  pl.BlockSpec((1, 1, dm), grp),
                  _resident((dm, n), layer)],
        out_specs=pl.BlockSpec((tm, n), lambda i: (i, 0)),
        out_shape=jax.ShapeDtypeStruct((bsz * rows * GRID_W, n), F32),
        scratch_shapes=[pltpu.VMEM((tm, dm), F32)],
        compiler_params=_cparams(("parallel",)),
        name="inproj_colmajor",
    )(x.reshape(bsz, rows, GRID_W, dm), g, shift, scale, w)


def _ffn_up(h, wg, wu, layer):
    m, dm = h.shape
    dff = wg.shape[2]
    tm = min(TM_FFN_UP, m)
    tn = TN_FFN_UP
    return pl.pallas_call(
        _ffn_up_kernel,
        grid=(m // tm, dff // tn),
        in_specs=[pl.BlockSpec((tm, dm), lambda i, j: (i, 0)),
                  pl.BlockSpec((None, dm, tn), lambda i, j: (layer, 0, j)),
                  pl.BlockSpec((None, dm, tn), lambda i, j: (layer, 0, j))],
        out_specs=pl.BlockSpec((tm, tn), lambda i, j: (i, j)),
        out_shape=jax.ShapeDtypeStruct((m, dff), BF16),
        compiler_params=_cparams(("parallel", "arbitrary")),
        name="ffn_up",
    )(h, wg, wu)


def _ffn_down(a, w, layer, x, gate, final_g, final, rows_per_group):
    m, dm = x.shape
    dff = a.shape[1]
    tm = TM_FFN_DOWN
    grp = lambda i: (i * tm // rows_per_group, 0, 0)
    return pl.pallas_call(
        functools.partial(_ffn_down_kernel, final),
        grid=(m // tm,),
        in_specs=[pl.BlockSpec((tm, dff), lambda i: (i, 0)),
                  _resident((dff, dm), layer),
                  pl.BlockSpec((tm, dm), lambda i: (i, 0)),
                  pl.BlockSpec((1, 1, dm), grp),
                  _resident((1, dm))],
        out_specs=pl.BlockSpec((tm, dm), lambda i: (i, 0)),
        out_shape=jax.ShapeDtypeStruct((m, dm), F32),
        compiler_params=_cparams(("parallel",)),
        name="ffn_down",
    )(a, w, x, gate, final_g)


def _pad_cols(a, width):
    return jnp.pad(a, ((0, 0), (0, width - a.shape[1])))


def _split_w_in(w):
    ssd, ml, lru, gla = jnp.split(w, [1552, 1552 + 2064, 1552 + 2064 + 1024], axis=2)
    ssd_z, ssd_xbc, ssd_dt = ssd[..., :512], ssd[..., 512:1536], ssd[..., 1536:]
    ml_qkvo, ml_g = ml[..., :2048], ml[..., 2048:]
    lru_gate, lru_x = lru[..., :512], lru[..., 512:]
    gla_qkvr, gla_g1 = gla[..., :1536], gla[..., 1536:]
    pad = lambda a: jnp.pad(a, ((0, 0), (0, 0), (0, 128 - a.shape[2])))
    w_row = jnp.concatenate([ssd_xbc, lru_x, lru_gate, ssd_z, pad(ssd_dt)], axis=2)
    w_col = jnp.concatenate([ml_qkvo, gla_qkvr, pad(ml_g), pad(gla_g1)], axis=2)
    return w_row.astype(BF16), w_col.astype(BF16)


def _row128(a):
    return _pad_cols(a.reshape(1, -1).astype(F32), 128)


def _block_diag(w):
    nb, bi, bj = w.shape
    eye = jnp.eye(nb, dtype=w.dtype)
    return (eye[:, None, :, None] * w[:, :, None, :]).reshape(nb * bi, nb * bj)


def kernel(x, c, ctx, c_ctx, norm1_g, norm2_g, w_mod, b_mod, w_in, w_out, ssd_conv_w, ssd_conv_b, ssd_dt_bias, ssd_a_log, ssd_d, ssd_norm_g, ml_conv_w, ml_conv_b, ml_igate_b, ml_fgate_b, ml_norm_g, lru_conv_w, lru_conv_b, lru_wa, lru_ba, lru_wx, lru_bx, lru_lambda, gla_wg2, gla_bg, gla_norm_g, w_gate, w_up, w_down, final_g):
    bsz, length, dm = x.shape
    lctx = ctx.shape[1]
    depth = w_in.shape[0]
    rows = length // GRID_W
    assert dm == D_MODEL and length % SSD_CHUNK == 0 and lctx % SSD_CHUNK == 0
    assert length % LRU_CHUNK == 0 and lctx % LRU_CHUNK == 0 and lctx % ML_CHUNK == 0
    assert rows == ML_CHUNK == GLA_CHUNK

    cvec = jnp.concatenate([c, c_ctx[None, :], jnp.zeros((8 - bsz - 1, dm), F32)], axis=0)
    mods = _mods(cvec, w_mod, b_mod)

    xl = x.reshape(bsz * length, dm)
    xt = ctx.reshape(bsz * lctx, dm)
    row = lambda a: a.reshape(1, -1).astype(F32)
    w_row, w_col = _split_w_in(w_in)
    w_out_b = w_out.astype(BF16)
    w_down_b = w_down.astype(BF16)

    for i in range(depth):
        need_ctx = i < depth - 1
        m_l = [mods[i, :bsz, k * dm:(k + 1) * dm].reshape(bsz, 1, dm) for k in range(N_MOD)]
        m_t = [mods[i, bsz:bsz + 1, k * dm:(k + 1) * dm].reshape(1, 1, dm) for k in range(N_MOD)]
        g1 = row(norm1_g[i])
        ul_row = _inproj(xl, g1, m_l[0], m_l[1], w_row, i, length).reshape(bsz, length, N_ROW)
        ul_col = _inproj_colmajor(xl, g1, m_l[0], m_l[1], w_col, i, bsz, rows).reshape(bsz, length, N_COL)
        ut_row = _inproj(xt, g1, m_t[0], m_t[1], w_row, i, bsz * lctx).reshape(bsz, lctx, N_ROW)
        ut_col = _inproj(xt, g1, m_t[0], m_t[1], w_col, i, bsz * lctx).reshape(bsz, lctx, N_COL)

        ssd_params = [ssd_conv_w[i], row(ssd_conv_b[i]), _row128(ssd_dt_bias[i]), _row128(ssd_a_log[i]),
                      row(jnp.repeat(ssd_d[i], GROUP_W // SSD_HEADS)), row(ssd_norm_g[i])]
        ssd_bwd = [("u", 1024, OFF_SSD_XBC, True), ("u", 128, OFF_SSD_DT, False)]
        ssd_fwd = [("saved", 1024, 0, False), ("u", 128, OFF_SSD_DT, False), ("u", 512, OFF_SSD_Z, False)]
        ssd_zero = [jnp.zeros((bsz, SSD_HEADS // 2, 128, 128), F32)]
        ya_t, ya_l = _bidir_mixer(_ssd_kernel, ssd_bwd, ssd_fwd, 1024, SSD_CHUNK, ut_row, ul_row, ssd_zero,
                                  ssd_params, "ssd")

        ml_gb = _row128(jnp.concatenate([ml_igate_b[i].reshape(-1), ml_fgate_b[i].reshape(-1)]))
        ml_params = [ml_conv_w[i], row(ml_conv_b[i]), ml_gb, row(ml_norm_g[i])]
        ml_bwd = [("u", 512, OFF_ML_Q, True), ("u", 512, OFF_ML_K, True), ("u", 512, OFF_ML_V, False),
                  ("u", 128, OFF_ML_G, False)]
        ml_fwd = [("saved", 1024, 0, False), ("u", 512, OFF_ML_V, False), ("u", 128, OFF_ML_G, False),
                  ("u", 512, OFF_ML_O, False)]
        ml_zero = [jnp.zeros((bsz, ML_HEADS, 128, 128), F32), jnp.zeros((bsz, 2 * ML_HEADS, 128), F32)]
        yb_t, yb_l = _bidir_mixer(_mlstm_kernel, ml_bwd, ml_fwd, 1024, ML_CHUNK, ut_col, ul_col, ml_zero,
                                  ml_params, "mlstm", F32)

        lru_w = jnp.stack([jnp.concatenate([_block_diag(lru_wa[i, dd]), _block_diag(lru_wx[i, dd])], axis=1)
                           for dd in range(2)]).astype(BF16)
        lru_bias = jnp.concatenate([lru_ba[i], lru_bx[i]], axis=1).reshape(2, 1, 2 * GROUP_W)
        lru_params = [lru_conv_w[i], row(lru_conv_b[i]), lru_w, lru_bias, lru_lambda[i].reshape(2, 1, GROUP_W)]
        lru_bwd = [("u", 512, OFF_LRU_X, True)]
        lru_fwd = [("saved", 512, 0, False), ("u", 512, OFF_LRU_GATE, False)]
        lru_zero = [jnp.zeros((bsz, 1, GROUP_W), F32)]
        lru_scratch = [pltpu.VMEM((bsz, GROUP_W // 128, LRU_CHUNK, 128), F32)] * 2
        yc_t, yc_l = _bidir_mixer(_lru_kernel, lru_bwd, lru_fwd, 512, LRU_CHUNK, ut_row, ul_row, lru_zero,
                                  lru_params, "lru", scratch=lru_scratch)

        wg = jnp.zeros((2, 128, GLA_HEADS * GLA_DK), F32)
        wg = wg.at[0, :GLA_RANK].set(gla_wg2[i, 0]).at[1, GLA_RANK:2 * GLA_RANK].set(gla_wg2[i, 1])
        gla_params = [wg.astype(BF16), gla_bg[i].reshape(2, 1, GLA_HEADS * GLA_DK), row(gla_norm_g[i])]
        gla_bwd = [("u", 512, OFF_GLA_QK, False), ("u", 512, OFF_GLA_V, False), ("u", 128, OFF_GLA_G1, False)]
        gla_fwd = gla_bwd + [("u", 512, OFF_GLA_R, False)]
        gla_zero = [jnp.zeros((bsz, GROUP_W, GLA_HEADS * GLA_DK), F32)]
        yd_t, yd_l = _bidir_mixer(_gla_kernel, gla_bwd, gla_fwd, 0, GLA_CHUNK, ut_col, ul_col, gla_zero,
                                  gla_params, "gla", F32)

        fg = row(final_g)

        def tail(xs, ys, m, rows_per_group, final, colmajor):
            cm = lambda y: y.reshape(bsz, GRID_W, rows, GROUP_W) if colmajor else y.reshape(-1, GROUP_W)
            ys = [ys[0].reshape(-1, GROUP_W), cm(ys[1]), ys[2].reshape(-1, GROUP_W), cm(ys[3])]
            xs, h2 = _outproj(ys, w_out_b, i, xs, m[2], row(norm2_g[i]), m[3], m[4], rows_per_group, colmajor)
            act = _ffn_up(h2, w_gate, w_up, i)
            return _ffn_down(act, w_down_b, i, xs, m[5], fg, final, rows_per_group)

        xl = tail(xl, [ya_l, yb_l, yc_l, yd_l], m_l, length, i == depth - 1, True)
        if need_ctx:
            xt = tail(xt, [ya_t, yb_t, yc_t, yd_t], m_t, bsz * lctx, False, False)
    return xl.reshape(bsz, length, dm)
```
